```python
import math
import jax
import jax.numpy as jnp
from jax import lax
import numpy as np

D_MODEL = 4096
BATCH = 1
SEQ = 8192
DEPTH = 2

N_EVEN = (DEPTH + 1) // 2
N_ODD = DEPTH // 2

A_WIDTH = D_MODEL // 2
A_VDIM = 128
A_HEADS = A_WIDTH // A_VDIM
A_QKDIM = A_VDIM // 2
Q_BLOCK = 128
POOL_WINDOWS = (2, 4, 8, 16)
B_WIDTH = D_MODEL - A_WIDTH
B_GROUPS = len(POOL_WINDOWS)
B_GROUP_DIM = B_WIDTH // B_GROUPS
C_KDIM = 128
C_HEADS = D_MODEL // C_KDIM
C_VDIM = D_MODEL // C_HEADS
C_CHUNK = 64
FF_DENSE = 256 * math.ceil(8 * D_MODEL / 3 / 256)
N_EXPERTS = 8
TOP_K = 2
FF_EXPERT = D_MODEL
DN_ALPHA = (2 * DEPTH) ** 0.25
DN_BETA = (8 * DEPTH) ** -0.25
LN_EPS = 1e-5
RMS_EPS = 1e-6
NEG_INF = -1e30

kernel_name = 'hybrid_diffattn_pool_hgrn2_moe_deepnorm'


def _post_norm_residual(x, y, gate, g, b):
    r = DN_ALPHA * x.astype(jnp.float32) + (1.0 + gate[:, None, :]) * y.astype(jnp.float32)
    mu = jnp.mean(r, axis=-1, keepdims=True)
    var = jnp.mean(jnp.square(r - mu), axis=-1, keepdims=True)
    return ((r - mu) * lax.rsqrt(var + LN_EPS) * g + b).astype(x.dtype)


def _rms_norm(x, g):
    xf = x.astype(jnp.float32)
    return xf * lax.rsqrt(jnp.mean(jnp.square(xf), axis=-1, keepdims=True) + RMS_EPS) * g


def _modulate(x, shift, scale):
    return (x * (1.0 + scale[:, None, :]) + shift[:, None, :]).astype(x.dtype)


def _swiglu(h, w_in, w_out):
    a, b = jnp.split(h @ w_in, 2, axis=-1)
    return (jax.nn.silu(a) * b) @ w_out


def _alibi_slopes(n_heads):
    return 2.0 ** (-8.0 * jnp.arange(1, n_heads + 1, dtype=jnp.float32) / n_heads)


def _diff_attention(q, k, v, lam):
    b_, s_, h_, _, dq = q.shape
    nb = s_ // Q_BLOCK
    slopes = _alibi_slopes(h_)
    scale = dq ** -0.5
    kf = k.astype(jnp.float32)
    vf = v.astype(jnp.float32)
    qb = q.astype(jnp.float32).reshape(b_, nb, Q_BLOCK, h_, 2, dq).transpose(1, 0, 2, 3, 4, 5)
    kpos = jnp.arange(s_)

    def block(args):
        i, qi = args
        qpos = i * Q_BLOCK + jnp.arange(Q_BLOCK)
        dist = qpos[:, None] - kpos[None, :]
        bias = -slopes[:, None, None] * dist.astype(jnp.float32)
        s = jnp.einsum('bqhmd,bkhmd->bhmqk', qi, kf) * scale + bias[None, :, None]
        s = jnp.where(dist >= 0, s, NEG_INF)
        p = jax.nn.softmax(s, axis=-1)
        w = p[:, :, 0] - lam * p[:, :, 1]
        return jnp.einsum('bhqk,bkhd->bqhd', w, vf)

    o = lax.map(block, (jnp.arange(nb), qb))
    return o.transpose(1, 0, 2, 3, 4).reshape(b_, s_, h_, vf.shape[-1])


def _multiscale_pool(u, pool_w, pool_scale):
    b_, s_, _ = u.shape
    uf = u.astype(jnp.float32).reshape(b_, s_, B_GROUPS, B_GROUP_DIM)
    csum = jnp.concatenate([jnp.zeros((b_, 1, B_GROUPS, B_GROUP_DIM), jnp.float32),
                            jnp.cumsum(uf, axis=1)], axis=1)
    t = jnp.arange(s_)[:, None]
    win = jnp.array(POOL_WINDOWS, dtype=jnp.int32)[None, :]
    start = jnp.maximum(t + 1 - win, 0)
    count = (t + 1 - start).astype(jnp.float32)
    lower = csum[:, start, jnp.arange(B_GROUPS)[None, :], :]
    mean = (csum[:, 1:] - lower) / count[None, :, :, None]
    y = jnp.einsum('bsgc,gce->bsge', mean - uf, pool_w.astype(jnp.float32))
    return y.reshape(b_, s_, B_WIDTH) * pool_scale.astype(jnp.float32)


def _even_mixer(h, w_in, lam_q1, lam_k1, lam_q2, lam_k2, subln_g, pool_w, pool_scale, w_out, lam_init):
    b_, s_, _ = h.shape
    proj = h @ w_in
    q, k, v, u = jnp.split(proj, [A_WIDTH, 2 * A_WIDTH, 3 * A_WIDTH], axis=-1)
    q = q.reshape(b_, s_, A_HEADS, 2, A_QKDIM)
    k = k.reshape(b_, s_, A_HEADS, 2, A_QKDIM)
    v = v.reshape(b_, s_, A_HEADS, A_VDIM)
    lam = (jnp.exp(jnp.sum(lam_q1.astype(jnp.float32) * lam_k1.astype(jnp.float32)))
           - jnp.exp(jnp.sum(lam_q2.astype(jnp.float32) * lam_k2.astype(jnp.float32))) + lam_init)
    o_a = _diff_attention(q, k, v, lam)
    o_a = (_rms_norm(o_a, subln_g) * (1.0 - lam_init)).reshape(b_, s_, A_WIDTH)
    o_b = _multiscale_pool(u, pool_w, pool_scale)
    return (jnp.concatenate([o_a, o_b], axis=-1) @ w_out).astype(h.dtype)


def _hgrn2_chunked(q, k, logf, v):
    b_, h_, s_, dk = q.shape
    dv = v.shape[-1]
    n = s_ // C_CHUNK

    def to_chunks(a):
        return a.reshape(b_, h_, n, C_CHUNK, a.shape[-1]).transpose(2, 0, 1, 3, 4)

    tri = jnp.tril(jnp.ones((C_CHUNK, C_CHUNK), dtype=bool))[:, :, None]

    def step(state, inp):
        qc, kc, gc, vc = inp
        bcum = jnp.cumsum(gc, axis=-2)
        diff = bcum[..., :, None, :] - bcum[..., None, :, :]
        decay = jnp.exp(jnp.where(tri, diff, -jnp.inf))
        attn = jnp.einsum('bhtk,bhsk,bhtsk->bhts', qc, kc, decay)
        o = (jnp.einsum('bhts,bhsv->bhtv', attn, vc)
             + jnp.einsum('bhtk,bhkv->bhtv', qc * jnp.exp(bcum), state))
        b_last = bcum[..., -1:, :]
        new_state = (jnp.exp(b_last)[..., 0, :, None] * state
                     + jnp.einsum('bhsk,bhsv->bhkv', kc * jnp.exp(b_last - bcum), vc))
        return new_state, o

    s0 = jnp.zeros((b_, h_, dk, dv), jnp.float32)
    _, o = lax.scan(step, s0, (to_chunks(q), to_chunks(k), to_chunks(logf), to_chunks(v)))
    return o.transpose(1, 2, 0, 3, 4).reshape(b_, h_, s_, dv)


def _odd_mixer(h, w_in, lb, gnorm_g, w_out):
    b_, s_, _ = h.shape
    q, f, i, g = jnp.split(h @ w_in, 4, axis=-1)
    q = jax.nn.silu(q.astype(jnp.float32))
    forget = lb + (1.0 - lb) * jax.nn.sigmoid(f.astype(jnp.float32))
    k = 1.0 - forget
    logf = jnp.log(forget)

    def heads(a):
        return a.reshape(b_, s_, C_HEADS, -1).transpose(0, 2, 1, 3)

    o = _hgrn2_chunked(heads(q), heads(k), heads(logf), heads(i.astype(jnp.float32)))
    o = o.transpose(0, 2, 1, 3)
    o = _rms_norm(o, gnorm_g) * jax.nn.silu(g.astype(jnp.float32).reshape(b_, s_, C_HEADS, C_VDIM))
    return (o.reshape(b_, s_, D_MODEL) @ w_out).astype(h.dtype)


def _moe_swiglu(h, router_w, w_in, w_out):
    b_, s_, d_ = h.shape
    t = h.reshape(b_ * s_, d_)
    logits = (t @ router_w).astype(jnp.float32)
    top_v, top_i = lax.top_k(logits, TOP_K)
    top_g = jax.nn.softmax(top_v, axis=-1)
    gate = jnp.sum(jax.nn.one_hot(top_i, N_EXPERTS, dtype=jnp.float32) * top_g[..., None], axis=1)
    y = jnp.zeros((b_ * s_, d_), jnp.float32)
    for e in range(N_EXPERTS):
        y = y + gate[:, e:e + 1] * _swiglu(t, w_in[e], w_out[e]).astype(jnp.float32)
    return y.reshape(b_, s_, d_).astype(h.dtype)


def setup_inputs(seed: int = 0) -> dict:
    key = jax.random.key(seed)
    ks = iter(jax.random.split(key, 40))

    def nrm(shape, scale):
        return jax.random.normal(next(ks), shape, jnp.float32) * scale

    D = D_MODEL
    sD = D ** -0.5
    x = nrm((BATCH, SEQ, D), 1.0)
    c = nrm((BATCH, D), 1.0)
    ada_w = nrm((DEPTH, D, 6 * D), 0.1 * sD)
    ada_b = nrm((DEPTH, 6 * D), 0.02)
    ln_g = 1.0 + nrm((DEPTH, 2, D), 0.02)
    ln_b = nrm((DEPTH, 2, D), 0.02)
    even_w_in = jnp.concatenate([nrm((N_EVEN, D, A_WIDTH), sD),
                                 nrm((N_EVEN, D, A_WIDTH), sD),
                                 nrm((N_EVEN, D, A_WIDTH), sD * DN_BETA),
                                 nrm((N_EVEN, D, B_WIDTH), sD)], axis=-1)
    lam_q1 = nrm((N_EVEN, A_QKDIM), 0.1)
    lam_k1 = nrm((N_EVEN, A_QKDIM), 0.1)
    lam_q2 = nrm((N_EVEN, A_QKDIM), 0.1)
    lam_k2 = nrm((N_EVEN, A_QKDIM), 0.1)
    subln_g = 1.0 + nrm((N_EVEN, A_VDIM), 0.02)
    pool_w = nrm((N_EVEN, B_GROUPS, B_GROUP_DIM, B_GROUP_DIM), B_GROUP_DIM ** -0.5 * DN_BETA)
    pool_scale = 1.0 + nrm((N_EVEN, B_WIDTH), 0.02)
    even_w_out = nrm((N_EVEN, D, D), sD * DN_BETA)
    ffn_w_in = nrm((N_EVEN, D, 2 * FF_DENSE), sD)
    ffn_w_out = nrm((N_EVEN, FF_DENSE, D), FF_DENSE ** -0.5 * DN_BETA)
    odd_w_in = jnp.concatenate([nrm((N_ODD, D, D), sD),
                                nrm((N_ODD, D, D), sD),
                                nrm((N_ODD, D, D), sD * DN_BETA),
                                nrm((N_ODD, D, D), sD)], axis=-1)
    lb_raw = nrm((DEPTH, D), 0.5)
    gnorm_g = 1.0 + nrm((N_ODD, C_VDIM), 0.02)
    odd_w_out = nrm((N_ODD, D, D), sD * DN_BETA)
    router_w = nrm((N_ODD, D, N_EXPERTS), sD)
    exp_w_in = nrm((N_ODD, N_EXPERTS, D, 2 * FF_EXPERT), sD)
    exp_w_out = nrm((N_ODD, N_EXPERTS, FF_EXPERT, D), FF_EXPERT ** -0.5 * DN_BETA)
    return {'x': x, 'c': c, 'ada_w': ada_w, 'ada_b': ada_b, 'ln_g': ln_g, 'ln_b': ln_b,
            'even_w_in': even_w_in, 'lam_q1': lam_q1, 'lam_k1': lam_k1, 'lam_q2': lam_q2,
            'lam_k2': lam_k2, 'subln_g': subln_g, 'pool_w': pool_w, 'pool_scale': pool_scale,
            'even_w_out': even_w_out, 'ffn_w_in': ffn_w_in, 'ffn_w_out': ffn_w_out,
            'odd_w_in': odd_w_in, 'lb_raw': lb_raw, 'gnorm_g': gnorm_g, 'odd_w_out': odd_w_out,
            'router_w': router_w, 'exp_w_in': exp_w_in, 'exp_w_out': exp_w_out}


def reference(x, c, ada_w, ada_b, ln_g, ln_b, even_w_in, lam_q1, lam_k1, lam_q2, lam_k2,
              subln_g, pool_w, pool_scale, even_w_out, ffn_w_in, ffn_w_out, odd_w_in, lb_raw,
              gnorm_g, odd_w_out, router_w, exp_w_in, exp_w_out):
    lb_all = jnp.cumsum(jax.nn.softmax(lb_raw.astype(jnp.float32), axis=0), axis=0)
    lb_all = lb_all - lb_all[0]
    c_act = jax.nn.silu(c.astype(jnp.float32))
    for l in range(DEPTH):
        j = l // 2
        mod = c_act @ ada_w[l] + ada_b[l]
        sh1, sc1, g1, sh2, sc2, g2 = jnp.split(mod, 6, axis=-1)
        h = _modulate(x, sh1, sc1)
        if l % 2 == 0:
            lam_init = 0.8 - 0.6 * math.exp(-0.3 * l)
            y = _even_mixer(h, even_w_in[j], lam_q1[j], lam_k1[j], lam_q2[j], lam_k2[j],
                            subln_g[j], pool_w[j], pool_scale[j], even_w_out[j], lam_init)
        else:
            y = _odd_mixer(h, odd_w_in[j], lb_all[l], gnorm_g[j], odd_w_out[j])
        x = _post_norm_residual(x, y, g1, ln_g[l, 0], ln_b[l, 0])
        h = _modulate(x, sh2, sc2)
        if l % 2 == 0:
            y = _swiglu(h, ffn_w_in[j], ffn_w_out[j])
        else:
            y = _moe_swiglu(h, router_w[j], exp_w_in[j], exp_w_out[j])
        x = _post_norm_residual(x, y, g2, ln_g[l, 1], ln_b[l, 1])
    return x
```

```python
import functools
import math

import numpy as np
import jax
import jax.numpy as jnp
from jax import lax
from jax.experimental import pallas as pl
from jax.experimental.pallas import tpu as pltpu

F32 = jnp.float32
BF16 = jnp.bfloat16

A_VDIM = 128
A_QKDIM = 64
POOL_WINDOWS = (2, 4, 8, 16)
POOL_HALO = 16
C_KDIM = 128
C_VDIM = 128
HGRN_CHUNK = 64
N_EXPERTS = 8
LN_EPS = 1e-5
RMS_EPS = 1e-6
NEG_INF = -1e30

LANES = 128
VMEM_LIMIT_BYTES = 56 * 1024 * 1024


def _cparams(*sem):
    return pltpu.CompilerParams(dimension_semantics=sem, vmem_limit_bytes=VMEM_LIMIT_BYTES)


def _silu(x):
    return x * (1.0 / (1.0 + jnp.exp(-x)))


def _dot(a, b):
    return jnp.dot(a, b, preferred_element_type=F32)


def _dot_t(a, b):
    return lax.dot_general(a, b, (((1,), (1,)), ((), ())), preferred_element_type=F32)


def _ada_body(c_ref, w_ref, b_ref, o_ref):
    ca = _silu(c_ref[...]).astype(BF16)
    o_ref[...] = _dot(ca, w_ref[...].astype(BF16)) + b_ref[...]


def _ada_mod(c, ada_w, ada_b, *, tn=1024):
    depth, d, n = ada_w.shape
    c8 = jnp.broadcast_to(c.astype(F32), (8, d))
    out = pl.pallas_call(
        _ada_body,
        grid=(depth, n // tn),
        in_specs=[pl.BlockSpec((8, d), lambda l, j: (0, 0)),
                  pl.BlockSpec((None, d, tn), lambda l, j: (l, 0, j)),
                  pl.BlockSpec((None, 1, tn), lambda l, j: (l, 0, j))],
        out_specs=pl.BlockSpec((None, 8, tn), lambda l, j: (l, 0, j)),
        out_shape=jax.ShapeDtypeStruct((depth, 8, n), F32),
        compiler_params=_cparams("arbitrary", "arbitrary"),
        name="ada_mod",
    )(c8, ada_w, ada_b.reshape(depth, 1, n))
    return out[:, 0, :]


def _modulate_body(x_ref, sc_ref, sh_ref, o_ref):
    o_ref[...] = (x_ref[...] * (1.0 + sc_ref[...]) + sh_ref[...]).astype(o_ref.dtype)


def _modulate(x, scale, shift, *, tm=512):
    s, d = x.shape
    vec = pl.BlockSpec((1, d), lambda i: (0, 0))
    return pl.pallas_call(
        _modulate_body,
        grid=(s // tm,),
        in_specs=[pl.BlockSpec((tm, d), lambda i: (i, 0)), vec, vec],
        out_specs=pl.BlockSpec((tm, d), lambda i: (i, 0)),
        out_shape=jax.ShapeDtypeStruct((s, d), BF16),
        compiler_params=_cparams("arbitrary"),
        name="modulate",
    )(x, scale.reshape(1, d), shift.reshape(1, d))


def _layer_norm_rows(r, g, b):
    mu = jnp.mean(r, axis=-1, keepdims=True)
    rc = r - mu
    var = jnp.mean(rc * rc, axis=-1, keepdims=True)
    return rc * lax.rsqrt(var + LN_EPS) * g + b


def _postnorm_body(x_ref, y_ref, gate_ref, g_ref, b_ref, sc_ref, sh_ref, xo_ref, ho_ref, *, alpha):
    r = alpha * x_ref[...] + (1.0 + gate_ref[...]) * y_ref[...].astype(F32)
    xn = _layer_norm_rows(r, g_ref[...], b_ref[...])
    xo_ref[...] = xn
    ho_ref[...] = (xn * (1.0 + sc_ref[...]) + sh_ref[...]).astype(ho_ref.dtype)


def _postnorm(x, y, gate, g, b, nscale, nshift, *, alpha, tm=256):
    s, d = x.shape
    row = pl.BlockSpec((tm, d), lambda i: (i, 0))
    vec = pl.BlockSpec((1, d), lambda i: (0, 0))
    return pl.pallas_call(
        functools.partial(_postnorm_body, alpha=alpha),
        grid=(s // tm,),
        in_specs=[row, row, vec, vec, vec, vec, vec],
        out_specs=[row, row],
        out_shape=[jax.ShapeDtypeStruct((s, d), F32), jax.ShapeDtypeStruct((s, d), BF16)],
        compiler_params=_cparams("arbitrary"),
        name="postnorm",
    )(x, y, gate.reshape(1, d), g.reshape(1, d), b.reshape(1, d), nscale.reshape(1, d), nshift.reshape(1, d))


def _proj_body(a_ref, w_ref, *rest, epilogue, tn, q_cols, q_scale):
    o_ref = rest[-1]
    acc = _dot(a_ref[...], w_ref[...])
    if epilogue == "silu":
        acc = _silu(acc)
    elif epilogue == "forget":
        lb = rest[0][...]
        acc = lb + (1.0 - lb) * (1.0 / (1.0 + jnp.exp(-acc)))
    elif epilogue == "qscale":
        acc = acc * jnp.where(pl.program_id(1) * tn < q_cols, q_scale, 1.0)
    o_ref[...] = acc.astype(o_ref.dtype)


def _proj(a, w, *, col_off, n_cols, out_dtype, epilogue="id", lb=None, q_cols=0, q_scale=1.0,
          tm=1024, tn=512, name="proj"):
    m, k = a.shape
    joff = col_off // tn
    in_specs = [pl.BlockSpec((tm, k), lambda i, j: (i, 0)),
                pl.BlockSpec((k, tn), lambda i, j: (0, j + joff))]
    args = [a, w]
    if epilogue == "forget":
        in_specs.append(pl.BlockSpec((1, tn), lambda i, j: (0, j)))
        args.append(lb.reshape(1, n_cols))
    body = functools.partial(_proj_body, epilogue=epilogue, tn=tn, q_cols=q_cols, q_scale=q_scale)
    return pl.pallas_call(
        body,
        grid=(m // tm, n_cols // tn),
        in_specs=in_specs,
        out_specs=pl.BlockSpec((tm, tn), lambda i, j: (i, j)),
        out_shape=jax.ShapeDtypeStruct((m, n_cols), out_dtype),
        compiler_params=_cparams("arbitrary", "arbitrary"),
        name=name,
    )(*args)


def _swiglu_in_body(a_ref, wa_ref, wb_ref, o_ref):
    a = a_ref[...]
    ga = _dot(a, wa_ref[...])
    gb = _dot(a, wb_ref[...])
    o_ref[...] = (_silu(ga) * gb).astype(o_ref.dtype)


def _swiglu_in(a, w, *, tm=1024, tn=512):
    m, k = a.shape
    half = w.shape[1] // 2
    hoff = half // tn
    return pl.pallas_call(
        _swiglu_in_body,
        grid=(m // tm, half // tn),
        in_specs=[pl.BlockSpec((tm, k), lambda i, j: (i, 0)),
                  pl.BlockSpec((k, tn), lambda i, j: (0, j)),
                  pl.BlockSpec((k, tn), lambda i, j: (0, j + hoff))],
        out_specs=pl.BlockSpec((tm, tn), lambda i, j: (i, j)),
        out_shape=jax.ShapeDtypeStruct((m, half), BF16),
        compiler_params=_cparams("arbitrary", "arbitrary"),
        name="swiglu_in",
    )(a, w, w)


def _mm_acc_body(a_ref, w_ref, o_ref, acc_ref, *, nk):
    kk = pl.program_id(2)
    part = _dot(a_ref[...], w_ref[...])

    @pl.when(kk == 0)
    def _():
        acc_ref[...] = part

    @pl.when(kk > 0)
    def _():
        acc_ref[...] += part

    @pl.when(kk == nk - 1)
    def _():
        o_ref[...] = acc_ref[...].astype(o_ref.dtype)


def _mm_acc(a, w, *, tm=1024, tn=1024, tk, out_dtype=F32):
    m, k = a.shape
    n = w.shape[1]
    nk = k // tk
    return pl.pallas_call(
        functools.partial(_mm_acc_body, nk=nk),
        grid=(m // tm, n // tn, nk),
        in_specs=[pl.BlockSpec((tm, tk), lambda i, j, kk: (i, kk)),
                  pl.BlockSpec((tk, tn), lambda i, j, kk: (kk, j))],
        out_specs=pl.BlockSpec((tm, tn), lambda i, j, kk: (i, j)),
        out_shape=jax.ShapeDtypeStruct((m, n), out_dtype),
        scratch_shapes=[pltpu.VMEM((tm, tn), F32)],
        compiler_params=_cparams("arbitrary", "arbitrary", "arbitrary"),
        name="mm_acc",
    )(a, w)


def _attn_body(qi_tab, kv_tab, slopes, q_ref, k_ref, v_ref, lq1_ref, lk1_ref, lq2_ref, lk2_ref, g_ref,
               o_ref, q1_s, q2_s, m1_s, l1_s, a1_s, m2_s, l2_s, a2_s, boff_s, bdiag_s, *, t_blk, lam_init):
    h = pl.program_id(0)
    t = pl.program_id(1)
    qi = qi_tab[t]
    kv = kv_tab[t]
    slope = slopes[h]

    @pl.when(t == 0)
    def _():
        row = lax.broadcasted_iota(jnp.int32, (t_blk, t_blk), 0)
        col = lax.broadcasted_iota(jnp.int32, (t_blk, t_blk), 1)
        d = (col - row).astype(F32) * slope
        boff_s[...] = d
        bdiag_s[...] = jnp.where(col <= row, d, NEG_INF)

    @pl.when(kv == 0)
    def _():
        q = q_ref[...]
        lane = lax.broadcasted_iota(jnp.int32, q.shape, 1)
        zero = jnp.zeros_like(q)
        q1_s[...] = jnp.where(lane < A_QKDIM, q, zero)
        q2_s[...] = jnp.where(lane >= A_QKDIM, q, zero)
        for m_s, l_s, a_s in ((m1_s, l1_s, a1_s), (m2_s, l2_s, a2_s)):
            m_s[...] = jnp.full(m_s.shape, -jnp.inf, F32)
            l_s[...] = jnp.zeros(l_s.shape, F32)
            a_s[...] = jnp.zeros(a_s.shape, F32)

    def step(bias_ref):
        k = k_ref[...]
        v = v_ref[...]
        c = slope * ((kv - qi) * t_blk).astype(F32)
        for q_s, m_s, l_s, a_s in ((q1_s, m1_s, l1_s, a1_s), (q2_s, m2_s, l2_s, a2_s)):
            s = _dot_t(q_s[...], k) + bias_ref[...]
            m_prev = m_s[...]
            m_new = jnp.maximum(m_prev, jnp.max(s, axis=-1, keepdims=True) + c)
            p = jnp.exp(s - (m_new - c))
            alpha = jnp.exp(m_prev - m_new)
            l_s[...] = alpha * l_s[...] + jnp.sum(p, axis=-1, keepdims=True)
            a_s[...] = alpha * a_s[...] + _dot(p.astype(BF16), v)
            m_s[...] = m_new

    @pl.when(kv < qi)
    def _():
        step(boff_s)

    @pl.when(kv == qi)
    def _():
        step(bdiag_s)
        lam = (jnp.exp(jnp.sum(lq1_ref[...] * lk1_ref[...], axis=-1, keepdims=True))
               - jnp.exp(jnp.sum(lq2_ref[...] * lk2_ref[...], axis=-1, keepdims=True)) + lam_init)
        o = a1_s[...] / l1_s[...] - lam * (a2_s[...] / l2_s[...])
        ms = jnp.mean(o * o, axis=-1, keepdims=True)
        o_ref[...] = (o * lax.rsqrt(ms + RMS_EPS) * g_ref[...] * (1.0 - lam_init)).astype(o_ref.dtype)


def _diff_attention(qkvu, lam_q1, lam_k1, lam_q2, lam_k2, subln_g, *, n_heads, lam_init, t_blk=512):
    s = qkvu.shape[0]
    nq = s // t_blk
    pairs = [(i, j) for i in range(nq) for j in range(i + 1)]
    qi_tab = jnp.asarray([p[0] for p in pairs], jnp.int32)
    kv_tab = jnp.asarray([p[1] for p in pairs], jnp.int32)
    slopes = jnp.asarray([2.0 ** (-8.0 * (i + 1) / n_heads) for i in range(n_heads)], F32)
    hd = A_VDIM
    small = pl.BlockSpec((1, A_QKDIM), lambda h, t, qt, kt, sl: (0, 0))
    grid_spec = pltpu.PrefetchScalarGridSpec(
        num_scalar_prefetch=3,
        grid=(n_heads, len(pairs)),
        in_specs=[pl.BlockSpec((t_blk, hd), lambda h, t, qt, kt, sl: (qt[t], h)),
                  pl.BlockSpec((t_blk, hd), lambda h, t, qt, kt, sl: (kt[t], n_heads + h)),
                  pl.BlockSpec((t_blk, hd), lambda h, t, qt, kt, sl: (kt[t], 2 * n_heads + h)),
                  small, small, small, small,
                  pl.BlockSpec((1, hd), lambda h, t, qt, kt, sl: (0, 0))],
        out_specs=pl.BlockSpec((t_blk, hd), lambda h, t, qt, kt, sl: (qt[t], h)),
        scratch_shapes=[pltpu.VMEM((t_blk, hd), BF16), pltpu.VMEM((t_blk, hd), BF16),
                        pltpu.VMEM((t_blk, 1), F32), pltpu.VMEM((t_blk, 1), F32), pltpu.VMEM((t_blk, hd), F32),
                        pltpu.VMEM((t_blk, 1), F32), pltpu.VMEM((t_blk, 1), F32), pltpu.VMEM((t_blk, hd), F32),
                        pltpu.VMEM((t_blk, t_blk), F32), pltpu.VMEM((t_blk, t_blk), F32)])
    return pl.pallas_call(
        functools.partial(_attn_body, t_blk=t_blk, lam_init=lam_init),
        grid_spec=grid_spec,
        out_shape=jax.ShapeDtypeStruct((s, n_heads * hd), BF16),
        compiler_params=_cparams("arbitrary", "arbitrary"),
        name="diff_attention",
    )(qi_tab, kv_tab, slopes, qkvu, qkvu, qkvu,
      lam_q1.reshape(1, -1), lam_k1.reshape(1, -1), lam_q2.reshape(1, -1), lam_k2.reshape(1, -1),
      subln_g.reshape(1, -1))


def _pool_body(ucur_ref, uprev_ref, w_ref, sc_ref, o_ref, *, t_blk):
    g = pl.program_id(0)
    i = pl.program_id(1)
    win = jnp.left_shift(2, g)
    row = lax.broadcasted_iota(jnp.int32, (t_blk, t_blk), 0)
    col = lax.broadcasted_iota(jnp.int32, (t_blk, t_blk), 1)
    d = row - col
    band = jnp.where(jnp.logical_and(d >= 0, d < win), 1.0, 0.0).astype(BF16)
    rowp = lax.broadcasted_iota(jnp.int32, (t_blk, POOL_HALO), 0)
    colp = lax.broadcasted_iota(jnp.int32, (t_blk, POOL_HALO), 1)
    dp = rowp + POOL_HALO - colp
    bandp = jnp.where(jnp.logical_and(dp < win, i > 0), 1.0, 0.0).astype(BF16)
    u = ucur_ref[...]
    usum = _dot(band, u) + _dot(bandp, uprev_ref[...])
    tpos = i * t_blk + lax.broadcasted_iota(jnp.int32, (t_blk, 1), 0)
    cnt = jnp.minimum(tpos + 1, win).astype(F32)
    dev = usum / cnt - u.astype(F32)
    y = _dot(dev.astype(BF16), w_ref[...]) * sc_ref[...]
    o_ref[...] = y.astype(o_ref.dtype)


def _multiscale_pool(qkvu, pool_w, pool_scale, *, col_off, t_blk=256):
    s = qkvu.shape[0]
    ng, gd, _ = pool_w.shape
    goff = col_off // gd
    hb = t_blk // POOL_HALO
    return pl.pallas_call(
        functools.partial(_pool_body, t_blk=t_blk),
        grid=(ng, s // t_blk),
        in_specs=[pl.BlockSpec((t_blk, gd), lambda g, i: (i, goff + g)),
                  pl.BlockSpec((POOL_HALO, gd), lambda g, i: (jnp.maximum(i * hb - 1, 0), goff + g)),
                  pl.BlockSpec((None, gd, gd), lambda g, i: (g, 0, 0)),
                  pl.BlockSpec((1, gd), lambda g, i: (0, g))],
        out_specs=pl.BlockSpec((t_blk, gd), lambda g, i: (i, g)),
        out_shape=jax.ShapeDtypeStruct((s, ng * gd), BF16),
        compiler_params=_cparams("arbitrary", "arbitrary"),
        name="multiscale_pool",
    )(qkvu, qkvu, pool_w, pool_scale.reshape(1, -1))


def _hgrn_tables():
    c = HGRN_CHUNK
    idx = np.arange(c)
    mats, masks = [], []
    h = c // 2
    while h >= 1:
        upper = (idx % (2 * h)) >= h
        e = idx - (idx % (2 * h)) + h - 1
        u = idx[None, :]
        pq = (upper[:, None] & (u > e[:, None]) & (u <= idx[:, None]))
        pk = ((~upper)[:, None] & (u > idx[:, None]) & (u <= e[:, None]))
        mats += [pq, pk]
        masks.append(upper[:, None] & (~upper)[None, :] & ((idx[:, None] // (2 * h)) == (idx[None, :] // (2 * h))))
        h //= 2
    masks.append(idx[:, None] == idx[None, :])
    mats.append(idx[None, :] <= idx[:, None])
    mats.append(idx[None, :] > idx[:, None])
    return (np.concatenate(mats, axis=0).astype(np.float32),
            np.stack(masks, axis=0).astype(np.float32))


def _hgrn_body(q_ref, f_ref, v_ref, gs_ref, p_ref, mask_ref, gn_ref, o_ref, st_ref, *, t_blk):
    c = HGRN_CHUNK
    n_lvl = mask_ref.shape[0] - 1

    @pl.when(pl.program_id(1) == 0)
    def _():
        st_ref[...] = jnp.zeros(st_ref.shape, F32)

    def chunk(ci, carry):
        rows = pl.ds(pl.multiple_of(ci * c, c), c)
        f = f_ref[rows, :]
        lf = jnp.log(f)
        kk = 1.0 - f
        hi = lf.astype(BF16)
        mid = (lf - hi.astype(F32)).astype(BF16)
        d2 = _dot(p_ref[...], jnp.concatenate([hi, mid], axis=1))
        e = jnp.exp(d2[:, :C_KDIM] + d2[:, C_KDIM:])
        q = q_ref[rows, :].astype(F32)
        v = v_ref[rows, :]
        attn = mask_ref[n_lvl] * _dot_t(q.astype(BF16), kk.astype(BF16))
        for lv in range(n_lvl):
            qs = (q * e[2 * lv * c:(2 * lv + 1) * c]).astype(BF16)
            ks = (kk * e[(2 * lv + 1) * c:(2 * lv + 2) * c]).astype(BF16)
            attn = attn + mask_ref[lv] * _dot_t(qs, ks)
        eb = e[2 * n_lvl * c:(2 * n_lvl + 1) * c]
        el = e[(2 * n_lvl + 1) * c:(2 * n_lvl + 2) * c]
        st = st_ref[...]
        o = _dot(attn.astype(BF16), v) + _dot_t((q * eb).astype(BF16), st.astype(BF16))
        kl = (kk * el).astype(BF16)
        st_ref[...] = st * eb[c - 1:c, :] + lax.dot_general(
            v, kl, (((0,), (0,)), ((), ())), preferred_element_type=F32)
        ms = jnp.mean(o * o, axis=-1, keepdims=True)
        o_ref[rows, :] = (o * lax.rsqrt(ms + RMS_EPS) * gn_ref[...]
                          * gs_ref[rows, :].astype(F32)).astype(o_ref.dtype)
        return carry

    lax.fori_loop(0, t_blk // c, chunk, 0)


def _hgrn2(qs, forget, v, gs, gnorm_g, *, t_blk=512):
    s, d = qs.shape
    n_heads = d // C_KDIM
    p_np, mask_np = _hgrn_tables()
    p_mat = jnp.asarray(p_np, BF16)
    masks = jnp.asarray(mask_np, F32)
    blk = lambda: pl.BlockSpec((t_blk, C_KDIM), lambda h, i: (i, h))
    return pl.pallas_call(
        functools.partial(_hgrn_body, t_blk=t_blk),
        grid=(n_heads, s // t_blk),
        in_specs=[blk(), blk(), blk(), blk(),
                  pl.BlockSpec(p_mat.shape, lambda h, i: (0, 0)),
                  pl.BlockSpec(masks.shape, lambda h, i: (0, 0, 0)),
                  pl.BlockSpec((1, C_VDIM), lambda h, i: (0, 0))],
        out_specs=blk(),
        out_shape=jax.ShapeDtypeStruct((s, d), BF16),
        scratch_shapes=[pltpu.VMEM((C_VDIM, C_KDIM), F32)],
        compiler_params=_cparams("arbitrary", "arbitrary"),
        name="hgrn2",
    )(qs, forget, v, gs, p_mat, masks, gnorm_g.reshape(1, -1))


def _router_body(x_ref, sc_ref, sh_ref, whi_ref, wlo_ref, info_ref, cnt_ref, carry_s, *, tm):
    @pl.when(pl.program_id(0) == 0)
    def _():
        carry_s[...] = jnp.zeros(carry_s.shape, F32)

    hmod = x_ref[...] * (1.0 + sc_ref[...]) + sh_ref[...]
    hh = hmod.astype(BF16)
    hl = (hmod - hh.astype(F32)).astype(BF16)
    whi = whi_ref[...]
    logits = _dot(hh, whi) + _dot(hl, whi) + _dot(hh, wlo_ref[...])
    lane = lax.broadcasted_iota(jnp.int32, (tm, LANES), 1)
    logits = jnp.where(lane < N_EXPERTS, logits, -jnp.inf)
    m1 = jnp.max(logits, axis=-1, keepdims=True)
    e1 = jnp.min(jnp.where(logits == m1, lane, LANES), axis=-1, keepdims=True)
    rest = jnp.where(lane == e1, -jnp.inf, logits)
    m2 = jnp.max(rest, axis=-1, keepdims=True)
    e2 = jnp.min(jnp.where(rest == m2, lane, LANES), axis=-1, keepdims=True)
    ex = jnp.exp(m2 - m1)
    g1 = 1.0 / (1.0 + ex)
    g2 = ex / (1.0 + ex)
    onehot = jnp.where(jnp.logical_or(lane == e1, lane == e2), 1.0, 0.0)
    row = lax.broadcasted_iota(jnp.int32, (tm, tm), 0)
    col = lax.broadcasted_iota(jnp.int32, (tm, tm), 1)
    before = jnp.where(row > col, 1.0, 0.0).astype(BF16)
    cum = _dot(before, onehot.astype(BF16)) + carry_s[...]
    r1 = jnp.sum(jnp.where(lane == e1, cum, 0.0), axis=-1, keepdims=True)
    r2 = jnp.sum(jnp.where(lane == e2, cum, 0.0), axis=-1, keepdims=True)
    carry_s[...] = carry_s[...] + jnp.sum(onehot, axis=0, keepdims=True)
    cnt_ref[...] = carry_s[...]
    info = jnp.where(lane == 0, e1.astype(F32), 0.0)
    info = jnp.where(lane == 1, e2.astype(F32), info)
    info = jnp.where(lane == 2, g1, info)
    info = jnp.where(lane == 3, g2, info)
    info = jnp.where(lane == 4, r1, info)
    info = jnp.where(lane == 5, r2, info)
    info_ref[...] = info


def _router(x, scale, shift, router_w, *, tm=256):
    s, d = x.shape
    wpad = jnp.zeros((d, LANES), F32).at[:, :N_EXPERTS].set(router_w)
    whi = wpad.astype(BF16)
    wlo = (wpad - whi.astype(F32)).astype(BF16)
    vec = pl.BlockSpec((1, d), lambda i: (0, 0))
    wspec = pl.BlockSpec((d, LANES), lambda i: (0, 0))
    return pl.pallas_call(
        functools.partial(_router_body, tm=tm),
        grid=(s // tm,),
        in_specs=[pl.BlockSpec((tm, d), lambda i: (i, 0)), vec, vec, wspec, wspec],
        out_specs=[pl.BlockSpec((tm, LANES), lambda i: (i, 0)),
                   pl.BlockSpec((1, LANES), lambda i: (0, 0))],
        out_shape=[jax.ShapeDtypeStruct((s, LANES), F32), jax.ShapeDtypeStruct((1, LANES), F32)],
        scratch_shapes=[pltpu.VMEM((1, LANES), F32)],
        compiler_params=_cparams("arbitrary"),
        name="router",
    )(x, scale.reshape(1, d), shift.reshape(1, d), whi, wlo)


def _dispatch_copies(s1_ref, s2_ref, h_ref, out_ref, sem, base, r):
    t = base + r
    src = h_ref.at[pl.ds(r, 1)]
    return (pltpu.make_async_copy(src, out_ref.at[pl.ds(s1_ref[t], 1)], sem.at[0]),
            pltpu.make_async_copy(src, out_ref.at[pl.ds(s2_ref[t], 1)], sem.at[1]))


def _dispatch_body(s1_ref, s2_ref, h_ref, init_ref, out_ref, sem, *, tb):
    del init_ref
    base = pl.program_id(0) * tb

    def issue(r, carry):
        for cp in _dispatch_copies(s1_ref, s2_ref, h_ref, out_ref, sem, base, r):
            cp.start()
        return carry

    def drain(r, carry):
        for cp in _dispatch_copies(s1_ref, s2_ref, h_ref, out_ref, sem, base, r):
            cp.wait()
        return carry

    lax.fori_loop(0, tb, issue, 0)
    lax.fori_loop(0, tb, drain, 0)


def _dispatch(h_words, slot1, slot2, n_rows, *, tb=256):
    s, dw = h_words.shape
    init = jnp.zeros((n_rows, dw), h_words.dtype)
    grid_spec = pltpu.PrefetchScalarGridSpec(
        num_scalar_prefetch=2,
        grid=(s // tb,),
        in_specs=[pl.BlockSpec((tb, dw), lambda i, a, b: (i, 0)),
                  pl.BlockSpec(memory_space=pl.ANY)],
        out_specs=pl.BlockSpec(memory_space=pl.ANY),
        scratch_shapes=[pltpu.SemaphoreType.DMA((2,))])
    return pl.pallas_call(
        functools.partial(_dispatch_body, tb=tb),
        grid_spec=grid_spec,
        out_shape=jax.ShapeDtypeStruct((n_rows, dw), h_words.dtype),
        input_output_aliases={3: 0},
        compiler_params=_cparams("arbitrary"),
        name="moe_dispatch",
    )(slot1, slot2, h_words, init)


def _expert_in_body(te_ref, nu_ref, x_ref, wa_ref, wb_ref, o_ref):
    i = pl.program_id(1)

    @pl.when(i < nu_ref[0])
    def _():
        x = x_ref[...]
        o_ref[...] = (_silu(_dot(x, wa_ref[...])) * _dot(x, wb_ref[...])).astype(o_ref.dtype)

    @pl.when(i >= nu_ref[0])
    def _():
        o_ref[...] = jnp.zeros(o_ref.shape, o_ref.dtype)


def _expert_out_body(te_ref, nu_ref, x_ref, w_ref, o_ref):
    i = pl.program_id(1)

    @pl.when(i < nu_ref[0])
    def _():
        o_ref[...] = _dot(x_ref[...], w_ref[...]).astype(o_ref.dtype)

    @pl.when(i >= nu_ref[0])
    def _():
        o_ref[...] = jnp.zeros(o_ref.shape, o_ref.dtype)


def _expert_in(xs, w_in, tile_expert, n_used, *, tm, tn=512):
    p, k = xs.shape
    half = w_in.shape[2] // 2
    hoff = half // tn
    row = lambda j, i, te, nu: (jnp.minimum(i, nu[0] - 1), 0)
    grid_spec = pltpu.PrefetchScalarGridSpec(
        num_scalar_prefetch=2,
        grid=(half // tn, p // tm),
        in_specs=[pl.BlockSpec((tm, k), row),
                  pl.BlockSpec((None, k, tn), lambda j, i, te, nu: (te[i], 0, j)),
                  pl.BlockSpec((None, k, tn), lambda j, i, te, nu: (te[i], 0, j + hoff))],
        out_specs=pl.BlockSpec((tm, tn), lambda j, i, te, nu: (i, j)))
    return pl.pallas_call(
        _expert_in_body,
        grid_spec=grid_spec,
        out_shape=jax.ShapeDtypeStruct((p, half), BF16),
        compiler_params=_cparams("arbitrary", "arbitrary"),
        name="expert_in",
    )(tile_expert, n_used, xs, w_in, w_in)


def _expert_out(gs, w_out, tile_expert, n_used, *, tm, tn=1024):
    p, k = gs.shape
    n = w_out.shape[2]
    row = lambda j, i, te, nu: (jnp.minimum(i, nu[0] - 1), 0)
    grid_spec = pltpu.PrefetchScalarGridSpec(
        num_scalar_prefetch=2,
        grid=(n // tn, p // tm),
        in_specs=[pl.BlockSpec((tm, k), row),
                  pl.BlockSpec((None, k, tn), lambda j, i, te, nu: (te[i], 0, j))],
        out_specs=pl.BlockSpec((tm, tn), lambda j, i, te, nu: (i, j)))
    return pl.pallas_call(
        _expert_out_body,
        grid_spec=grid_spec,
        out_shape=jax.ShapeDtypeStruct((p, n), F32),
        compiler_params=_cparams("arbitrary", "arbitrary"),
        name="expert_out",
    )(tile_expert, n_used, gs, w_out)


def _combine_copies(s1_ref, s2_ref, ys_ref, buf, sem, base, r):
    t = base + r
    return (pltpu.make_async_copy(ys_ref.at[pl.ds(s1_ref[t], 1)], buf.at[0, pl.ds(r, 1)], sem.at[0]),
            pltpu.make_async_copy(ys_ref.at[pl.ds(s2_ref[t], 1)], buf.at[1, pl.ds(r, 1)], sem.at[1]))


def _combine_body(s1_ref, s2_ref, x_ref, info_ref, ys_ref, gate_ref, g_ref, b_ref, o_ref, buf, sem,
                  *, tb, alpha):
    base = pl.program_id(0) * tb

    def issue(r, carry):
        for cp in _combine_copies(s1_ref, s2_ref, ys_ref, buf, sem, base, r):
            cp.start()
        return carry

    def drain(r, carry):
        for cp in _combine_copies(s1_ref, s2_ref, ys_ref, buf, sem, base, r):
            cp.wait()
        return carry

    lax.fori_loop(0, tb, issue, 0)
    lax.fori_loop(0, tb, drain, 0)
    info = info_ref[...]
    y = info[:, 2:3] * buf[0] + info[:, 3:4] * buf[1]
    r = alpha * x_ref[...] + (1.0 + gate_ref[...]) * y
    o_ref[...] = _layer_norm_rows(r, g_ref[...], b_ref[...])


def _combine_postnorm(x, info, ys, slot1, slot2, gate, g, b, *, alpha, tb=128):
    s, d = x.shape
    vec = pl.BlockSpec((1, d), lambda i, a, c: (0, 0))
    grid_spec = pltpu.PrefetchScalarGridSpec(
        num_scalar_prefetch=2,
        grid=(s // tb,),
        in_specs=[pl.BlockSpec((tb, d), lambda i, a, c: (i, 0)),
                  pl.BlockSpec((tb, LANES), lambda i, a, c: (i, 0)),
                  pl.BlockSpec(memory_space=pl.ANY),
                  vec, vec, vec],
        out_specs=pl.BlockSpec((tb, d), lambda i, a, c: (i, 0)),
        scratch_shapes=[pltpu.VMEM((2, tb, d), F32), pltpu.SemaphoreType.DMA((2,))])
    return pl.pallas_call(
        functools.partial(_combine_body, tb=tb, alpha=alpha),
        grid_spec=grid_spec,
        out_shape=jax.ShapeDtypeStruct((s, d), F32),
        compiler_params=_cparams("arbitrary"),
        name="moe_combine_postnorm",
    )(slot1, slot2, x, info, ys, gate.reshape(1, d), g.reshape(1, d), b.reshape(1, d))


def _moe_plan(info, counts, *, tm, n_tiles):
    e1 = info[:, 0].astype(jnp.int32)
    e2 = info[:, 1].astype(jnp.int32)
    r1 = info[:, 4].astype(jnp.int32)
    r2 = info[:, 5].astype(jnp.int32)
    cnt = counts[0, :N_EXPERTS].astype(jnp.int32)
    tiles = (cnt + tm - 1) // tm
    tile_end = jnp.cumsum(tiles)
    row_off = (tile_end - tiles) * tm
    slot1 = row_off[e1] + r1
    slot2 = row_off[e2] + r2
    tile_expert = jnp.minimum(
        jnp.sum(jnp.arange(n_tiles, dtype=jnp.int32)[:, None] >= tile_end[None, :], axis=1),
        N_EXPERTS - 1).astype(jnp.int32)
    n_used = tile_end[-1:].astype(jnp.int32)
    return slot1, slot2, tile_expert, n_used


def _pad_cols(w, n):
    return jnp.pad(w, ((0, 0), (0, n - w.shape[1])))


def kernel(x, c, ada_w, ada_b, ln_g, ln_b, even_w_in, lam_q1, lam_k1, lam_q2, lam_k2, subln_g, pool_w,
           pool_scale, even_w_out, ffn_w_in, ffn_w_out, odd_w_in, lb_raw, gnorm_g, odd_w_out, router_w,
           exp_w_in, exp_w_out):
    _, seq, d = x.shape
    depth = ada_w.shape[0]
    alpha = (2 * depth) ** 0.25
    a_width = d // 2
    a_heads = a_width // A_VDIM
    x2 = x.reshape(seq, d)

    lb_all = jnp.cumsum(jax.nn.softmax(lb_raw.astype(F32), axis=0), axis=0)
    lb_all = lb_all - lb_all[0]
    mod = _ada_mod(c, ada_w, ada_b)

    def mod_parts(l):
        return [mod[l, i * d:(i + 1) * d] for i in range(6)]

    sh1, sc1, g1, sh2, sc2, g2 = mod_parts(0)
    h = _modulate(x2, sc1, sh1)
    qkvu = _proj(h, even_w_in[0].astype(BF16), col_off=0, n_cols=3 * a_width + (d - a_width), out_dtype=BF16,
                 epilogue="qscale", q_cols=a_width, q_scale=A_QKDIM ** -0.5, name="even_in_proj")
    lam_init = 0.8 - 0.6 * math.exp(-0.3 * 0)
    o_a = _diff_attention(qkvu, lam_q1[0], lam_k1[0], lam_q2[0], lam_k2[0], subln_g[0],
                          n_heads=a_heads, lam_init=lam_init)
    o_b = _multiscale_pool(qkvu, pool_w[0].astype(BF16), pool_scale[0], col_off=3 * a_width)
    y = _proj(jnp.concatenate([o_a, o_b], axis=-1), even_w_out[0].astype(BF16), col_off=0, n_cols=d,
              out_dtype=F32, name="even_out_proj")
    x2, h = _postnorm(x2, y, g1, ln_g[0, 0], ln_b[0, 0], sc2, sh2, alpha=alpha)

    ff = ffn_w_out.shape[1]
    ffp = -(-ff // 1024) * 1024
    w_in = jnp.concatenate([_pad_cols(ffn_w_in[0][:, :ff].astype(BF16), ffp),
                            _pad_cols(ffn_w_in[0][:, ff:].astype(BF16), ffp)], axis=1)
    w_out = jnp.pad(ffn_w_out[0].astype(BF16), ((0, ffp - ff), (0, 0)))
    gact = _swiglu_in(h, w_in)
    y = _mm_acc(gact, w_out, tk=ffp // 4)
    sh1, sc1, g1n, sh2n, sc2n, g2n = mod_parts(1)
    x2, h = _postnorm(x2, y, g2, ln_g[0, 1], ln_b[0, 1], sc1, sh1, alpha=alpha)

    w_odd = odd_w_in[0].astype(BF16)
    qs = _proj(h, w_odd, col_off=0, n_cols=d, out_dtype=BF16, epilogue="silu", name="odd_in_q")
    fg = _proj(h, w_odd, col_off=d, n_cols=d, out_dtype=F32, epilogue="forget", lb=lb_all[1], name="odd_in_f")
    vi = _proj(h, w_odd, col_off=2 * d, n_cols=d, out_dtype=BF16, name="odd_in_i")
    gs = _proj(h, w_odd, col_off=3 * d, n_cols=d, out_dtype=BF16, epilogue="silu", name="odd_in_g")
    o_c = _hgrn2(qs, fg, vi, gs, gnorm_g[0])
    y = _proj(o_c, odd_w_out[0].astype(BF16), col_off=0, n_cols=d, out_dtype=F32, name="odd_out_proj")
    x2, h = _postnorm(x2, y, g1n, ln_g[1, 0], ln_b[1, 0], sc2n, sh2n, alpha=alpha)

    tm_e = 512
    n_tiles = (2 * seq) // tm_e + N_EXPERTS
    info, counts = _router(x2, sc2n, sh2n, router_w[0])
    slot1, slot2, tile_expert, n_used = _moe_plan(info, counts, tm=tm_e, n_tiles=n_tiles)
    h_words = lax.bitcast_convert_type(h.reshape(seq, d // 2, 2), jnp.uint32)
    xs_words = _dispatch(h_words, slot1, slot2, n_tiles * tm_e)
    xs = lax.bitcast_convert_type(xs_words, BF16).reshape(n_tiles * tm_e, d)
    gexp = _expert_in(xs, exp_w_in[0].astype(BF16), tile_expert, n_used, tm=tm_e)
    ys = _expert_out(gexp, exp_w_out[0].astype(BF16), tile_expert, n_used, tm=tm_e)
    out = _combine_postnorm(x2, info, ys, slot1, slot2, g2n, ln_g[1, 1], ln_b[1, 1], alpha=alpha)
    return out.reshape(x.shape)
```

```python
import functools
import math

import numpy as np
import jax
import jax.numpy as jnp
from jax import lax
from jax.experimental import pallas as pl
from jax.experimental.pallas import tpu as pltpu

F32 = jnp.float32
BF16 = jnp.bfloat16
I32 = jnp.int32

A_VDIM = 128
A_QKDIM = 64
POOL_WINDOWS = (2, 4, 8, 16)
POOL_HALO = 16
C_KDIM = 128
C_VDIM = 128
HGRN_CHUNK = 64
N_EXPERTS = 8
LN_EPS = 1e-5
RMS_EPS = 1e-6
NEG_INF = -1e30

LANES = 128
VMEM_LIMIT_BYTES = 56 * 1024 * 1024


def _cparams(*sem):
    return pltpu.CompilerParams(dimension_semantics=sem, vmem_limit_bytes=VMEM_LIMIT_BYTES)


def _silu(x):
    return x * (1.0 / (1.0 + jnp.exp(-x)))


def _dot(a, b):
    return jnp.dot(a, b, preferred_element_type=F32)


def _dot_t(a, b):
    return lax.dot_general(a, b, (((1,), (1,)), ((), ())), preferred_element_type=F32)


def _pack_bf16_pairs(h):
    half = h.shape[1] // 2
    bits = pltpu.bitcast(h.astype(BF16).astype(F32), I32)
    return jnp.bitwise_or(bits[:, :half], lax.shift_right_logical(bits[:, half:], 16))


def _unpack_bf16_pairs(w):
    hi = pltpu.bitcast(jnp.bitwise_and(w, -65536), F32).astype(BF16)
    lo = pltpu.bitcast(lax.shift_left(w, 16), F32).astype(BF16)
    return jnp.concatenate([hi, lo], axis=1)


def _ada_body(c_ref, w_ref, b_ref, o_ref):
    ca = _silu(c_ref[...]).astype(BF16)
    o_ref[...] = _dot(ca, w_ref[...].astype(BF16)) + b_ref[...]


def _ada_mod(c, ada_w, ada_b, *, tn=1024):
    depth, d, n = ada_w.shape
    c8 = jnp.broadcast_to(c.astype(F32), (8, d))
    out = pl.pallas_call(
        _ada_body,
        grid=(depth, n // tn),
        in_specs=[pl.BlockSpec((8, d), lambda l, j: (0, 0)),
                  pl.BlockSpec((None, d, tn), lambda l, j: (l, 0, j)),
                  pl.BlockSpec((None, 1, tn), lambda l, j: (l, 0, j))],
        out_specs=pl.BlockSpec((None, 8, tn), lambda l, j: (l, 0, j)),
        out_shape=jax.ShapeDtypeStruct((depth, 8, n), F32),
        compiler_params=_cparams("arbitrary", "arbitrary"),
        name="ada_mod",
    )(c8, ada_w, ada_b.reshape(depth, 1, n))
    return out[:, 0, :]


def _modulate_body(x_ref, sc_ref, sh_ref, o_ref):
    o_ref[...] = (x_ref[...] * (1.0 + sc_ref[...]) + sh_ref[...]).astype(o_ref.dtype)


def _modulate(x, scale, shift, *, tm=512):
    s, d = x.shape
    vec = pl.BlockSpec((1, d), lambda i: (0, 0))
    return pl.pallas_call(
        _modulate_body,
        grid=(s // tm,),
        in_specs=[pl.BlockSpec((tm, d), lambda i: (i, 0)), vec, vec],
        out_specs=pl.BlockSpec((tm, d), lambda i: (i, 0)),
        out_shape=jax.ShapeDtypeStruct((s, d), BF16),
        compiler_params=_cparams("arbitrary"),
        name="modulate",
    )(x, scale.reshape(1, d), shift.reshape(1, d))


def _layer_norm_rows(r, g, b):
    mu = jnp.mean(r, axis=-1, keepdims=True)
    rc = r - mu
    var = jnp.mean(rc * rc, axis=-1, keepdims=True)
    return rc * lax.rsqrt(var + LN_EPS) * g + b


def _postnorm_body(x_ref, y_ref, gate_ref, g_ref, b_ref, sc_ref, sh_ref, xo_ref, ho_ref, *, alpha, pack):
    r = alpha * x_ref[...] + (1.0 + gate_ref[...]) * y_ref[...].astype(F32)
    xn = _layer_norm_rows(r, g_ref[...], b_ref[...])
    xo_ref[...] = xn
    hn = xn * (1.0 + sc_ref[...]) + sh_ref[...]
    ho_ref[...] = _pack_bf16_pairs(hn) if pack else hn.astype(ho_ref.dtype)


def _postnorm(x, y, gate, g, b, nscale, nshift, *, alpha, pack=False, tm=256):
    s, d = x.shape
    row = pl.BlockSpec((tm, d), lambda i: (i, 0))
    vec = pl.BlockSpec((1, d), lambda i: (0, 0))
    hd, hdt = (d // 2, I32) if pack else (d, BF16)
    return pl.pallas_call(
        functools.partial(_postnorm_body, alpha=alpha, pack=pack),
        grid=(s // tm,),
        in_specs=[row, row, vec, vec, vec, vec, vec],
        out_specs=[row, pl.BlockSpec((tm, hd), lambda i: (i, 0))],
        out_shape=[jax.ShapeDtypeStruct((s, d), F32), jax.ShapeDtypeStruct((s, hd), hdt)],
        compiler_params=_cparams("arbitrary"),
        name="postnorm",
    )(x, y, gate.reshape(1, d), g.reshape(1, d), b.reshape(1, d), nscale.reshape(1, d), nshift.reshape(1, d))


def _proj_body(*refs, n_a, epilogue, tn, q_cols, q_scale):
    a_refs, w_ref, extra = refs[:n_a], refs[n_a], refs[n_a + 1:-2]
    o_ref, wb_s = refs[-2], refs[-1]

    @pl.when(pl.program_id(1) == 0)
    def _():
        wb_s[...] = w_ref[...].astype(BF16)

    acc, off = None, 0
    for a_ref in a_refs:
        kp = a_ref.shape[1]
        part = _dot(a_ref[...], wb_s[off:off + kp, :])
        acc = part if acc is None else acc + part
        off += kp
    if epilogue == "silu":
        acc = _silu(acc)
    elif epilogue == "forget":
        lb = extra[0][...]
        acc = lb + (1.0 - lb) * (1.0 / (1.0 + jnp.exp(-acc)))
    elif epilogue == "qscale":
        acc = acc * jnp.where(pl.program_id(0) * tn < q_cols, q_scale, 1.0)
    o_ref[...] = acc.astype(o_ref.dtype)


def _proj(a_parts, w, *, col_off, n_cols, out_dtype, epilogue="id", lb=None, q_cols=0, q_scale=1.0,
          tm=1024, tn=512, name="proj"):
    m = a_parts[0].shape[0]
    k = w.shape[0]
    joff = col_off // tn
    in_specs = [pl.BlockSpec((tm, a.shape[1]), lambda j, i: (i, 0)) for a in a_parts]
    in_specs.append(pl.BlockSpec((k, tn), lambda j, i: (0, j + joff)))
    args = list(a_parts) + [w]
    if epilogue == "forget":
        in_specs.append(pl.BlockSpec((1, tn), lambda j, i: (0, j)))
        args.append(lb.reshape(1, n_cols))
    return pl.pallas_call(
        functools.partial(_proj_body, n_a=len(a_parts), epilogue=epilogue, tn=tn, q_cols=q_cols, q_scale=q_scale),
        grid=(n_cols // tn, m // tm),
        in_specs=in_specs,
        out_specs=pl.BlockSpec((tm, tn), lambda j, i: (i, j)),
        out_shape=jax.ShapeDtypeStruct((m, n_cols), out_dtype),
        scratch_shapes=[pltpu.VMEM((k, tn), BF16)],
        compiler_params=_cparams("arbitrary", "arbitrary"),
        name=name,
    )(*args)


def _swiglu_in_body(a_ref, wa_ref, wb_ref, o_ref):
    a = a_ref[...]
    ga = _dot(a, wa_ref[...])
    gb = _dot(a, wb_ref[...])
    o_ref[...] = (_silu(ga) * gb).astype(o_ref.dtype)


def _swiglu_in(a, w, *, tm=1024, tn=512):
    m, k = a.shape
    half = w.shape[1] // 2
    hoff = half // tn
    return pl.pallas_call(
        _swiglu_in_body,
        grid=(m // tm, half // tn),
        in_specs=[pl.BlockSpec((tm, k), lambda i, j: (i, 0)),
                  pl.BlockSpec((k, tn), lambda i, j: (0, j)),
                  pl.BlockSpec((k, tn), lambda i, j: (0, j + hoff))],
        out_specs=pl.BlockSpec((tm, tn), lambda i, j: (i, j)),
        out_shape=jax.ShapeDtypeStruct((m, half), BF16),
        compiler_params=_cparams("arbitrary", "arbitrary"),
        name="swiglu_in",
    )(a, w, w)


def _mm_acc_body(a_ref, w_ref, o_ref, acc_ref, *, nk):
    kk = pl.program_id(2)
    part = _dot(a_ref[...], w_ref[...])

    @pl.when(kk == 0)
    def _():
        acc_ref[...] = part

    @pl.when(kk > 0)
    def _():
        acc_ref[...] += part

    @pl.when(kk == nk - 1)
    def _():
        o_ref[...] = acc_ref[...].astype(o_ref.dtype)


def _mm_acc(a, w, *, tm=1024, tn=1024, tk, out_dtype=F32):
    m, k = a.shape
    n = w.shape[1]
    nk = k // tk
    return pl.pallas_call(
        functools.partial(_mm_acc_body, nk=nk),
        grid=(m // tm, n // tn, nk),
        in_specs=[pl.BlockSpec((tm, tk), lambda i, j, kk: (i, kk)),
                  pl.BlockSpec((tk, tn), lambda i, j, kk: (kk, j))],
        out_specs=pl.BlockSpec((tm, tn), lambda i, j, kk: (i, j)),
        out_shape=jax.ShapeDtypeStruct((m, n), out_dtype),
        scratch_shapes=[pltpu.VMEM((tm, tn), F32)],
        compiler_params=_cparams("arbitrary", "arbitrary", "arbitrary"),
        name="mm_acc",
    )(a, w)


ATTN_ONES_ROWS = 16
ATTN_SLAB = 256


def _attn_body(qi_tab, kv_tab, slopes, q_ref, k_ref, vt_ref, pos_ref, lq1_ref, lk1_ref, lq2_ref, lk2_ref, g_ref,
               o_ref, q1_s, q2_s, m1_s, a1_s, m2_s, a2_s, jmi_s, *, tq, tk, lam_init):
    h = pl.program_id(0)
    t = pl.program_id(1)
    qi = qi_tab[t]
    kv = kv_tab[t]
    slope = slopes[h]
    n_sub = tq // tk
    dv = A_VDIM

    @pl.when(t == 0)
    def _():
        jmi_s[...] = lax.broadcasted_iota(I32, (tk, tq), 0) - lax.broadcasted_iota(I32, (tk, tq), 1)

    @pl.when(kv == 0)
    def _():
        q = q_ref[...]
        lane = lax.broadcasted_iota(I32, q.shape, 1)
        zero = jnp.zeros_like(q)
        sv = jnp.full(q.shape, slope, F32)
        s_hi = sv.astype(BF16).astype(F32)
        s_lo = sv - s_hi
        coef = jnp.where(lane == 0, s_hi * 16.0, jnp.where(lane == 1, s_lo * 16.0,
                         jnp.where(lane == 2, s_hi, jnp.where(lane == 3, s_lo, 0.0)))).astype(BF16)
        q1_s[...] = jnp.concatenate([jnp.where(lane < A_QKDIM, q, zero), coef], axis=1)
        q2_s[...] = jnp.concatenate([jnp.where(lane >= A_QKDIM, q, zero), coef], axis=1)
        for m_s, a_s in ((m1_s, a1_s), (m2_s, a2_s)):
            m_s[...] = jnp.full(m_s.shape, -jnp.inf, F32)
            a_s[...] = jnp.zeros(a_s.shape, F32)

    shift = kv * tk - qi * tq

    def step(key_off):
        k_aug = jnp.concatenate([k_ref[...], pos_ref[...]], axis=1)
        vt = vt_ref[...]
        c = slope * shift.astype(F32)
        chains = []
        for c0 in range(0, tq, ATTN_SLAB):
            if key_off is not None and c0 + ATTN_SLAB - 1 < key_off:
                continue
            masked = key_off is not None and c0 < key_off + tk - 1
            for q_s, m_s, a_s in ((q1_s, m1_s, a1_s), (q2_s, m2_s, a2_s)):
                chains.append((q_s, m_s, a_s, slice(c0, c0 + ATTN_SLAB), masked))
        scores = [_dot_t(k_aug, q_s[cols, :]) for q_s, _, _, cols, _ in chains]
        for s, (_, m_s, a_s, cols, masked) in zip(scores, chains):
            if masked:
                s = jnp.where(jmi_s[:, cols] <= -key_off, s, NEG_INF)
            m_prev = m_s[:, cols]
            m_new = jnp.maximum(m_prev, jnp.max(s, axis=0, keepdims=True) + c)
            p = jnp.exp2(s - (m_new - c))
            a_s[:, cols] = jnp.exp2(m_prev - m_new) * a_s[:, cols] + _dot(vt, p.astype(BF16))
            m_s[:, cols] = m_new

    @pl.when(kv < qi * n_sub)
    def _():
        step(None)

    for r in range(n_sub):
        @pl.when(kv == qi * n_sub + r)
        def _():
            step(r * tk)

    @pl.when(kv == (qi + 1) * n_sub - 1)
    def _():
        lam = (jnp.exp(jnp.sum(lq1_ref[...] * lk1_ref[...], axis=-1, keepdims=True))
               - jnp.exp(jnp.sum(lq2_ref[...] * lk2_ref[...], axis=-1, keepdims=True)) + lam_init)
        a1 = a1_s[...]
        a2 = a2_s[...]
        o = a1[:dv] / a1[dv:dv + 1] - lam * (a2[:dv] / a2[dv:dv + 1])
        ms = jnp.mean(o * o, axis=0, keepdims=True)
        on = o * lax.rsqrt(ms + RMS_EPS) * g_ref[...] * (1.0 - lam_init)
        o_ref[...] = on.T.astype(o_ref.dtype)


def _diff_attention(qkvu, v_t, lam_q1, lam_k1, lam_q2, lam_k2, subln_g, *, n_heads, lam_init, tq=1024, tk=512):
    s = qkvu.shape[0]
    n_sub = tq // tk
    pairs = [(i, j) for i in range(s // tq) for j in range((i + 1) * n_sub)]
    qi_tab = jnp.asarray([p[0] for p in pairs], I32)
    kv_tab = jnp.asarray([p[1] for p in pairs], I32)
    slopes = jnp.asarray([math.log2(math.e) * 2.0 ** (-8.0 * (i + 1) / n_heads) for i in range(n_heads)], F32)
    hd = A_VDIM
    hv = hd + ATTN_ONES_ROWS
    j = np.arange(tk)
    pos = np.zeros((tk, hd), np.float32)
    pos[:, 0] = pos[:, 1] = j // 16
    pos[:, 2] = pos[:, 3] = j % 16
    small = pl.BlockSpec((1, A_QKDIM), lambda h, t, qt, kt, sl: (0, 0))
    grid_spec = pltpu.PrefetchScalarGridSpec(
        num_scalar_prefetch=3,
        grid=(n_heads, len(pairs)),
        in_specs=[pl.BlockSpec((tq, hd), lambda h, t, qt, kt, sl: (qt[t], h)),
                  pl.BlockSpec((tk, hd), lambda h, t, qt, kt, sl: (kt[t], n_heads + h)),
                  pl.BlockSpec((hv, tk), lambda h, t, qt, kt, sl: (h, kt[t])),
                  pl.BlockSpec((tk, hd), lambda h, t, qt, kt, sl: (0, 0)),
                  small, small, small, small,
                  pl.BlockSpec((hd, 1), lambda h, t, qt, kt, sl: (0, 0))],
        out_specs=pl.BlockSpec((tq, hd), lambda h, t, qt, kt, sl: (qt[t], h)),
        scratch_shapes=[pltpu.VMEM((tq, 2 * hd), BF16), pltpu.VMEM((tq, 2 * hd), BF16),
                        pltpu.VMEM((1, tq), F32), pltpu.VMEM((hv, tq), F32),
                        pltpu.VMEM((1, tq), F32), pltpu.VMEM((hv, tq), F32),
                        pltpu.VMEM((tk, tq), I32)])
    return pl.pallas_call(
        functools.partial(_attn_body, tq=tq, tk=tk, lam_init=lam_init),
        grid_spec=grid_spec,
        out_shape=jax.ShapeDtypeStruct((s, n_heads * hd), BF16),
        compiler_params=_cparams("arbitrary", "arbitrary"),
        name="diff_attention",
    )(qi_tab, kv_tab, slopes, qkvu, qkvu, v_t, jnp.asarray(pos, BF16),
      lam_q1.reshape(1, -1), lam_k1.reshape(1, -1), lam_q2.reshape(1, -1), lam_k2.reshape(1, -1),
      subln_g.reshape(-1, 1))


def _attn_value_rows(v, n_heads):
    s = v.shape[0]
    vt = v.T.reshape(n_heads, A_VDIM, s)
    ones = jnp.ones((n_heads, ATTN_ONES_ROWS, s), v.dtype)
    return jnp.concatenate([vt, ones], axis=1).reshape(n_heads * (A_VDIM + ATTN_ONES_ROWS), s)


def _pool_body(ucur_ref, uprev_ref, w_ref, sc_ref, o_ref, *, t_blk):
    g = pl.program_id(0)
    i = pl.program_id(1)
    win = jnp.left_shift(2, g)
    row = lax.broadcasted_iota(I32, (t_blk, t_blk), 0)
    col = lax.broadcasted_iota(I32, (t_blk, t_blk), 1)
    d = row - col
    band = jnp.where(jnp.logical_and(d >= 0, d < win), 1.0, 0.0).astype(BF16)
    rowp = lax.broadcasted_iota(I32, (t_blk, POOL_HALO), 0)
    colp = lax.broadcasted_iota(I32, (t_blk, POOL_HALO), 1)
    dp = rowp + POOL_HALO - colp
    bandp = jnp.where(jnp.logical_and(dp < win, i > 0), 1.0, 0.0).astype(BF16)
    u = ucur_ref[...]
    usum = _dot(band, u) + _dot(bandp, uprev_ref[...])
    tpos = i * t_blk + lax.broadcasted_iota(I32, (t_blk, 1), 0)
    cnt = jnp.minimum(tpos + 1, win).astype(F32)
    dev = usum / cnt - u.astype(F32)
    y = _dot(dev.astype(BF16), w_ref[...]) * sc_ref[...]
    o_ref[...] = y.astype(o_ref.dtype)


def _multiscale_pool(qkvu, pool_w, pool_scale, *, col_off, t_blk=256):
    s = qkvu.shape[0]
    ng, gd, _ = pool_w.shape
    goff = col_off // gd
    hb = t_blk // POOL_HALO
    return pl.pallas_call(
        functools.partial(_pool_body, t_blk=t_blk),
        grid=(ng, s // t_blk),
        in_specs=[pl.BlockSpec((t_blk, gd), lambda g, i: (i, goff + g)),
                  pl.BlockSpec((POOL_HALO, gd), lambda g, i: (jnp.maximum(i * hb - 1, 0), goff + g)),
                  pl.BlockSpec((None, gd, gd), lambda g, i: (g, 0, 0)),
                  pl.BlockSpec((1, gd), lambda g, i: (0, g))],
        out_specs=pl.BlockSpec((t_blk, gd), lambda g, i: (i, g)),
        out_shape=jax.ShapeDtypeStruct((s, ng * gd), BF16),
        compiler_params=_cparams("arbitrary", "arbitrary"),
        name="multiscale_pool",
    )(qkvu, qkvu, pool_w, pool_scale.reshape(1, -1))


def _hgrn_tables():
    c = HGRN_CHUNK
    idx = np.arange(c)
    mats, masks = [], []
    h = c // 2
    while h >= 1:
        upper = (idx % (2 * h)) >= h
        e = idx - (idx % (2 * h)) + h - 1
        u = idx[None, :]
        pq = (upper[:, None] & (u > e[:, None]) & (u <= idx[:, None]))
        pk = ((~upper)[:, None] & (u > idx[:, None]) & (u <= e[:, None]))
        mats += [pq, pk]
        masks.append(upper[:, None] & (~upper)[None, :] & ((idx[:, None] // (2 * h)) == (idx[None, :] // (2 * h))))
        h //= 2
    masks.append(idx[:, None] == idx[None, :])
    mats.append(idx[None, :] <= idx[:, None])
    mats.append(idx[None, :] > idx[:, None])
    return (np.concatenate(mats, axis=0).astype(np.float32),
            np.stack(masks, axis=0).astype(np.float32))


def _hgrn_body(q_ref, f_ref, v_ref, gs_ref, p_ref, mask_ref, gn_ref, o_ref, st_ref, *, t_blk, n_grp):
    c = HGRN_CHUNK
    n_lvl = mask_ref.shape[0] - 1
    heads = range(n_grp)

    @pl.when(pl.program_id(1) == 0)
    def _():
        st_ref[...] = jnp.zeros(st_ref.shape, F32)

    def col(x, hg):
        return x[:, hg * C_KDIM:(hg + 1) * C_KDIM]

    def stack(parts):
        return jnp.concatenate(parts, axis=0)

    def block_diag(parts):
        zero = jnp.zeros_like(parts[0])
        return stack([jnp.concatenate([parts[hg] if j == hg else zero for j in heads], axis=1) for hg in heads])

    def chunk(ci, carry):
        rows = pl.ds(pl.multiple_of(ci * c, c), c)
        f = f_ref[rows, :]
        lf = jnp.log(f)
        kk = 1.0 - f
        hi = lf.astype(BF16)
        mid = (lf - hi.astype(F32)).astype(BF16)
        lf2 = jnp.concatenate([part for hg in heads for part in (col(hi, hg), col(mid, hg))], axis=1)
        d2 = _dot(p_ref[...], lf2)
        e = [jnp.exp(col(d2, 2 * hg) + col(d2, 2 * hg + 1)) for hg in heads]
        q = q_ref[rows, :].astype(F32)
        qh = [col(q, hg) for hg in heads]
        kh = [col(kk, hg) for hg in heads]
        attn = mask_ref[n_lvl] * _dot_t(stack([x.astype(BF16) for x in qh]), stack([x.astype(BF16) for x in kh]))
        for lv in range(n_lvl):
            qs = stack([(qh[hg] * e[hg][2 * lv * c:(2 * lv + 1) * c]).astype(BF16) for hg in heads])
            ks = stack([(kh[hg] * e[hg][(2 * lv + 1) * c:(2 * lv + 2) * c]).astype(BF16) for hg in heads])
            attn = attn + mask_ref[lv] * _dot_t(qs, ks)
        eb = [e[hg][2 * n_lvl * c:(2 * n_lvl + 1) * c] for hg in heads]
        el = [e[hg][(2 * n_lvl + 1) * c:(2 * n_lvl + 2) * c] for hg in heads]
        v = v_ref[rows, :]
        v_st = stack([col(v, hg) for hg in heads])
        st = st_ref[...]
        q_bd = block_diag([(qh[hg] * eb[hg]).astype(BF16) for hg in heads])
        o = _dot(attn.astype(BF16), v_st) + _dot_t(q_bd, st.astype(BF16))
        k_bd = block_diag([(kh[hg] * el[hg]).astype(BF16) for hg in heads])
        decay = jnp.concatenate([eb[hg][c - 1:c, :] for hg in heads], axis=1)
        st_ref[...] = st * decay + lax.dot_general(v_st, k_bd, (((0,), (0,)), ((), ())),
                                                   preferred_element_type=F32)
        ms = jnp.mean(o * o, axis=-1, keepdims=True)
        on = o * lax.rsqrt(ms + RMS_EPS) * gn_ref[...]
        gs = gs_ref[rows, :].astype(F32)
        for hg in heads:
            o_ref[rows, hg * C_VDIM:(hg + 1) * C_VDIM] = (on[hg * c:(hg + 1) * c] * col(gs, hg)).astype(o_ref.dtype)
        return carry

    lax.fori_loop(0, t_blk // c, chunk, 0)


def _hgrn2(qs, forget, v, gs, gnorm_g, *, t_blk=512, n_grp=4):
    s, d = qs.shape
    n_heads = d // C_KDIM
    p_np, mask_np = _hgrn_tables()
    p_mat = jnp.asarray(p_np, BF16)
    masks = jnp.asarray(np.stack([np.kron(np.eye(n_grp, dtype=np.float32), m) for m in mask_np]), F32)
    blk = lambda: pl.BlockSpec((t_blk, n_grp * C_KDIM), lambda h, i: (i, h))
    return pl.pallas_call(
        functools.partial(_hgrn_body, t_blk=t_blk, n_grp=n_grp),
        grid=(n_heads // n_grp, s // t_blk),
        in_specs=[blk(), blk(), blk(), blk(),
                  pl.BlockSpec(p_mat.shape, lambda h, i: (0, 0)),
                  pl.BlockSpec(masks.shape, lambda h, i: (0, 0, 0)),
                  pl.BlockSpec((1, C_VDIM), lambda h, i: (0, 0))],
        out_specs=blk(),
        out_shape=jax.ShapeDtypeStruct((s, d), BF16),
        scratch_shapes=[pltpu.VMEM((C_VDIM, n_grp * C_KDIM), F32)],
        compiler_params=_cparams("arbitrary", "arbitrary"),
        name="hgrn2",
    )(qs, forget, v, gs, p_mat, masks, gnorm_g.reshape(1, -1))


def _router_body(x_ref, sc_ref, sh_ref, whi_ref, wlo_ref, info_ref, cnt_ref, carry_s, *, tm):
    @pl.when(pl.program_id(0) == 0)
    def _():
        carry_s[...] = jnp.zeros(carry_s.shape, F32)

    hmod = x_ref[...] * (1.0 + sc_ref[...]) + sh_ref[...]
    hh = hmod.astype(BF16)
    hl = (hmod - hh.astype(F32)).astype(BF16)
    whi = whi_ref[...]
    logits = _dot(hh, whi) + _dot(hl, whi) + _dot(hh, wlo_ref[...])
    lane = lax.broadcasted_iota(I32, (tm, LANES), 1)
    logits = jnp.where(lane < N_EXPERTS, logits, -jnp.inf)
    m1 = jnp.max(logits, axis=-1, keepdims=True)
    e1 = jnp.min(jnp.where(logits == m1, lane, LANES), axis=-1, keepdims=True)
    rest = jnp.where(lane == e1, -jnp.inf, logits)
    m2 = jnp.max(rest, axis=-1, keepdims=True)
    e2 = jnp.min(jnp.where(rest == m2, lane, LANES), axis=-1, keepdims=True)
    ex = jnp.exp(m2 - m1)
    g1 = 1.0 / (1.0 + ex)
    g2 = ex / (1.0 + ex)
    onehot = jnp.where(jnp.logical_or(lane == e1, lane == e2), 1.0, 0.0)
    row = lax.broadcasted_iota(I32, (tm, tm), 0)
    col = lax.broadcasted_iota(I32, (tm, tm), 1)
    before = jnp.where(row > col, 1.0, 0.0).astype(BF16)
    cum = _dot(before, onehot.astype(BF16)) + carry_s[...]
    r1 = jnp.sum(jnp.where(lane == e1, cum, 0.0), axis=-1, keepdims=True)
    r2 = jnp.sum(jnp.where(lane == e2, cum, 0.0), axis=-1, keepdims=True)
    carry_s[...] = carry_s[...] + jnp.sum(onehot, axis=0, keepdims=True)
    cnt_ref[...] = carry_s[...]
    info = jnp.where(lane == 0, e1.astype(F32), 0.0)
    info = jnp.where(lane == 1, e2.astype(F32), info)
    info = jnp.where(lane == 2, g1, info)
    info = jnp.where(lane == 3, g2, info)
    info = jnp.where(lane == 4, r1, info)
    info = jnp.where(lane == 5, r2, info)
    info_ref[...] = info


def _router(x, scale, shift, router_w, *, tm=256):
    s, d = x.shape
    wpad = jnp.zeros((d, LANES), F32).at[:, :N_EXPERTS].set(router_w)
    whi = wpad.astype(BF16)
    wlo = (wpad - whi.astype(F32)).astype(BF16)
    vec = pl.BlockSpec((1, d), lambda i: (0, 0))
    wspec = pl.BlockSpec((d, LANES), lambda i: (0, 0))
    return pl.pallas_call(
        functools.partial(_router_body, tm=tm),
        grid=(s // tm,),
        in_specs=[pl.BlockSpec((tm, d), lambda i: (i, 0)), vec, vec, wspec, wspec],
        out_specs=[pl.BlockSpec((tm, LANES), lambda i: (i, 0)),
                   pl.BlockSpec((1, LANES), lambda i: (0, 0))],
        out_shape=[jax.ShapeDtypeStruct((s, LANES), F32), jax.ShapeDtypeStruct((1, LANES), F32)],
        scratch_shapes=[pltpu.VMEM((1, LANES), F32)],
        compiler_params=_cparams("arbitrary"),
        name="router",
    )(x, scale.reshape(1, d), shift.reshape(1, d), whi, wlo)


def _dispatch_copies(s1_ref, s2_ref, h_ref, out_ref, sem, base, r):
    t = base + r
    src = h_ref.at[pl.ds(r, 1)]
    return (pltpu.make_async_copy(src, out_ref.at[pl.ds(s1_ref[t], 1)], sem.at[0]),
            pltpu.make_async_copy(src, out_ref.at[pl.ds(s2_ref[t], 1)], sem.at[1]))


def _dispatch_body(s1_ref, s2_ref, h_ref, init_ref, out_ref, sem, *, tb):
    del init_ref
    base = pl.program_id(0) * tb

    def issue(r, carry):
        for cp in _dispatch_copies(s1_ref, s2_ref, h_ref, out_ref, sem, base, r):
            cp.start()
        return carry

    def drain(r, carry):
        for cp in _dispatch_copies(s1_ref, s2_ref, h_ref, out_ref, sem, base, r):
            cp.wait()
        return carry

    lax.fori_loop(0, tb, issue, 0)
    lax.fori_loop(0, tb, drain, 0)


def _dispatch(h_words, slot1, slot2, n_rows, *, tb=256):
    s, dw = h_words.shape
    init = jnp.zeros((n_rows, dw), h_words.dtype)
    grid_spec = pltpu.PrefetchScalarGridSpec(
        num_scalar_prefetch=2,
        grid=(s // tb,),
        in_specs=[pl.BlockSpec((tb, dw), lambda i, a, b: (i, 0)),
                  pl.BlockSpec(memory_space=pl.ANY)],
        out_specs=pl.BlockSpec(memory_space=pl.ANY),
        scratch_shapes=[pltpu.SemaphoreType.DMA((2,))])
    return pl.pallas_call(
        functools.partial(_dispatch_body, tb=tb),
        grid_spec=grid_spec,
        out_shape=jax.ShapeDtypeStruct((n_rows, dw), h_words.dtype),
        input_output_aliases={3: 0},
        compiler_params=_cparams("arbitrary"),
        name="moe_dispatch",
    )(slot1, slot2, h_words, init)


def _expert_tile_changed(te_ref, i):
    return jnp.logical_or(i == 0, te_ref[i] != te_ref[jnp.maximum(i - 1, 0)])


def _expert_in_body(te_ref, nu_ref, x_ref, wa_ref, wb_ref, o_ref, wa_s, wb_s):
    i = pl.program_id(1)

    @pl.when(_expert_tile_changed(te_ref, i))
    def _():
        wa_s[...] = wa_ref[...].astype(BF16)
        wb_s[...] = wb_ref[...].astype(BF16)

    @pl.when(i < nu_ref[0])
    def _():
        x = _unpack_bf16_pairs(x_ref[...])
        o_ref[...] = (_silu(_dot(x, wa_s[...])) * _dot(x, wb_s[...])).astype(o_ref.dtype)

    @pl.when(i >= nu_ref[0])
    def _():
        o_ref[...] = jnp.zeros(o_ref.shape, o_ref.dtype)


def _expert_out_body(te_ref, nu_ref, x_ref, w_ref, o_ref, w_s):
    i = pl.program_id(1)

    @pl.when(_expert_tile_changed(te_ref, i))
    def _():
        w_s[...] = w_ref[...].astype(BF16)

    @pl.when(i < nu_ref[0])
    def _():
        o_ref[...] = _dot(x_ref[...], w_s[...]).astype(o_ref.dtype)

    @pl.when(i >= nu_ref[0])
    def _():
        o_ref[...] = jnp.zeros(o_ref.shape, o_ref.dtype)


def _expert_in(xs_words, w_in, tile_expert, n_used, *, tm, tn=512):
    p, kw = xs_words.shape
    k = 2 * kw
    half = w_in.shape[2] // 2
    hoff = half // tn
    row = lambda j, i, te, nu: (jnp.minimum(i, nu[0] - 1), 0)
    grid_spec = pltpu.PrefetchScalarGridSpec(
        num_scalar_prefetch=2,
        grid=(half // tn, p // tm),
        in_specs=[pl.BlockSpec((tm, kw), row),
                  pl.BlockSpec((None, k, tn), lambda j, i, te, nu: (te[i], 0, j)),
                  pl.BlockSpec((None, k, tn), lambda j, i, te, nu: (te[i], 0, j + hoff))],
        out_specs=pl.BlockSpec((tm, tn), lambda j, i, te, nu: (i, j)),
        scratch_shapes=[pltpu.VMEM((k, tn), BF16), pltpu.VMEM((k, tn), BF16)])
    return pl.pallas_call(
        _expert_in_body,
        grid_spec=grid_spec,
        out_shape=jax.ShapeDtypeStruct((p, half), BF16),
        compiler_params=_cparams("arbitrary", "arbitrary"),
        name="expert_in",
    )(tile_expert, n_used, xs_words, w_in, w_in)


def _expert_out(gs, w_out, tile_expert, n_used, *, tm, tn=512):
    p, k = gs.shape
    n = w_out.shape[2]
    row = lambda j, i, te, nu: (jnp.minimum(i, nu[0] - 1), 0)
    grid_spec = pltpu.PrefetchScalarGridSpec(
        num_scalar_prefetch=2,
        grid=(n // tn, p // tm),
        in_specs=[pl.BlockSpec((tm, k), row),
                  pl.BlockSpec((None, k, tn), lambda j, i, te, nu: (te[i], 0, j))],
        out_specs=pl.BlockSpec((tm, tn), lambda j, i, te, nu: (i, j)),
        scratch_shapes=[pltpu.VMEM((k, tn), BF16)])
    return pl.pallas_call(
        _expert_out_body,
        grid_spec=grid_spec,
        out_shape=jax.ShapeDtypeStruct((p, n), F32),
        compiler_params=_cparams("arbitrary", "arbitrary"),
        name="expert_out",
    )(tile_expert, n_used, gs, w_out)


def _combine_copies(s1_ref, s2_ref, ys_ref, buf, sem, base, r):
    t = base + r
    return (pltpu.make_async_copy(ys_ref.at[pl.ds(s1_ref[t], 1)], buf.at[0, pl.ds(r, 1)], sem.at[0]),
            pltpu.make_async_copy(ys_ref.at[pl.ds(s2_ref[t], 1)], buf.at[1, pl.ds(r, 1)], sem.at[1]))


def _combine_body(s1_ref, s2_ref, x_ref, info_ref, ys_ref, gate_ref, g_ref, b_ref, o_ref, buf, sem,
                  *, tb, alpha):
    base = pl.program_id(0) * tb

    def issue(r, carry):
        for cp in _combine_copies(s1_ref, s2_ref, ys_ref, buf, sem, base, r):
            cp.start()
        return carry

    def drain(r, carry):
        for cp in _combine_copies(s1_ref, s2_ref, ys_ref, buf, sem, base, r):
            cp.wait()
        return carry

    lax.fori_loop(0, tb, issue, 0)
    lax.fori_loop(0, tb, drain, 0)
    info = info_ref[...]
    y = info[:, 2:3] * buf[0] + info[:, 3:4] * buf[1]
    r = alpha * x_ref[...] + (1.0 + gate_ref[...]) * y
    o_ref[...] = _layer_norm_rows(r, g_ref[...], b_ref[...])


def _combine_postnorm(x, info, ys, slot1, slot2, gate, g, b, *, alpha, tb=128):
    s, d = x.shape
    vec = pl.BlockSpec((1, d), lambda i, a, c: (0, 0))
    grid_spec = pltpu.PrefetchScalarGridSpec(
        num_scalar_prefetch=2,
        grid=(s // tb,),
        in_specs=[pl.BlockSpec((tb, d), lambda i, a, c: (i, 0)),
                  pl.BlockSpec((tb, LANES), lambda i, a, c: (i, 0)),
                  pl.BlockSpec(memory_space=pl.ANY),
                  vec, vec, vec],
        out_specs=pl.BlockSpec((tb, d), lambda i, a, c: (i, 0)),
        scratch_shapes=[pltpu.VMEM((2, tb, d), F32), pltpu.SemaphoreType.DMA((2,))])
    return pl.pallas_call(
        functools.partial(_combine_body, tb=tb, alpha=alpha),
        grid_spec=grid_spec,
        out_shape=jax.ShapeDtypeStruct((s, d), F32),
        compiler_params=_cparams("arbitrary"),
        name="moe_combine_postnorm",
    )(slot1, slot2, x, info, ys, gate.reshape(1, d), g.reshape(1, d), b.reshape(1, d))


def _moe_plan(info, counts, *, tm, n_tiles):
    e1 = info[:, 0].astype(I32)
    e2 = info[:, 1].astype(I32)
    r1 = info[:, 4].astype(I32)
    r2 = info[:, 5].astype(I32)
    cnt = counts[0, :N_EXPERTS].astype(I32)
    tiles = (cnt + tm - 1) // tm
    tile_end = jnp.cumsum(tiles)
    row_off = (tile_end - tiles) * tm
    slot1 = row_off[e1] + r1
    slot2 = row_off[e2] + r2
    tile_expert = jnp.minimum(
        jnp.sum(jnp.arange(n_tiles, dtype=I32)[:, None] >= tile_end[None, :], axis=1),
        N_EXPERTS - 1).astype(I32)
    n_used = tile_end[-1:].astype(I32)
    return slot1, slot2, tile_expert, n_used


def _pad_cols(w, n):
    return jnp.pad(w, ((0, 0), (0, n - w.shape[1])))


def kernel(x, c, ada_w, ada_b, ln_g, ln_b, even_w_in, lam_q1, lam_k1, lam_q2, lam_k2, subln_g, pool_w,
           pool_scale, even_w_out, ffn_w_in, ffn_w_out, odd_w_in, lb_raw, gnorm_g, odd_w_out, router_w,
           exp_w_in, exp_w_out):
    _, seq, d = x.shape
    depth = ada_w.shape[0]
    alpha = (2 * depth) ** 0.25
    a_width = d // 2
    a_heads = a_width // A_VDIM
    x2 = x.reshape(seq, d)

    lb_all = jnp.cumsum(jax.nn.softmax(lb_raw.astype(F32), axis=0), axis=0)
    lb_all = lb_all - lb_all[0]
    mod = _ada_mod(c, ada_w, ada_b)

    def mod_parts(l):
        return [mod[l, i * d:(i + 1) * d] for i in range(6)]

    sh1, sc1, g1, sh2, sc2, g2 = mod_parts(0)
    h = _modulate(x2, sc1, sh1)
    qkvu = _proj([h], even_w_in[0], col_off=0, n_cols=3 * a_width + (d - a_width), out_dtype=BF16,
                 epilogue="qscale", q_cols=a_width, q_scale=A_QKDIM ** -0.5 * math.log2(math.e),
                 name="even_in_proj")
    lam_init = 0.8 - 0.6 * math.exp(-0.3 * 0)
    v_t = _attn_value_rows(qkvu[:, 2 * a_width:3 * a_width], a_heads)
    o_a = _diff_attention(qkvu, v_t, lam_q1[0], lam_k1[0], lam_q2[0], lam_k2[0], subln_g[0],
                          n_heads=a_heads, lam_init=lam_init)
    o_b = _multiscale_pool(qkvu, pool_w[0].astype(BF16), pool_scale[0], col_off=3 * a_width)
    y = _proj([o_a, o_b], even_w_out[0], col_off=0, n_cols=d, out_dtype=F32, name="even_out_proj")
    x2, h = _postnorm(x2, y, g1, ln_g[0, 0], ln_b[0, 0], sc2, sh2, alpha=alpha)

    ff = ffn_w_out.shape[1]
    ffp = -(-ff // 1024) * 1024
    w_in = jnp.concatenate([_pad_cols(ffn_w_in[0][:, :ff].astype(BF16), ffp),
                            _pad_cols(ffn_w_in[0][:, ff:].astype(BF16), ffp)], axis=1)
    w_out = jnp.pad(ffn_w_out[0].astype(BF16), ((0, ffp - ff), (0, 0)))
    gact = _swiglu_in(h, w_in)
    y = _mm_acc(gact, w_out, tk=ffp // 4)
    sh1, sc1, g1n, sh2n, sc2n, g2n = mod_parts(1)
    x2, h = _postnorm(x2, y, g2, ln_g[0, 1], ln_b[0, 1], sc1, sh1, alpha=alpha)

    w_odd = odd_w_in[0]
    qs = _proj([h], w_odd, col_off=0, n_cols=d, out_dtype=BF16, epilogue="silu", name="odd_in_q")
    fg = _proj([h], w_odd, col_off=d, n_cols=d, out_dtype=F32, epilogue="forget", lb=lb_all[1], name="odd_in_f")
    vi = _proj([h], w_odd, col_off=2 * d, n_cols=d, out_dtype=BF16, name="odd_in_i")
    gs = _proj([h], w_odd, col_off=3 * d, n_cols=d, out_dtype=BF16, epilogue="silu", name="odd_in_g")
    o_c = _hgrn2(qs, fg, vi, gs, gnorm_g[0])
    y = _proj([o_c], odd_w_out[0], col_off=0, n_cols=d, out_dtype=F32, name="odd_out_proj")
    x2, h_words = _postnorm(x2, y, g1n, ln_g[1, 0], ln_b[1, 0], sc2n, sh2n, alpha=alpha, pack=True)

    tm_e = 512
    n_tiles = (2 * seq) // tm_e + N_EXPERTS
    info, counts = _router(x2, sc2n, sh2n, router_w[0])
    slot1, slot2, tile_expert, n_used = _moe_plan(info, counts, tm=tm_e, n_tiles=n_tiles)
    xs_words = _dispatch(h_words, slot1, slot2, n_tiles * tm_e)
    gexp = _expert_in(xs_words, exp_w_in[0], tile_expert, n_used, tm=tm_e)
    ys = _expert_out(gexp, exp_w_out[0], tile_expert, n_used, tm=tm_e)
    out = _combine_postnorm(x2, info, ys, slot1, slot2, g2n, ln_g[1, 1], ln_b[1, 1], alpha=alpha)
    return out.reshape(x.shape)
```

```python
import functools
import math

import numpy as np
import jax
import jax.numpy as jnp
from jax import lax
from jax.experimental import pallas as pl
from jax.experimental.pallas import tpu as pltpu

F32 = jnp.float32
BF16 = jnp.bfloat16
I32 = jnp.int32

A_VDIM = 128
A_QKDIM = 64
POOL_WINDOWS = (2, 4, 8, 16)
POOL_HALO = 16
C_KDIM = 128
C_VDIM = 128
HGRN_CHUNK = 64
N_EXPERTS = 8
LN_EPS = 1e-5
RMS_EPS = 1e-6
NEG_INF = -1e30

LANES = 128
VMEM_LIMIT_BYTES = 56 * 1024 * 1024


def _cparams(*sem):
    return pltpu.CompilerParams(dimension_semantics=sem, vmem_limit_bytes=VMEM_LIMIT_BYTES)


def _silu(x):
    return x * (1.0 / (1.0 + jnp.exp(-x)))


def _dot(a, b):
    return jnp.dot(a, b, preferred_element_type=F32)


def _dot_t(a, b):
    return lax.dot_general(a, b, (((1,), (1,)), ((), ())), preferred_element_type=F32)


def _pack_bf16_pairs(h):
    half = h.shape[1] // 2
    bits = pltpu.bitcast(h.astype(BF16).astype(F32), I32)
    return jnp.bitwise_or(bits[:, :half], lax.shift_right_logical(bits[:, half:], 16))


def _unpack_bf16_pairs(w):
    hi = pltpu.bitcast(jnp.bitwise_and(w, -65536), F32).astype(BF16)
    lo = pltpu.bitcast(lax.shift_left(w, 16), F32).astype(BF16)
    return jnp.concatenate([hi, lo], axis=1)


def _ada_body(c_ref, w_ref, b_ref, o_ref):
    ca = _silu(c_ref[...]).astype(BF16)
    o_ref[...] = _dot(ca, w_ref[...].astype(BF16)) + b_ref[...]


def _ada_mod(c, ada_w, ada_b, *, tn=1024):
    depth, d, n = ada_w.shape
    c8 = jnp.broadcast_to(c.astype(F32), (8, d))
    out = pl.pallas_call(
        _ada_body,
        grid=(depth, n // tn),
        in_specs=[pl.BlockSpec((8, d), lambda l, j: (0, 0)),
                  pl.BlockSpec((None, d, tn), lambda l, j: (l, 0, j)),
                  pl.BlockSpec((None, 1, tn), lambda l, j: (l, 0, j))],
        out_specs=pl.BlockSpec((None, 8, tn), lambda l, j: (l, 0, j)),
        out_shape=jax.ShapeDtypeStruct((depth, 8, n), F32),
        compiler_params=_cparams("arbitrary", "arbitrary"),
        name="ada_mod",
    )(c8, ada_w, ada_b.reshape(depth, 1, n))
    return out[:, 0, :]


def _modulate_body(x_ref, sc_ref, sh_ref, o_ref):
    o_ref[...] = (x_ref[...] * (1.0 + sc_ref[...]) + sh_ref[...]).astype(o_ref.dtype)


def _modulate(x, scale, shift, *, tm=512):
    s, d = x.shape
    vec = pl.BlockSpec((1, d), lambda i: (0, 0))
    return pl.pallas_call(
        _modulate_body,
        grid=(s // tm,),
        in_specs=[pl.BlockSpec((tm, d), lambda i: (i, 0)), vec, vec],
        out_specs=pl.BlockSpec((tm, d), lambda i: (i, 0)),
        out_shape=jax.ShapeDtypeStruct((s, d), BF16),
        compiler_params=_cparams("arbitrary"),
        name="modulate",
    )(x, scale.reshape(1, d), shift.reshape(1, d))


def _layer_norm_rows(r, g, b):
    mu = jnp.mean(r, axis=-1, keepdims=True)
    rc = r - mu
    var = jnp.mean(rc * rc, axis=-1, keepdims=True)
    return rc * lax.rsqrt(var + LN_EPS) * g + b


def _postnorm_body(x_ref, y_ref, gate_ref, g_ref, b_ref, sc_ref, sh_ref, xo_ref, ho_ref, *, alpha, pack):
    r = alpha * x_ref[...] + (1.0 + gate_ref[...]) * y_ref[...].astype(F32)
    xn = _layer_norm_rows(r, g_ref[...], b_ref[...])
    xo_ref[...] = xn
    hn = xn * (1.0 + sc_ref[...]) + sh_ref[...]
    ho_ref[...] = _pack_bf16_pairs(hn) if pack else hn.astype(ho_ref.dtype)


def _postnorm(x, y, gate, g, b, nscale, nshift, *, alpha, pack=False, tm=256):
    s, d = x.shape
    row = pl.BlockSpec((tm, d), lambda i: (i, 0))
    vec = pl.BlockSpec((1, d), lambda i: (0, 0))
    hd, hdt = (d // 2, I32) if pack else (d, BF16)
    return pl.pallas_call(
        functools.partial(_postnorm_body, alpha=alpha, pack=pack),
        grid=(s // tm,),
        in_specs=[row, row, vec, vec, vec, vec, vec],
        out_specs=[row, pl.BlockSpec((tm, hd), lambda i: (i, 0))],
        out_shape=[jax.ShapeDtypeStruct((s, d), F32), jax.ShapeDtypeStruct((s, hd), hdt)],
        compiler_params=_cparams("arbitrary"),
        name="postnorm",
    )(x, y, gate.reshape(1, d), g.reshape(1, d), b.reshape(1, d), nscale.reshape(1, d), nshift.reshape(1, d))


def _proj_body(*refs, n_a, epilogue, tn, q_cols, q_scale):
    a_refs, w_ref, extra = refs[:n_a], refs[n_a], refs[n_a + 1:-2]
    o_ref, wb_s = refs[-2], refs[-1]

    @pl.when(pl.program_id(1) == 0)
    def _():
        wb_s[...] = w_ref[...].astype(BF16)

    acc, off = None, 0
    for a_ref in a_refs:
        kp = a_ref.shape[1]
        part = _dot(a_ref[...], wb_s[off:off + kp, :])
        acc = part if acc is None else acc + part
        off += kp
    if epilogue == "silu":
        acc = _silu(acc)
    elif epilogue == "forget":
        lb = extra[0][...]
        acc = lb + (1.0 - lb) * (1.0 / (1.0 + jnp.exp(-acc)))
    elif epilogue == "qscale":
        acc = acc * jnp.where(pl.program_id(0) * tn < q_cols, q_scale, 1.0)
    o_ref[...] = acc.astype(o_ref.dtype)


def _proj(a_parts, w, *, col_off, n_cols, out_dtype, epilogue="id", lb=None, q_cols=0, q_scale=1.0,
          tm=1024, tn=512, name="proj"):
    m = a_parts[0].shape[0]
    k = w.shape[0]
    joff = col_off // tn
    in_specs = [pl.BlockSpec((tm, a.shape[1]), lambda j, i: (i, 0)) for a in a_parts]
    in_specs.append(pl.BlockSpec((k, tn), lambda j, i: (0, j + joff)))
    args = list(a_parts) + [w]
    if epilogue == "forget":
        in_specs.append(pl.BlockSpec((1, tn), lambda j, i: (0, j)))
        args.append(lb.reshape(1, n_cols))
    return pl.pallas_call(
        functools.partial(_proj_body, n_a=len(a_parts), epilogue=epilogue, tn=tn, q_cols=q_cols, q_scale=q_scale),
        grid=(n_cols // tn, m // tm),
        in_specs=in_specs,
        out_specs=pl.BlockSpec((tm, tn), lambda j, i: (i, j)),
        out_shape=jax.ShapeDtypeStruct((m, n_cols), out_dtype),
        scratch_shapes=[pltpu.VMEM((k, tn), BF16)],
        compiler_params=_cparams("arbitrary", "arbitrary"),
        name=name,
    )(*args)


def _swiglu_in_body(a_ref, wa_ref, wb_ref, o_ref, wa_s, wb_s, *, n_valid):
    j = pl.program_id(0)

    @pl.when(pl.program_id(1) == 0)
    def _():
        wa_s[...] = wa_ref[...].astype(BF16)
        wb_s[...] = wb_ref[...].astype(BF16)

    @pl.when(j < n_valid)
    def _():
        a = a_ref[...]
        o_ref[...] = (_silu(_dot(a, wa_s[...])) * _dot(a, wb_s[...])).astype(o_ref.dtype)

    @pl.when(j >= n_valid)
    def _():
        o_ref[...] = jnp.zeros(o_ref.shape, o_ref.dtype)


def _swiglu_in(a, w, n_out, *, tm=1024, tn=256):
    m, k = a.shape
    half = w.shape[1] // 2
    n_valid = half // tn
    col = lambda j: jnp.minimum(j, n_valid - 1)
    return pl.pallas_call(
        functools.partial(_swiglu_in_body, n_valid=n_valid),
        grid=(n_out // tn, m // tm),
        in_specs=[pl.BlockSpec((tm, k), lambda j, i: (i, 0)),
                  pl.BlockSpec((k, tn), lambda j, i: (0, col(j))),
                  pl.BlockSpec((k, tn), lambda j, i: (0, col(j) + n_valid))],
        out_specs=pl.BlockSpec((tm, tn), lambda j, i: (i, j)),
        out_shape=jax.ShapeDtypeStruct((m, n_out), BF16),
        scratch_shapes=[pltpu.VMEM((k, tn), BF16), pltpu.VMEM((k, tn), BF16)],
        compiler_params=_cparams("arbitrary", "arbitrary"),
        name="swiglu_in",
    )(a, w, w)


def _mm_acc_body(a_ref, w_ref, o_ref, acc_ref, *, nk):
    kk = pl.program_id(2)
    part = _dot(a_ref[...], w_ref[...])

    @pl.when(kk == 0)
    def _():
        acc_ref[...] = part

    @pl.when(kk > 0)
    def _():
        acc_ref[...] += part

    @pl.when(kk == nk - 1)
    def _():
        o_ref[...] = acc_ref[...].astype(o_ref.dtype)


def _mm_acc(a, w, *, tm=1024, tn=1024, tk, out_dtype=F32):
    m, k = a.shape
    n = w.shape[1]
    nk = k // tk
    return pl.pallas_call(
        functools.partial(_mm_acc_body, nk=nk),
        grid=(m // tm, n // tn, nk),
        in_specs=[pl.BlockSpec((tm, tk), lambda i, j, kk: (i, kk)),
                  pl.BlockSpec((tk, tn), lambda i, j, kk: (kk, j))],
        out_specs=pl.BlockSpec((tm, tn), lambda i, j, kk: (i, j)),
        out_shape=jax.ShapeDtypeStruct((m, n), out_dtype),
        scratch_shapes=[pltpu.VMEM((tm, tn), F32)],
        compiler_params=_cparams("arbitrary", "arbitrary", "arbitrary"),
        name="mm_acc",
    )(a, w)


ATTN_ONES_ROWS = 16
ATTN_SLAB = 256


def _attn_body(qi_tab, kv_tab, slopes, q_ref, k_ref, vt_ref, pos_ref, lq1_ref, lk1_ref, lq2_ref, lk2_ref, g_ref,
               o_ref, q1_s, q2_s, m1_s, a1_s, m2_s, a2_s, jmi_s, *, tq, tk, lam_init):
    h = pl.program_id(0)
    t = pl.program_id(1)
    qi = qi_tab[t]
    kv = kv_tab[t]
    slope = slopes[h]
    n_sub = tq // tk
    dv = A_VDIM

    @pl.when(t == 0)
    def _():
        jmi_s[...] = lax.broadcasted_iota(I32, (tk, tq), 0) - lax.broadcasted_iota(I32, (tk, tq), 1)

    @pl.when(kv == 0)
    def _():
        q = q_ref[...]
        lane = lax.broadcasted_iota(I32, q.shape, 1)
        zero = jnp.zeros_like(q)
        sv = jnp.full(q.shape, slope, F32)
        s_hi = sv.astype(BF16).astype(F32)
        s_lo = sv - s_hi
        coef = jnp.where(lane == 0, s_hi * 16.0, jnp.where(lane == 1, s_lo * 16.0,
                         jnp.where(lane == 2, s_hi, jnp.where(lane == 3, s_lo, 0.0)))).astype(BF16)
        q1_s[...] = jnp.concatenate([jnp.where(lane < A_QKDIM, q, zero), coef], axis=1)
        q2_s[...] = jnp.concatenate([jnp.where(lane >= A_QKDIM, q, zero), coef], axis=1)
        for m_s, a_s in ((m1_s, a1_s), (m2_s, a2_s)):
            m_s[...] = jnp.full(m_s.shape, -jnp.inf, F32)
            a_s[...] = jnp.zeros(a_s.shape, F32)

    shift = kv * tk - qi * tq

    def step(key_off):
        k_aug = jnp.concatenate([k_ref[...], pos_ref[...]], axis=1)
        vt = vt_ref[...]
        c = slope * shift.astype(F32)
        chains = []
        for c0 in range(0, tq, ATTN_SLAB):
            if key_off is not None and c0 + ATTN_SLAB - 1 < key_off:
                continue
            masked = key_off is not None and c0 < key_off + tk - 1
            for q_s, m_s, a_s in ((q1_s, m1_s, a1_s), (q2_s, m2_s, a2_s)):
                chains.append((q_s, m_s, a_s, slice(c0, c0 + ATTN_SLAB), masked))
        scores = [_dot_t(k_aug, q_s[cols, :]) for q_s, _, _, cols, _ in chains]
        for s, (_, m_s, a_s, cols, masked) in zip(scores, chains):
            if masked:
                s = jnp.where(jmi_s[:, cols] <= -key_off, s, NEG_INF)
            m_prev = m_s[:, cols]
            m_new = jnp.maximum(m_prev, jnp.max(s, axis=0, keepdims=True) + c)
            p = jnp.exp2(s - (m_new - c))
            a_s[:, cols] = jnp.exp2(m_prev - m_new) * a_s[:, cols] + _dot(vt, p.astype(BF16))
            m_s[:, cols] = m_new

    @pl.when(kv < qi * n_sub)
    def _():
        step(None)

    for r in range(n_sub):
        @pl.when(kv == qi * n_sub + r)
        def _():
            step(r * tk)

    @pl.when(kv == (qi + 1) * n_sub - 1)
    def _():
        lam = (jnp.exp(jnp.sum(lq1_ref[...] * lk1_ref[...], axis=-1, keepdims=True))
               - jnp.exp(jnp.sum(lq2_ref[...] * lk2_ref[...], axis=-1, keepdims=True)) + lam_init)
        a1 = a1_s[...]
        a2 = a2_s[...]
        o = a1[:dv] / a1[dv:dv + 1] - lam * (a2[:dv] / a2[dv:dv + 1])
        ms = jnp.mean(o * o, axis=0, keepdims=True)
        on = o * lax.rsqrt(ms + RMS_EPS) * g_ref[...] * (1.0 - lam_init)
        o_ref[...] = on.T.astype(o_ref.dtype)


def _diff_attention(qkvu, v_t, lam_q1, lam_k1, lam_q2, lam_k2, subln_g, *, n_heads, lam_init, tq=2048, tk=512):
    s = qkvu.shape[0]
    n_sub = tq // tk
    pairs = [(i, j) for i in range(s // tq) for j in range((i + 1) * n_sub)]
    qi_tab = jnp.asarray([p[0] for p in pairs], I32)
    kv_tab = jnp.asarray([p[1] for p in pairs], I32)
    slopes = jnp.asarray([math.log2(math.e) * 2.0 ** (-8.0 * (i + 1) / n_heads) for i in range(n_heads)], F32)
    hd = A_VDIM
    hv = hd + ATTN_ONES_ROWS
    j = np.arange(tk)
    pos = np.zeros((tk, hd), np.float32)
    pos[:, 0] = pos[:, 1] = j // 16
    pos[:, 2] = pos[:, 3] = j % 16
    small = pl.BlockSpec((1, A_QKDIM), lambda h, t, qt, kt, sl: (0, 0))
    grid_spec = pltpu.PrefetchScalarGridSpec(
        num_scalar_prefetch=3,
        grid=(n_heads, len(pairs)),
        in_specs=[pl.BlockSpec((tq, hd), lambda h, t, qt, kt, sl: (qt[t], h)),
                  pl.BlockSpec((tk, hd), lambda h, t, qt, kt, sl: (kt[t], n_heads + h)),
                  pl.BlockSpec((hv, tk), lambda h, t, qt, kt, sl: (h, kt[t])),
                  pl.BlockSpec((tk, hd), lambda h, t, qt, kt, sl: (0, 0)),
                  small, small, small, small,
                  pl.BlockSpec((hd, 1), lambda h, t, qt, kt, sl: (0, 0))],
        out_specs=pl.BlockSpec((tq, hd), lambda h, t, qt, kt, sl: (qt[t], h)),
        scratch_shapes=[pltpu.VMEM((tq, 2 * hd), BF16), pltpu.VMEM((tq, 2 * hd), BF16),
                        pltpu.VMEM((1, tq), F32), pltpu.VMEM((hv, tq), F32),
                        pltpu.VMEM((1, tq), F32), pltpu.VMEM((hv, tq), F32),
                        pltpu.VMEM((tk, tq), I32)])
    return pl.pallas_call(
        functools.partial(_attn_body, tq=tq, tk=tk, lam_init=lam_init),
        grid_spec=grid_spec,
        out_shape=jax.ShapeDtypeStruct((s, n_heads * hd), BF16),
        compiler_params=_cparams("arbitrary", "arbitrary"),
        name="diff_attention",
    )(qi_tab, kv_tab, slopes, qkvu, qkvu, v_t, jnp.asarray(pos, BF16),
      lam_q1.reshape(1, -1), lam_k1.reshape(1, -1), lam_q2.reshape(1, -1), lam_k2.reshape(1, -1),
      subln_g.reshape(-1, 1))


def _attn_value_rows(v, n_heads):
    s = v.shape[0]
    vt = v.T.reshape(n_heads, A_VDIM, s)
    ones = jnp.ones((n_heads, ATTN_ONES_ROWS, s), v.dtype)
    return jnp.concatenate([vt, ones], axis=1).reshape(n_heads * (A_VDIM + ATTN_ONES_ROWS), s)


def _pool_body(ucur_ref, uprev_ref, w_ref, sc_ref, o_ref, *, t_blk):
    g = pl.program_id(0)
    i = pl.program_id(1)
    win = jnp.left_shift(2, g)
    row = lax.broadcasted_iota(I32, (t_blk, t_blk), 0)
    col = lax.broadcasted_iota(I32, (t_blk, t_blk), 1)
    d = row - col
    band = jnp.where(jnp.logical_and(d >= 0, d < win), 1.0, 0.0).astype(BF16)
    rowp = lax.broadcasted_iota(I32, (t_blk, POOL_HALO), 0)
    colp = lax.broadcasted_iota(I32, (t_blk, POOL_HALO), 1)
    dp = rowp + POOL_HALO - colp
    bandp = jnp.where(jnp.logical_and(dp < win, i > 0), 1.0, 0.0).astype(BF16)
    u = ucur_ref[...]
    usum = _dot(band, u) + _dot(bandp, uprev_ref[...])
    tpos = i * t_blk + lax.broadcasted_iota(I32, (t_blk, 1), 0)
    cnt = jnp.minimum(tpos + 1, win).astype(F32)
    dev = usum / cnt - u.astype(F32)
    y = _dot(dev.astype(BF16), w_ref[...]) * sc_ref[...]
    o_ref[...] = y.astype(o_ref.dtype)


def _multiscale_pool(qkvu, pool_w, pool_scale, *, col_off, t_blk=256):
    s = qkvu.shape[0]
    ng, gd, _ = pool_w.shape
    goff = col_off // gd
    hb = t_blk // POOL_HALO
    return pl.pallas_call(
        functools.partial(_pool_body, t_blk=t_blk),
        grid=(ng, s // t_blk),
        in_specs=[pl.BlockSpec((t_blk, gd), lambda g, i: (i, goff + g)),
                  pl.BlockSpec((POOL_HALO, gd), lambda g, i: (jnp.maximum(i * hb - 1, 0), goff + g)),
                  pl.BlockSpec((None, gd, gd), lambda g, i: (g, 0, 0)),
                  pl.BlockSpec((1, gd), lambda g, i: (0, g))],
        out_specs=pl.BlockSpec((t_blk, gd), lambda g, i: (i, g)),
        out_shape=jax.ShapeDtypeStruct((s, ng * gd), BF16),
        compiler_params=_cparams("arbitrary", "arbitrary"),
        name="multiscale_pool",
    )(qkvu, qkvu, pool_w, pool_scale.reshape(1, -1))


def _hgrn_tables():
    c = HGRN_CHUNK
    idx = np.arange(c)
    mats, masks = [], []
    h = c // 2
    while h >= 1:
        upper = (idx % (2 * h)) >= h
        e = idx - (idx % (2 * h)) + h - 1
        u = idx[None, :]
        pq = (upper[:, None] & (u > e[:, None]) & (u <= idx[:, None]))
        pk = ((~upper)[:, None] & (u > idx[:, None]) & (u <= e[:, None]))
        mats += [pq, pk]
        masks.append(upper[:, None] & (~upper)[None, :] & ((idx[:, None] // (2 * h)) == (idx[None, :] // (2 * h))))
        h //= 2
    masks.append(idx[:, None] == idx[None, :])
    mats.append(idx[None, :] <= idx[:, None])
    mats.append(idx[None, :] > idx[:, None])
    return (np.concatenate(mats, axis=0).astype(np.float32),
            np.stack(masks, axis=0).astype(np.float32))


def _hgrn_body(q_ref, f_ref, v_ref, gs_ref, p_ref, mask_ref, gn_ref, o_ref, st_ref, *, t_blk, n_grp):
    c = HGRN_CHUNK
    n_lvl = mask_ref.shape[0] - 1
    heads = range(n_grp)

    @pl.when(pl.program_id(1) == 0)
    def _():
        st_ref[...] = jnp.zeros(st_ref.shape, F32)

    def col(x, hg):
        return x[:, hg * C_KDIM:(hg + 1) * C_KDIM]

    def stack(parts):
        return jnp.concatenate(parts, axis=0)

    def block_diag(parts):
        zero = jnp.zeros_like(parts[0])
        return stack([jnp.concatenate([parts[hg] if j == hg else zero for j in heads], axis=1) for hg in heads])

    def chunk(ci, carry):
        rows = pl.ds(pl.multiple_of(ci * c, c), c)
        f = f_ref[rows, :]
        lf = jnp.log(f)
        kk = 1.0 - f
        hi = lf.astype(BF16)
        r1 = lf - hi.astype(F32)
        mid = r1.astype(BF16)
        lo = (r1 - mid.astype(F32)).astype(BF16)
        ed = jnp.exp(_dot(p_ref[...], stack([hi, mid, lo])))
        e = [col(ed, hg) for hg in heads]
        q = q_ref[rows, :].astype(F32)
        qh = [col(q, hg) for hg in heads]
        kh = [col(kk, hg) for hg in heads]
        attn = mask_ref[n_lvl] * _dot_t(stack([x.astype(BF16) for x in qh]), stack([x.astype(BF16) for x in kh]))
        for lv in range(n_lvl):
            qs = stack([(qh[hg] * e[hg][2 * lv * c:(2 * lv + 1) * c]).astype(BF16) for hg in heads])
            ks = stack([(kh[hg] * e[hg][(2 * lv + 1) * c:(2 * lv + 2) * c]).astype(BF16) for hg in heads])
            attn = attn + mask_ref[lv] * _dot_t(qs, ks)
        eb = [e[hg][2 * n_lvl * c:(2 * n_lvl + 1) * c] for hg in heads]
        el = [e[hg][(2 * n_lvl + 1) * c:(2 * n_lvl + 2) * c] for hg in heads]
        v = v_ref[rows, :]
        v_st = stack([col(v, hg) for hg in heads])
        st = st_ref[...]
        q_bd = block_diag([(qh[hg] * eb[hg]).astype(BF16) for hg in heads])
        o = _dot(attn.astype(BF16), v_st) + _dot_t(q_bd, st.astype(BF16))
        k_bd = block_diag([(kh[hg] * el[hg]).astype(BF16) for hg in heads])
        decay = jnp.concatenate([eb[hg][c - 1:c, :] for hg in heads], axis=1)
        st_ref[...] = st * decay + lax.dot_general(v_st, k_bd, (((0,), (0,)), ((), ())),
                                                   preferred_element_type=F32)
        ms = jnp.mean(o * o, axis=-1, keepdims=True)
        on = o * lax.rsqrt(ms + RMS_EPS) * gn_ref[...]
        gs = gs_ref[rows, :].astype(F32)
        for hg in heads:
            o_ref[rows, hg * C_VDIM:(hg + 1) * C_VDIM] = (on[hg * c:(hg + 1) * c] * col(gs, hg)).astype(o_ref.dtype)
        return carry

    lax.fori_loop(0, t_blk // c, chunk, 0)


def _hgrn2(qs, forget, v, gs, gnorm_g, *, t_blk=512, n_grp=4):
    s, d = qs.shape
    n_heads = d // C_KDIM
    p_np, mask_np = _hgrn_tables()
    p_mat = jnp.asarray(np.concatenate([p_np] * 3, axis=1), BF16)
    masks = jnp.asarray(np.stack([np.kron(np.eye(n_grp, dtype=np.float32), m) for m in mask_np]), F32)
    blk = lambda: pl.BlockSpec((t_blk, n_grp * C_KDIM), lambda h, i: (i, h))
    return pl.pallas_call(
        functools.partial(_hgrn_body, t_blk=t_blk, n_grp=n_grp),
        grid=(n_heads // n_grp, s // t_blk),
        in_specs=[blk(), blk(), blk(), blk(),
                  pl.BlockSpec(p_mat.shape, lambda h, i: (0, 0)),
                  pl.BlockSpec(masks.shape, lambda h, i: (0, 0, 0)),
                  pl.BlockSpec((1, C_VDIM), lambda h, i: (0, 0))],
        out_specs=blk(),
        out_shape=jax.ShapeDtypeStruct((s, d), BF16),
        scratch_shapes=[pltpu.VMEM((C_VDIM, n_grp * C_KDIM), F32)],
        compiler_params=_cparams("arbitrary", "arbitrary"),
        name="hgrn2",
    )(qs, forget, v, gs, p_mat, masks, gnorm_g.reshape(1, -1))


def _router_body(x_ref, sc_ref, sh_ref, whi_ref, wlo_ref, info_ref, cnt_ref, carry_s, *, tm):
    @pl.when(pl.program_id(0) == 0)
    def _():
        carry_s[...] = jnp.zeros(carry_s.shape, F32)

    hmod = x_ref[...] * (1.0 + sc_ref[...]) + sh_ref[...]
    hh = hmod.astype(BF16)
    hl = (hmod - hh.astype(F32)).astype(BF16)
    whi = whi_ref[...]
    logits = _dot(hh, whi) + _dot(hl, whi) + _dot(hh, wlo_ref[...])
    lane = lax.broadcasted_iota(I32, (tm, LANES), 1)
    logits = jnp.where(lane < N_EXPERTS, logits, -jnp.inf)
    m1 = jnp.max(logits, axis=-1, keepdims=True)
    e1 = jnp.min(jnp.where(logits == m1, lane, LANES), axis=-1, keepdims=True)
    rest = jnp.where(lane == e1, -jnp.inf, logits)
    m2 = jnp.max(rest, axis=-1, keepdims=True)
    e2 = jnp.min(jnp.where(rest == m2, lane, LANES), axis=-1, keepdims=True)
    ex = jnp.exp(m2 - m1)
    g1 = 1.0 / (1.0 + ex)
    g2 = ex / (1.0 + ex)
    onehot = jnp.where(jnp.logical_or(lane == e1, lane == e2), 1.0, 0.0)
    row = lax.broadcasted_iota(I32, (tm, tm), 0)
    col = lax.broadcasted_iota(I32, (tm, tm), 1)
    before = jnp.where(row > col, 1.0, 0.0).astype(BF16)
    cum = _dot(before, onehot.astype(BF16)) + carry_s[...]
    r1 = jnp.sum(jnp.where(lane == e1, cum, 0.0), axis=-1, keepdims=True)
    r2 = jnp.sum(jnp.where(lane == e2, cum, 0.0), axis=-1, keepdims=True)
    carry_s[...] = carry_s[...] + jnp.sum(onehot, axis=0, keepdims=True)
    cnt_ref[...] = carry_s[...]
    info = jnp.where(lane == 0, e1.astype(F32), 0.0)
    info = jnp.where(lane == 1, e2.astype(F32), info)
    info = jnp.where(lane == 2, g1, info)
    info = jnp.where(lane == 3, g2, info)
    info = jnp.where(lane == 4, r1, info)
    info = jnp.where(lane == 5, r2, info)
    info_ref[...] = info


def _router(x, scale, shift, router_w, *, tm=256):
    s, d = x.shape
    wpad = jnp.zeros((d, LANES), F32).at[:, :N_EXPERTS].set(router_w)
    whi = wpad.astype(BF16)
    wlo = (wpad - whi.astype(F32)).astype(BF16)
    vec = pl.BlockSpec((1, d), lambda i: (0, 0))
    wspec = pl.BlockSpec((d, LANES), lambda i: (0, 0))
    return pl.pallas_call(
        functools.partial(_router_body, tm=tm),
        grid=(s // tm,),
        in_specs=[pl.BlockSpec((tm, d), lambda i: (i, 0)), vec, vec, wspec, wspec],
        out_specs=[pl.BlockSpec((tm, LANES), lambda i: (i, 0)),
                   pl.BlockSpec((1, LANES), lambda i: (0, 0))],
        out_shape=[jax.ShapeDtypeStruct((s, LANES), F32), jax.ShapeDtypeStruct((1, LANES), F32)],
        scratch_shapes=[pltpu.VMEM((1, LANES), F32)],
        compiler_params=_cparams("arbitrary"),
        name="router",
    )(x, scale.reshape(1, d), shift.reshape(1, d), whi, wlo)


def _dispatch_copies(s1_ref, s2_ref, h_ref, out_ref, sem, base, r):
    t = base + r
    src = h_ref.at[pl.ds(r, 1)]
    return (pltpu.make_async_copy(src, out_ref.at[pl.ds(s1_ref[t], 1)], sem.at[0]),
            pltpu.make_async_copy(src, out_ref.at[pl.ds(s2_ref[t], 1)], sem.at[1]))


def _dispatch_body(s1_ref, s2_ref, h_ref, init_ref, out_ref, sem, *, tb):
    del init_ref
    base = pl.program_id(0) * tb

    def issue(r, carry):
        for cp in _dispatch_copies(s1_ref, s2_ref, h_ref, out_ref, sem, base, r):
            cp.start()
        return carry

    def drain(r, carry):
        for cp in _dispatch_copies(s1_ref, s2_ref, h_ref, out_ref, sem, base, r):
            cp.wait()
        return carry

    lax.fori_loop(0, tb, issue, 0)
    lax.fori_loop(0, tb, drain, 0)


def _dispatch(h_words, slot1, slot2, n_rows, *, tb=256):
    s, dw = h_words.shape
    init = jnp.zeros((n_rows, dw), h_words.dtype)
    grid_spec = pltpu.PrefetchScalarGridSpec(
        num_scalar_prefetch=2,
        grid=(s // tb,),
        in_specs=[pl.BlockSpec((tb, dw), lambda i, a, b: (i, 0)),
                  pl.BlockSpec(memory_space=pl.ANY)],
        out_specs=pl.BlockSpec(memory_space=pl.ANY),
        scratch_shapes=[pltpu.SemaphoreType.DMA((2,))])
    return pl.pallas_call(
        functools.partial(_dispatch_body, tb=tb),
        grid_spec=grid_spec,
        out_shape=jax.ShapeDtypeStruct((n_rows, dw), h_words.dtype),
        input_output_aliases={3: 0},
        compiler_params=_cparams("arbitrary"),
        name="moe_dispatch",
    )(slot1, slot2, h_words, init)


def _expert_tile_changed(te_ref, i):
    return jnp.logical_or(i == 0, te_ref[i] != te_ref[jnp.maximum(i - 1, 0)])


def _expert_in_body(te_ref, nu_ref, x_ref, wa_ref, wb_ref, o_ref, wa_s, wb_s):
    i = pl.program_id(1)

    @pl.when(_expert_tile_changed(te_ref, i))
    def _():
        wa_s[...] = wa_ref[...].astype(BF16)
        wb_s[...] = wb_ref[...].astype(BF16)

    @pl.when(i < nu_ref[0])
    def _():
        x = _unpack_bf16_pairs(x_ref[...])
        o_ref[...] = (_silu(_dot(x, wa_s[...])) * _dot(x, wb_s[...])).astype(o_ref.dtype)

    @pl.when(i >= nu_ref[0])
    def _():
        o_ref[...] = jnp.zeros(o_ref.shape, o_ref.dtype)


def _expert_out_body(te_ref, nu_ref, x_ref, w_ref, o_ref, w_s):
    i = pl.program_id(1)

    @pl.when(_expert_tile_changed(te_ref, i))
    def _():
        w_s[...] = w_ref[...].astype(BF16)

    @pl.when(i < nu_ref[0])
    def _():
        o_ref[...] = _dot(x_ref[...], w_s[...]).astype(o_ref.dtype)

    @pl.when(i >= nu_ref[0])
    def _():
        o_ref[...] = jnp.zeros(o_ref.shape, o_ref.dtype)


def _expert_in(xs_words, w_in, tile_expert, n_used, *, tm, tn=512):
    p, kw = xs_words.shape
    k = 2 * kw
    half = w_in.shape[2] // 2
    hoff = half // tn
    row = lambda j, i, te, nu: (jnp.minimum(i, nu[0] - 1), 0)
    grid_spec = pltpu.PrefetchScalarGridSpec(
        num_scalar_prefetch=2,
        grid=(half // tn, p // tm),
        in_specs=[pl.BlockSpec((tm, kw), row),
                  pl.BlockSpec((None, k, tn), lambda j, i, te, nu: (te[i], 0, j)),
                  pl.BlockSpec((None, k, tn), lambda j, i, te, nu: (te[i], 0, j + hoff))],
        out_specs=pl.BlockSpec((tm, tn), lambda j, i, te, nu: (i, j)),
        scratch_shapes=[pltpu.VMEM((k, tn), BF16), pltpu.VMEM((k, tn), BF16)])
    return pl.pallas_call(
        _expert_in_body,
        grid_spec=grid_spec,
        out_shape=jax.ShapeDtypeStruct((p, half), BF16),
        compiler_params=_cparams("arbitrary", "arbitrary"),
        name="expert_in",
    )(tile_expert, n_used, xs_words, w_in, w_in)


def _expert_out(gs, w_out, tile_expert, n_used, *, tm, tn=1024):
    p, k = gs.shape
    n = w_out.shape[2]
    row = lambda j, i, te, nu: (jnp.minimum(i, nu[0] - 1), 0)
    grid_spec = pltpu.PrefetchScalarGridSpec(
        num_scalar_prefetch=2,
        grid=(n // tn, p // tm),
        in_specs=[pl.BlockSpec((tm, k), row),
                  pl.BlockSpec((None, k, tn), lambda j, i, te, nu: (te[i], 0, j))],
        out_specs=pl.BlockSpec((tm, tn), lambda j, i, te, nu: (i, j)),
        scratch_shapes=[pltpu.VMEM((k, tn), BF16)])
    return pl.pallas_call(
        _expert_out_body,
        grid_spec=grid_spec,
        out_shape=jax.ShapeDtypeStruct((p, n), F32),
        compiler_params=_cparams("arbitrary", "arbitrary"),
        name="expert_out",
    )(tile_expert, n_used, gs, w_out)


def _combine_copies(s1_ref, s2_ref, ys_ref, buf, sem, blk, slot, r, tb):
    t = blk * tb + r
    return (pltpu.make_async_copy(ys_ref.at[pl.ds(s1_ref[t], 1)], buf.at[slot, 0, pl.ds(r, 1)], sem.at[slot, 0]),
            pltpu.make_async_copy(ys_ref.at[pl.ds(s2_ref[t], 1)], buf.at[slot, 1, pl.ds(r, 1)], sem.at[slot, 1]))


def _combine_body(s1_ref, s2_ref, x_ref, info_ref, ys_ref, gate_ref, g_ref, b_ref, o_ref, buf, sem,
                  *, tb, alpha):
    i = pl.program_id(0)
    slot = lax.rem(i, 2)

    def gather(blk, slt):
        def issue(r, carry):
            for cp in _combine_copies(s1_ref, s2_ref, ys_ref, buf, sem, blk, slt, r, tb):
                cp.start()
            return carry
        lax.fori_loop(0, tb, issue, 0)

    @pl.when(i == 0)
    def _():
        gather(i, slot)

    @pl.when(i + 1 < pl.num_programs(0))
    def _():
        gather(i + 1, 1 - slot)

    def drain(r, carry):
        for cp in _combine_copies(s1_ref, s2_ref, ys_ref, buf, sem, i, slot, r, tb):
            cp.wait()
        return carry

    lax.fori_loop(0, tb, drain, 0)
    info = info_ref[...]
    y = info[:, 2:3] * buf[slot, 0] + info[:, 3:4] * buf[slot, 1]
    r = alpha * x_ref[...] + (1.0 + gate_ref[...]) * y
    o_ref[...] = _layer_norm_rows(r, g_ref[...], b_ref[...])


def _combine_postnorm(x, info, ys, slot1, slot2, gate, g, b, *, alpha, tb=128):
    s, d = x.shape
    vec = pl.BlockSpec((1, d), lambda i, a, c: (0, 0))
    grid_spec = pltpu.PrefetchScalarGridSpec(
        num_scalar_prefetch=2,
        grid=(s // tb,),
        in_specs=[pl.BlockSpec((tb, d), lambda i, a, c: (i, 0)),
                  pl.BlockSpec((tb, LANES), lambda i, a, c: (i, 0)),
                  pl.BlockSpec(memory_space=pl.ANY),
                  vec, vec, vec],
        out_specs=pl.BlockSpec((tb, d), lambda i, a, c: (i, 0)),
        scratch_shapes=[pltpu.VMEM((2, 2, tb, d), F32), pltpu.SemaphoreType.DMA((2, 2))])
    return pl.pallas_call(
        functools.partial(_combine_body, tb=tb, alpha=alpha),
        grid_spec=grid_spec,
        out_shape=jax.ShapeDtypeStruct((s, d), F32),
        compiler_params=_cparams("arbitrary"),
        name="moe_combine_postnorm",
    )(slot1, slot2, x, info, ys, gate.reshape(1, d), g.reshape(1, d), b.reshape(1, d))


def _moe_plan(info, counts, *, tm, n_tiles):
    e1 = info[:, 0].astype(I32)
    e2 = info[:, 1].astype(I32)
    r1 = info[:, 4].astype(I32)
    r2 = info[:, 5].astype(I32)
    cnt = counts[0, :N_EXPERTS].astype(I32)
    tiles = (cnt + tm - 1) // tm
    tile_end = jnp.cumsum(tiles)
    row_off = (tile_end - tiles) * tm
    slot1 = row_off[e1] + r1
    slot2 = row_off[e2] + r2
    tile_expert = jnp.minimum(
        jnp.sum(jnp.arange(n_tiles, dtype=I32)[:, None] >= tile_end[None, :], axis=1),
        N_EXPERTS - 1).astype(I32)
    n_used = tile_end[-1:].astype(I32)
    return slot1, slot2, tile_expert, n_used


def kernel(x, c, ada_w, ada_b, ln_g, ln_b, even_w_in, lam_q1, lam_k1, lam_q2, lam_k2, subln_g, pool_w,
           pool_scale, even_w_out, ffn_w_in, ffn_w_out, odd_w_in, lb_raw, gnorm_g, odd_w_out, router_w,
           exp_w_in, exp_w_out):
    _, seq, d = x.shape
    depth = ada_w.shape[0]
    alpha = (2 * depth) ** 0.25
    a_width = d // 2
    a_heads = a_width // A_VDIM
    x2 = x.reshape(seq, d)

    lb_all = jnp.cumsum(jax.nn.softmax(lb_raw.astype(F32), axis=0), axis=0)
    lb_all = lb_all - lb_all[0]
    mod = _ada_mod(c, ada_w, ada_b)

    def mod_parts(l):
        return [mod[l, i * d:(i + 1) * d] for i in range(6)]

    sh1, sc1, g1, sh2, sc2, g2 = mod_parts(0)
    h = _modulate(x2, sc1, sh1)
    qkvu = _proj([h], even_w_in[0], col_off=0, n_cols=3 * a_width + (d - a_width), out_dtype=BF16,
                 epilogue="qscale", q_cols=a_width, q_scale=A_QKDIM ** -0.5 * math.log2(math.e),
                 name="even_in_proj")
    lam_init = 0.8 - 0.6 * math.exp(-0.3 * 0)
    v_t = _attn_value_rows(qkvu[:, 2 * a_width:3 * a_width], a_heads)
    o_a = _diff_attention(qkvu, v_t, lam_q1[0], lam_k1[0], lam_q2[0], lam_k2[0], subln_g[0],
                          n_heads=a_heads, lam_init=lam_init)
    o_b = _multiscale_pool(qkvu, pool_w[0].astype(BF16), pool_scale[0], col_off=3 * a_width)
    y = _proj([o_a, o_b], even_w_out[0], col_off=0, n_cols=d, out_dtype=F32, name="even_out_proj")
    x2, h = _postnorm(x2, y, g1, ln_g[0, 0], ln_b[0, 0], sc2, sh2, alpha=alpha)

    ff = ffn_w_out.shape[1]
    ffp = -(-ff // 1024) * 1024
    w_out = jnp.pad(ffn_w_out[0].astype(BF16), ((0, ffp - ff), (0, 0)))
    gact = _swiglu_in(h, ffn_w_in[0], ffp)
    y = _mm_acc(gact, w_out, tk=ffp // 4)
    sh1, sc1, g1n, sh2n, sc2n, g2n = mod_parts(1)
    x2, h = _postnorm(x2, y, g2, ln_g[0, 1], ln_b[0, 1], sc1, sh1, alpha=alpha)

    w_odd = odd_w_in[0]
    qs = _proj([h], w_odd, col_off=0, n_cols=d, out_dtype=BF16, epilogue="silu", name="odd_in_q")
    fg = _proj([h], w_odd, col_off=d, n_cols=d, out_dtype=F32, epilogue="forget", lb=lb_all[1], name="odd_in_f")
    vi = _proj([h], w_odd, col_off=2 * d, n_cols=d, out_dtype=BF16, name="odd_in_i")
    gs = _proj([h], w_odd, col_off=3 * d, n_cols=d, out_dtype=BF16, epilogue="silu", name="odd_in_g")
    o_c = _hgrn2(qs, fg, vi, gs, gnorm_g[0])
    y = _proj([o_c], odd_w_out[0], col_off=0, n_cols=d, out_dtype=F32, name="odd_out_proj")
    x2, h_words = _postnorm(x2, y, g1n, ln_g[1, 0], ln_b[1, 0], sc2n, sh2n, alpha=alpha, pack=True)

    tm_e = 512
    n_tiles = (2 * seq) // tm_e + N_EXPERTS
    info, counts = _router(x2, sc2n, sh2n, router_w[0])
    slot1, slot2, tile_expert, n_used = _moe_plan(info, counts, tm=tm_e, n_tiles=n_tiles)
    xs_words = _dispatch(h_words, slot1, slot2, n_tiles * tm_e)
    gexp = _expert_in(xs_words, exp_w_in[0], tile_expert, n_used, tm=tm_e)
    ys = _expert_out(gexp, exp_w_out[0], tile_expert, n_used, tm=tm_e)
    out = _combine_postnorm(x2, info, ys, slot1, slot2, g2n, ln_g[1, 1], ln_b[1, 1], alpha=alpha)
    return out.reshape(x.shape)
```

```python
import functools
import math

import numpy as np
import jax
import jax.numpy as jnp
from jax import lax
from jax.experimental import pallas as pl
from jax.experimental.pallas import tpu as pltpu

F32 = jnp.float32
BF16 = jnp.bfloat16
I32 = jnp.int32

A_VDIM = 128
A_QKDIM = 64
POOL_WINDOWS = (2, 4, 8, 16)
POOL_HALO = 16
C_KDIM = 128
C_VDIM = 128
HGRN_CHUNK = 64
N_EXPERTS = 8
LN_EPS = 1e-5
RMS_EPS = 1e-6
NEG_INF = -1e30

LANES = 128
VMEM_LIMIT_BYTES = 56 * 1024 * 1024
DMA_ISSUE_UNROLL = 8


def _cparams(*sem):
    return pltpu.CompilerParams(dimension_semantics=sem, vmem_limit_bytes=VMEM_LIMIT_BYTES)


def _silu(x):
    return x * (1.0 / (1.0 + jnp.exp(-x)))


def _dot(a, b):
    return jnp.dot(a, b, preferred_element_type=F32)


def _dot_t(a, b):
    return lax.dot_general(a, b, (((1,), (1,)), ((), ())), preferred_element_type=F32)


def _pack_bf16_pairs(h):
    half = h.shape[1] // 2
    bits = pltpu.bitcast(h.astype(BF16).astype(F32), I32)
    return jnp.bitwise_or(bits[:, :half], lax.shift_right_logical(bits[:, half:], 16))


def _unpack_bf16_pairs(w):
    hi = pltpu.bitcast(jnp.bitwise_and(w, -65536), F32).astype(BF16)
    lo = pltpu.bitcast(lax.shift_left(w, 16), F32).astype(BF16)
    return jnp.concatenate([hi, lo], axis=1)


def _ada_body(c_ref, w_ref, b_ref, o_ref):
    ca = _silu(c_ref[...]).astype(BF16)
    o_ref[...] = _dot(ca, w_ref[...].astype(BF16)) + b_ref[...]


def _ada_mod(c, ada_w, ada_b, *, tn=1024):
    depth, d, n = ada_w.shape
    c8 = jnp.broadcast_to(c.astype(F32), (8, d))
    out = pl.pallas_call(
        _ada_body,
        grid=(depth, n // tn),
        in_specs=[pl.BlockSpec((8, d), lambda l, j: (0, 0)),
                  pl.BlockSpec((None, d, tn), lambda l, j: (l, 0, j)),
                  pl.BlockSpec((None, 1, tn), lambda l, j: (l, 0, j))],
        out_specs=pl.BlockSpec((None, 8, tn), lambda l, j: (l, 0, j)),
        out_shape=jax.ShapeDtypeStruct((depth, 8, n), F32),
        compiler_params=_cparams("arbitrary", "arbitrary"),
        name="ada_mod",
    )(c8, ada_w, ada_b.reshape(depth, 1, n))
    return out[:, 0, :]


def _modulate_body(x_ref, sc_ref, sh_ref, o_ref):
    o_ref[...] = (x_ref[...] * (1.0 + sc_ref[...]) + sh_ref[...]).astype(o_ref.dtype)


def _modulate(x, scale, shift, *, tm=512):
    s, d = x.shape
    vec = pl.BlockSpec((1, d), lambda i: (0, 0))
    return pl.pallas_call(
        _modulate_body,
        grid=(s // tm,),
        in_specs=[pl.BlockSpec((tm, d), lambda i: (i, 0)), vec, vec],
        out_specs=pl.BlockSpec((tm, d), lambda i: (i, 0)),
        out_shape=jax.ShapeDtypeStruct((s, d), BF16),
        compiler_params=_cparams("arbitrary"),
        name="modulate",
    )(x, scale.reshape(1, d), shift.reshape(1, d))


def _layer_norm_rows(r, g, b):
    mu = jnp.mean(r, axis=-1, keepdims=True)
    rc = r - mu
    var = jnp.mean(rc * rc, axis=-1, keepdims=True)
    return rc * lax.rsqrt(var + LN_EPS) * g + b


def _postnorm_body(x_ref, y_ref, gate_ref, g_ref, b_ref, sc_ref, sh_ref, xo_ref, ho_ref, *, alpha, pack):
    r = alpha * x_ref[...] + (1.0 + gate_ref[...]) * y_ref[...].astype(F32)
    xn = _layer_norm_rows(r, g_ref[...], b_ref[...])
    xo_ref[...] = xn
    hn = xn * (1.0 + sc_ref[...]) + sh_ref[...]
    ho_ref[...] = _pack_bf16_pairs(hn) if pack else hn.astype(ho_ref.dtype)


def _postnorm(x, y, gate, g, b, nscale, nshift, *, alpha, pack=False, tm=256):
    s, d = x.shape
    row = pl.BlockSpec((tm, d), lambda i: (i, 0))
    vec = pl.BlockSpec((1, d), lambda i: (0, 0))
    hd, hdt = (d // 2, I32) if pack else (d, BF16)
    return pl.pallas_call(
        functools.partial(_postnorm_body, alpha=alpha, pack=pack),
        grid=(s // tm,),
        in_specs=[row, row, vec, vec, vec, vec, vec],
        out_specs=[row, pl.BlockSpec((tm, hd), lambda i: (i, 0))],
        out_shape=[jax.ShapeDtypeStruct((s, d), F32), jax.ShapeDtypeStruct((s, hd), hdt)],
        compiler_params=_cparams("arbitrary"),
        name="postnorm",
    )(x, y, gate.reshape(1, d), g.reshape(1, d), b.reshape(1, d), nscale.reshape(1, d), nshift.reshape(1, d))


def _proj_body(*refs, n_a, epilogue, tn, q_cols, q_scale):
    a_refs, w_ref, extra = refs[:n_a], refs[n_a], refs[n_a + 1:-2]
    o_ref, wb_s = refs[-2], refs[-1]

    @pl.when(pl.program_id(1) == 0)
    def _():
        wb_s[...] = w_ref[...].astype(BF16)

    acc, off = None, 0
    for a_ref in a_refs:
        kp = a_ref.shape[1]
        part = _dot(a_ref[...], wb_s[off:off + kp, :])
        acc = part if acc is None else acc + part
        off += kp
    if epilogue == "silu":
        acc = _silu(acc)
    elif epilogue == "forget":
        lb = extra[0][...]
        acc = lb + (1.0 - lb) * (1.0 / (1.0 + jnp.exp(-acc)))
    elif epilogue == "qscale":
        acc = acc * jnp.where(pl.program_id(0) * tn < q_cols, q_scale, 1.0)
    o_ref[...] = acc.astype(o_ref.dtype)


def _proj(a_parts, w, *, col_off, n_cols, out_dtype, epilogue="id", lb=None, q_cols=0, q_scale=1.0,
          tm=1024, tn=512, name="proj"):
    m = a_parts[0].shape[0]
    k = w.shape[0]
    joff = col_off // tn
    in_specs = [pl.BlockSpec((tm, a.shape[1]), lambda j, i: (i, 0)) for a in a_parts]
    in_specs.append(pl.BlockSpec((k, tn), lambda j, i: (0, j + joff)))
    args = list(a_parts) + [w]
    if epilogue == "forget":
        in_specs.append(pl.BlockSpec((1, tn), lambda j, i: (0, j)))
        args.append(lb.reshape(1, n_cols))
    return pl.pallas_call(
        functools.partial(_proj_body, n_a=len(a_parts), epilogue=epilogue, tn=tn, q_cols=q_cols, q_scale=q_scale),
        grid=(n_cols // tn, m // tm),
        in_specs=in_specs,
        out_specs=pl.BlockSpec((tm, tn), lambda j, i: (i, j)),
        out_shape=jax.ShapeDtypeStruct((m, n_cols), out_dtype),
        scratch_shapes=[pltpu.VMEM((k, tn), BF16)],
        compiler_params=_cparams("arbitrary", "arbitrary"),
        name=name,
    )(*args)


def _swiglu_in_body(a_ref, wa_ref, wb_ref, o_ref, wa_s, wb_s):
    @pl.when(pl.program_id(1) == 0)
    def _():
        wa_s[...] = wa_ref[...].astype(BF16)
        wb_s[...] = wb_ref[...].astype(BF16)

    a = a_ref[...]
    o_ref[...] = (_silu(_dot(a, wa_s[...])) * _dot(a, wb_s[...])).astype(o_ref.dtype)


def _swiglu_in(a, w, *, tm=1024, tn=256):
    m, k = a.shape
    half = w.shape[1] // 2
    hoff = half // tn
    return pl.pallas_call(
        _swiglu_in_body,
        grid=(half // tn, m // tm),
        in_specs=[pl.BlockSpec((tm, k), lambda j, i: (i, 0)),
                  pl.BlockSpec((k, tn), lambda j, i: (0, j)),
                  pl.BlockSpec((k, tn), lambda j, i: (0, j + hoff))],
        out_specs=pl.BlockSpec((tm, tn), lambda j, i: (i, j)),
        out_shape=jax.ShapeDtypeStruct((m, half), BF16),
        scratch_shapes=[pltpu.VMEM((k, tn), BF16), pltpu.VMEM((k, tn), BF16)],
        compiler_params=_cparams("arbitrary", "arbitrary"),
        name="swiglu_in",
    )(a, w, w)


def _mm_acc_body(a_ref, w_ref, o_ref, acc_ref, *, nk):
    kk = pl.program_id(2)
    part = _dot(a_ref[...], w_ref[...].astype(BF16))

    @pl.when(kk == 0)
    def _():
        acc_ref[...] = part

    @pl.when(kk > 0)
    def _():
        acc_ref[...] += part

    @pl.when(kk == nk - 1)
    def _():
        o_ref[...] = acc_ref[...].astype(o_ref.dtype)


def _mm_acc(a, w, *, tm=1024, tn=512, tk, out_dtype=F32):
    m, k = a.shape
    n = w.shape[1]
    nk = k // tk
    return pl.pallas_call(
        functools.partial(_mm_acc_body, nk=nk),
        grid=(m // tm, n // tn, nk),
        in_specs=[pl.BlockSpec((tm, tk), lambda i, j, kk: (i, kk)),
                  pl.BlockSpec((tk, tn), lambda i, j, kk: (kk, j))],
        out_specs=pl.BlockSpec((tm, tn), lambda i, j, kk: (i, j)),
        out_shape=jax.ShapeDtypeStruct((m, n), out_dtype),
        scratch_shapes=[pltpu.VMEM((tm, tn), F32)],
        compiler_params=_cparams("arbitrary", "arbitrary", "arbitrary"),
        name="mm_acc",
    )(a, w)


ATTN_ONES_ROWS = 16
ATTN_SLAB = 256


def _attn_body(qi_tab, kv_tab, slopes, q_ref, k_ref, vt_ref, pos_ref, lq1_ref, lk1_ref, lq2_ref, lk2_ref, g_ref,
               o_ref, q1_s, q2_s, m1_s, a1_s, m2_s, a2_s, jmi_s, *, tq, tk, lam_init):
    h = pl.program_id(0)
    t = pl.program_id(1)
    qi = qi_tab[t]
    kv = kv_tab[t]
    slope = slopes[h]
    n_sub = tq // tk
    dv = A_VDIM

    @pl.when(t == 0)
    def _():
        jmi_s[...] = lax.broadcasted_iota(I32, (tk, tq), 0) - lax.broadcasted_iota(I32, (tk, tq), 1)

    @pl.when(kv == 0)
    def _():
        q = q_ref[...]
        lane = lax.broadcasted_iota(I32, q.shape, 1)
        zero = jnp.zeros_like(q)
        sv = jnp.full(q.shape, slope, F32)
        s_hi = sv.astype(BF16).astype(F32)
        s_lo = sv - s_hi
        coef = jnp.where(lane == 0, s_hi * 16.0, jnp.where(lane == 1, s_lo * 16.0,
                         jnp.where(lane == 2, s_hi, jnp.where(lane == 3, s_lo, 0.0)))).astype(BF16)
        q1_s[...] = jnp.concatenate([jnp.where(lane < A_QKDIM, q, zero), coef], axis=1)
        q2_s[...] = jnp.concatenate([jnp.where(lane >= A_QKDIM, q, zero), coef], axis=1)
        for m_s, a_s in ((m1_s, a1_s), (m2_s, a2_s)):
            m_s[...] = jnp.full(m_s.shape, -jnp.inf, F32)
            a_s[...] = jnp.zeros(a_s.shape, F32)

    shift = kv * tk - qi * tq

    def step(key_off):
        k_aug = jnp.concatenate([k_ref[...], pos_ref[...]], axis=1)
        vt = vt_ref[...]
        c = slope * shift.astype(F32)
        chains = []
        for c0 in range(0, tq, ATTN_SLAB):
            if key_off is not None and c0 + ATTN_SLAB - 1 < key_off:
                continue
            masked = key_off is not None and c0 < key_off + tk - 1
            for q_s, m_s, a_s in ((q1_s, m1_s, a1_s), (q2_s, m2_s, a2_s)):
                chains.append((q_s, m_s, a_s, slice(c0, c0 + ATTN_SLAB), masked))
        scores = [_dot_t(k_aug, q_s[cols, :]) for q_s, _, _, cols, _ in chains]
        for s, (_, m_s, a_s, cols, masked) in zip(scores, chains):
            if masked:
                s = jnp.where(jmi_s[:, cols] <= -key_off, s, NEG_INF)
            m_prev = m_s[:, cols]
            m_new = jnp.maximum(m_prev, jnp.max(s, axis=0, keepdims=True) + c)
            p = jnp.exp2(s - (m_new - c))
            a_s[:, cols] = jnp.exp2(m_prev - m_new) * a_s[:, cols] + _dot(vt, p.astype(BF16))
            m_s[:, cols] = m_new

    @pl.when(kv < qi * n_sub)
    def _():
        step(None)

    for r in range(n_sub):
        @pl.when(kv == qi * n_sub + r)
        def _():
            step(r * tk)

    @pl.when(kv == (qi + 1) * n_sub - 1)
    def _():
        lam = (jnp.exp(jnp.sum(lq1_ref[...] * lk1_ref[...], axis=-1, keepdims=True))
               - jnp.exp(jnp.sum(lq2_ref[...] * lk2_ref[...], axis=-1, keepdims=True)) + lam_init)
        a1 = a1_s[...]
        a2 = a2_s[...]
        o = a1[:dv] / a1[dv:dv + 1] - lam * (a2[:dv] / a2[dv:dv + 1])
        ms = jnp.mean(o * o, axis=0, keepdims=True)
        on = o * lax.rsqrt(ms + RMS_EPS) * g_ref[...] * (1.0 - lam_init)
        o_ref[...] = on.T.astype(o_ref.dtype)


def _diff_attention(qkvu, v_t, lam_q1, lam_k1, lam_q2, lam_k2, subln_g, *, n_heads, lam_init, tq=2048, tk=512):
    s = qkvu.shape[0]
    n_sub = tq // tk
    pairs = [(i, j) for i in range(s // tq) for j in range((i + 1) * n_sub)]
    qi_tab = jnp.asarray([p[0] for p in pairs], I32)
    kv_tab = jnp.asarray([p[1] for p in pairs], I32)
    slopes = jnp.asarray([math.log2(math.e) * 2.0 ** (-8.0 * (i + 1) / n_heads) for i in range(n_heads)], F32)
    hd = A_VDIM
    hv = hd + ATTN_ONES_ROWS
    j = np.arange(tk)
    pos = np.zeros((tk, hd), np.float32)
    pos[:, 0] = pos[:, 1] = j // 16
    pos[:, 2] = pos[:, 3] = j % 16
    small = pl.BlockSpec((1, A_QKDIM), lambda h, t, qt, kt, sl: (0, 0))
    grid_spec = pltpu.PrefetchScalarGridSpec(
        num_scalar_prefetch=3,
        grid=(n_heads, len(pairs)),
        in_specs=[pl.BlockSpec((tq, hd), lambda h, t, qt, kt, sl: (qt[t], h)),
                  pl.BlockSpec((tk, hd), lambda h, t, qt, kt, sl: (kt[t], n_heads + h)),
                  pl.BlockSpec((hv, tk), lambda h, t, qt, kt, sl: (h, kt[t])),
                  pl.BlockSpec((tk, hd), lambda h, t, qt, kt, sl: (0, 0)),
                  small, small, small, small,
                  pl.BlockSpec((hd, 1), lambda h, t, qt, kt, sl: (0, 0))],
        out_specs=pl.BlockSpec((tq, hd), lambda h, t, qt, kt, sl: (qt[t], h)),
        scratch_shapes=[pltpu.VMEM((tq, 2 * hd), BF16), pltpu.VMEM((tq, 2 * hd), BF16),
                        pltpu.VMEM((1, tq), F32), pltpu.VMEM((hv, tq), F32),
                        pltpu.VMEM((1, tq), F32), pltpu.VMEM((hv, tq), F32),
                        pltpu.VMEM((tk, tq), I32)])
    return pl.pallas_call(
        functools.partial(_attn_body, tq=tq, tk=tk, lam_init=lam_init),
        grid_spec=grid_spec,
        out_shape=jax.ShapeDtypeStruct((s, n_heads * hd), BF16),
        compiler_params=_cparams("arbitrary", "arbitrary"),
        name="diff_attention",
    )(qi_tab, kv_tab, slopes, qkvu, qkvu, v_t, jnp.asarray(pos, BF16),
      lam_q1.reshape(1, -1), lam_k1.reshape(1, -1), lam_q2.reshape(1, -1), lam_k2.reshape(1, -1),
      subln_g.reshape(-1, 1))


def _attn_value_rows(v, n_heads):
    s = v.shape[0]
    vt = v.T.reshape(n_heads, A_VDIM, s)
    ones = jnp.ones((n_heads, ATTN_ONES_ROWS, s), v.dtype)
    return jnp.concatenate([vt, ones], axis=1).reshape(n_heads * (A_VDIM + ATTN_ONES_ROWS), s)


def _pool_body(ucur_ref, uprev_ref, w_ref, sc_ref, o_ref, *, t_blk):
    g = pl.program_id(0)
    i = pl.program_id(1)
    win = jnp.left_shift(2, g)
    row = lax.broadcasted_iota(I32, (t_blk, t_blk), 0)
    col = lax.broadcasted_iota(I32, (t_blk, t_blk), 1)
    d = row - col
    band = jnp.where(jnp.logical_and(d >= 0, d < win), 1.0, 0.0).astype(BF16)
    rowp = lax.broadcasted_iota(I32, (t_blk, POOL_HALO), 0)
    colp = lax.broadcasted_iota(I32, (t_blk, POOL_HALO), 1)
    dp = rowp + POOL_HALO - colp
    bandp = jnp.where(jnp.logical_and(dp < win, i > 0), 1.0, 0.0).astype(BF16)
    u = ucur_ref[...]
    usum = _dot(band, u) + _dot(bandp, uprev_ref[...])
    tpos = i * t_blk + lax.broadcasted_iota(I32, (t_blk, 1), 0)
    cnt = jnp.minimum(tpos + 1, win).astype(F32)
    dev = usum / cnt - u.astype(F32)
    y = _dot(dev.astype(BF16), w_ref[...]) * sc_ref[...]
    o_ref[...] = y.astype(o_ref.dtype)


def _multiscale_pool(qkvu, pool_w, pool_scale, *, col_off, t_blk=256):
    s = qkvu.shape[0]
    ng, gd, _ = pool_w.shape
    goff = col_off // gd
    hb = t_blk // POOL_HALO
    return pl.pallas_call(
        functools.partial(_pool_body, t_blk=t_blk),
        grid=(ng, s // t_blk),
        in_specs=[pl.BlockSpec((t_blk, gd), lambda g, i: (i, goff + g)),
                  pl.BlockSpec((POOL_HALO, gd), lambda g, i: (jnp.maximum(i * hb - 1, 0), goff + g)),
                  pl.BlockSpec((None, gd, gd), lambda g, i: (g, 0, 0)),
                  pl.BlockSpec((1, gd), lambda g, i: (0, g))],
        out_specs=pl.BlockSpec((t_blk, gd), lambda g, i: (i, g)),
        out_shape=jax.ShapeDtypeStruct((s, ng * gd), BF16),
        compiler_params=_cparams("arbitrary", "arbitrary"),
        name="multiscale_pool",
    )(qkvu, qkvu, pool_w, pool_scale.reshape(1, -1))


def _hgrn_tables():
    c = HGRN_CHUNK
    idx = np.arange(c)
    mats, masks = [], []
    h = c // 2
    while h >= 1:
        upper = (idx % (2 * h)) >= h
        e = idx - (idx % (2 * h)) + h - 1
        u = idx[None, :]
        pq = (upper[:, None] & (u > e[:, None]) & (u <= idx[:, None]))
        pk = ((~upper)[:, None] & (u > idx[:, None]) & (u <= e[:, None]))
        mats += [pq, pk]
        masks.append(upper[:, None] & (~upper)[None, :] & ((idx[:, None] // (2 * h)) == (idx[None, :] // (2 * h))))
        h //= 2
    masks.append(idx[:, None] == idx[None, :])
    mats.append(idx[None, :] <= idx[:, None])
    mats.append(idx[None, :] > idx[:, None])
    return (np.concatenate(mats, axis=0).astype(np.float32),
            np.stack(masks, axis=0).astype(np.float32))


def _hgrn_body(q_ref, f_ref, v_ref, gs_ref, p_ref, mask_ref, gn_ref, o_ref, st_ref, *, t_blk, n_grp):
    c = HGRN_CHUNK
    n_lvl = mask_ref.shape[0] - 1
    heads = range(n_grp)

    @pl.when(pl.program_id(1) == 0)
    def _():
        st_ref[...] = jnp.zeros(st_ref.shape, F32)

    def col(x, hg):
        return x[:, hg * C_KDIM:(hg + 1) * C_KDIM]

    def stack(parts):
        return jnp.concatenate(parts, axis=0)

    def block_diag(parts):
        zero = jnp.zeros_like(parts[0])
        return stack([jnp.concatenate([parts[hg] if j == hg else zero for j in heads], axis=1) for hg in heads])

    def chunk(ci, carry):
        rows = pl.ds(pl.multiple_of(ci * c, c), c)
        f = f_ref[rows, :]
        lf = jnp.log(f)
        kk = 1.0 - f
        hi = lf.astype(BF16)
        r1 = lf - hi.astype(F32)
        mid = r1.astype(BF16)
        lo = (r1 - mid.astype(F32)).astype(BF16)
        ed = jnp.exp(_dot(p_ref[...], stack([hi, mid, lo])))
        e = [col(ed, hg) for hg in heads]
        q = q_ref[rows, :].astype(F32)
        qh = [col(q, hg) for hg in heads]
        kh = [col(kk, hg) for hg in heads]
        attn = mask_ref[n_lvl] * _dot_t(stack([x.astype(BF16) for x in qh]), stack([x.astype(BF16) for x in kh]))
        for lv in range(n_lvl):
            qs = stack([(qh[hg] * e[hg][2 * lv * c:(2 * lv + 1) * c]).astype(BF16) for hg in heads])
            ks = stack([(kh[hg] * e[hg][(2 * lv + 1) * c:(2 * lv + 2) * c]).astype(BF16) for hg in heads])
            attn = attn + mask_ref[lv] * _dot_t(qs, ks)
        eb = [e[hg][2 * n_lvl * c:(2 * n_lvl + 1) * c] for hg in heads]
        el = [e[hg][(2 * n_lvl + 1) * c:(2 * n_lvl + 2) * c] for hg in heads]
        v = v_ref[rows, :]
        v_st = stack([col(v, hg) for hg in heads])
        st = st_ref[...]
        q_bd = block_diag([(qh[hg] * eb[hg]).astype(BF16) for hg in heads])
        o = _dot(attn.astype(BF16), v_st) + _dot_t(q_bd, st.astype(BF16))
        k_bd = block_diag([(kh[hg] * el[hg]).astype(BF16) for hg in heads])
        decay = jnp.concatenate([eb[hg][c - 1:c, :] for hg in heads], axis=1)
        st_ref[...] = st * decay + lax.dot_general(v_st, k_bd, (((0,), (0,)), ((), ())),
                                                   preferred_element_type=F32)
        ms = jnp.mean(o * o, axis=-1, keepdims=True)
        on = o * lax.rsqrt(ms + RMS_EPS) * gn_ref[...]
        gs = gs_ref[rows, :].astype(F32)
        for hg in heads:
            o_ref[rows, hg * C_VDIM:(hg + 1) * C_VDIM] = (on[hg * c:(hg + 1) * c] * col(gs, hg)).astype(o_ref.dtype)
        return carry

    lax.fori_loop(0, t_blk // c, chunk, 0)


def _hgrn2(qs, forget, v, gs, gnorm_g, *, t_blk=512, n_grp=4):
    s, d = qs.shape
    n_heads = d // C_KDIM
    p_np, mask_np = _hgrn_tables()
    p_mat = jnp.asarray(np.concatenate([p_np] * 3, axis=1), BF16)
    masks = jnp.asarray(np.stack([np.kron(np.eye(n_grp, dtype=np.float32), m) for m in mask_np]), F32)
    blk = lambda: pl.BlockSpec((t_blk, n_grp * C_KDIM), lambda h, i: (i, h))
    return pl.pallas_call(
        functools.partial(_hgrn_body, t_blk=t_blk, n_grp=n_grp),
        grid=(n_heads // n_grp, s // t_blk),
        in_specs=[blk(), blk(), blk(), blk(),
                  pl.BlockSpec(p_mat.shape, lambda h, i: (0, 0)),
                  pl.BlockSpec(masks.shape, lambda h, i: (0, 0, 0)),
                  pl.BlockSpec((1, C_VDIM), lambda h, i: (0, 0))],
        out_specs=blk(),
        out_shape=jax.ShapeDtypeStruct((s, d), BF16),
        scratch_shapes=[pltpu.VMEM((C_VDIM, n_grp * C_KDIM), F32)],
        compiler_params=_cparams("arbitrary", "arbitrary"),
        name="hgrn2",
    )(qs, forget, v, gs, p_mat, masks, gnorm_g.reshape(1, -1))


def _router_body(x_ref, sc_ref, sh_ref, whi_ref, wlo_ref, info_ref, cnt_ref, carry_s, *, tm):
    @pl.when(pl.program_id(0) == 0)
    def _():
        carry_s[...] = jnp.zeros(carry_s.shape, F32)

    hmod = x_ref[...] * (1.0 + sc_ref[...]) + sh_ref[...]
    hh = hmod.astype(BF16)
    hl = (hmod - hh.astype(F32)).astype(BF16)
    whi = whi_ref[...]
    logits = _dot(hh, whi) + _dot(hl, whi) + _dot(hh, wlo_ref[...])
    lane = lax.broadcasted_iota(I32, (tm, LANES), 1)
    logits = jnp.where(lane < N_EXPERTS, logits, -jnp.inf)
    m1 = jnp.max(logits, axis=-1, keepdims=True)
    e1 = jnp.min(jnp.where(logits == m1, lane, LANES), axis=-1, keepdims=True)
    rest = jnp.where(lane == e1, -jnp.inf, logits)
    m2 = jnp.max(rest, axis=-1, keepdims=True)
    e2 = jnp.min(jnp.where(rest == m2, lane, LANES), axis=-1, keepdims=True)
    ex = jnp.exp(m2 - m1)
    g1 = 1.0 / (1.0 + ex)
    g2 = ex / (1.0 + ex)
    onehot = jnp.where(jnp.logical_or(lane == e1, lane == e2), 1.0, 0.0)
    row = lax.broadcasted_iota(I32, (tm, tm), 0)
    col = lax.broadcasted_iota(I32, (tm, tm), 1)
    before = jnp.where(row > col, 1.0, 0.0).astype(BF16)
    cum = _dot(before, onehot.astype(BF16)) + carry_s[...]
    r1 = jnp.sum(jnp.where(lane == e1, cum, 0.0), axis=-1, keepdims=True)
    r2 = jnp.sum(jnp.where(lane == e2, cum, 0.0), axis=-1, keepdims=True)
    carry_s[...] = carry_s[...] + jnp.sum(onehot, axis=0, keepdims=True)
    cnt_ref[...] = carry_s[...]
    info = jnp.where(lane == 0, e1.astype(F32), 0.0)
    info = jnp.where(lane == 1, e2.astype(F32), info)
    info = jnp.where(lane == 2, g1, info)
    info = jnp.where(lane == 3, g2, info)
    info = jnp.where(lane == 4, r1, info)
    info = jnp.where(lane == 5, r2, info)
    info_ref[...] = info


def _router(x, scale, shift, router_w, *, tm=256):
    s, d = x.shape
    wpad = jnp.zeros((d, LANES), F32).at[:, :N_EXPERTS].set(router_w)
    whi = wpad.astype(BF16)
    wlo = (wpad - whi.astype(F32)).astype(BF16)
    vec = pl.BlockSpec((1, d), lambda i: (0, 0))
    wspec = pl.BlockSpec((d, LANES), lambda i: (0, 0))
    return pl.pallas_call(
        functools.partial(_router_body, tm=tm),
        grid=(s // tm,),
        in_specs=[pl.BlockSpec((tm, d), lambda i: (i, 0)), vec, vec, wspec, wspec],
        out_specs=[pl.BlockSpec((tm, LANES), lambda i: (i, 0)),
                   pl.BlockSpec((1, LANES), lambda i: (0, 0))],
        out_shape=[jax.ShapeDtypeStruct((s, LANES), F32), jax.ShapeDtypeStruct((1, LANES), F32)],
        scratch_shapes=[pltpu.VMEM((1, LANES), F32)],
        compiler_params=_cparams("arbitrary"),
        name="router",
    )(x, scale.reshape(1, d), shift.reshape(1, d), whi, wlo)


def _dispatch_copies(s1_ref, s2_ref, h_ref, out_ref, sem, base, r):
    t = base + r
    src = h_ref.at[pl.ds(r, 1)]
    return (pltpu.make_async_copy(src, out_ref.at[pl.ds(s1_ref[t], 1)], sem.at[0]),
            pltpu.make_async_copy(src, out_ref.at[pl.ds(s2_ref[t], 1)], sem.at[1]))


def _dispatch_body(s1_ref, s2_ref, h_ref, init_ref, out_ref, sem, *, tb):
    del init_ref
    base = pl.program_id(0) * tb

    def issue(r, carry):
        for cp in _dispatch_copies(s1_ref, s2_ref, h_ref, out_ref, sem, base, r):
            cp.start()
        return carry

    def drain(r, carry):
        for cp in _dispatch_copies(s1_ref, s2_ref, h_ref, out_ref, sem, base, r):
            cp.wait()
        return carry

    lax.fori_loop(0, tb, issue, 0, unroll=DMA_ISSUE_UNROLL)
    lax.fori_loop(0, tb, drain, 0, unroll=DMA_ISSUE_UNROLL)


def _dispatch(h_words, slot1, slot2, n_rows, *, tb=256):
    s, dw = h_words.shape
    init = jnp.zeros((n_rows, dw), h_words.dtype)
    grid_spec = pltpu.PrefetchScalarGridSpec(
        num_scalar_prefetch=2,
        grid=(s // tb,),
        in_specs=[pl.BlockSpec((tb, dw), lambda i, a, b: (i, 0)),
                  pl.BlockSpec(memory_space=pl.ANY)],
        out_specs=pl.BlockSpec(memory_space=pl.ANY),
        scratch_shapes=[pltpu.SemaphoreType.DMA((2,))])
    return pl.pallas_call(
        functools.partial(_dispatch_body, tb=tb),
        grid_spec=grid_spec,
        out_shape=jax.ShapeDtypeStruct((n_rows, dw), h_words.dtype),
        input_output_aliases={3: 0},
        compiler_params=_cparams("arbitrary"),
        name="moe_dispatch",
    )(slot1, slot2, h_words, init)


def _expert_rows_body(off_ref, cnt_ref, *refs, swiglu, tr, tn, n_rows):
    if swiglu:
        wa_ref, wb_ref, x_hbm, o_hbm, wa_s, wb_s, xbuf, obuf, xsem, osem = refs
        wa_s[...] = wa_ref[...].astype(BF16)
        wb_s[...] = wb_ref[...].astype(BF16)
    else:
        w_ref, x_hbm, o_hbm, w_s, xbuf, obuf, xsem, osem = refs
        w_s[...] = w_ref[...].astype(BF16)
    j = pl.program_id(0)
    e = pl.program_id(1)
    n = (cnt_ref[e] + (tr - 1)) // tr
    col0 = pl.multiple_of(j * tn, tn)

    def row(t):
        return pl.multiple_of(off_ref[e] + t * tr, tr)

    def x_copy(row0, slot):
        return pltpu.make_async_copy(x_hbm.at[pl.ds(row0, tr)], xbuf.at[slot], xsem.at[slot])

    def o_copy(row0, slot):
        return pltpu.make_async_copy(obuf.at[slot], o_hbm.at[pl.ds(row0, tr), pl.ds(col0, tn)], osem.at[slot])

    @pl.when(n > 0)
    def _():
        x_copy(row(0), 0).start()

    def tile(t, carry):
        slot = lax.rem(t, 2)

        @pl.when(t + 1 < n)
        def _():
            x_copy(row(t + 1), 1 - slot).start()

        x_copy(row(t), slot).wait()

        @pl.when(t >= 2)
        def _():
            o_copy(row(t - 2), slot).wait()

        if swiglu:
            x = _unpack_bf16_pairs(xbuf[slot])
            res = _silu(_dot(x, wa_s[...])) * _dot(x, wb_s[...])
        else:
            res = _dot(xbuf[slot], w_s[...])
        obuf[slot] = res.astype(obuf.dtype)
        o_copy(row(t), slot).start()
        return carry

    lax.fori_loop(0, n, tile, 0)

    @pl.when(n >= 2)
    def _():
        o_copy(row(n - 2), lax.rem(n, 2)).wait()

    @pl.when(n >= 1)
    def _():
        o_copy(row(n - 1), lax.rem(n - 1, 2)).wait()

    @pl.when(e == pl.num_programs(1) - 1)
    def _():
        obuf[0] = jnp.zeros(obuf.shape[1:], obuf.dtype)
        end = off_ref[e] + n * tr

        def fill(t, carry):
            cp = o_copy(pl.multiple_of(end + t * tr, tr), 0)
            cp.start()
            cp.wait()
            return carry

        lax.fori_loop(0, (n_rows - end) // tr, fill, 0)


def _expert_in(xs_words, w_in, offs, counts, *, tr, tn=512):
    p, kw = xs_words.shape
    k = 2 * kw
    n_exp = w_in.shape[0]
    half = w_in.shape[2] // 2
    hoff = half // tn
    grid_spec = pltpu.PrefetchScalarGridSpec(
        num_scalar_prefetch=2,
        grid=(half // tn, n_exp),
        in_specs=[pl.BlockSpec((None, k, tn), lambda j, e, off, cnt: (e, 0, j)),
                  pl.BlockSpec((None, k, tn), lambda j, e, off, cnt: (e, 0, j + hoff)),
                  pl.BlockSpec(memory_space=pl.ANY)],
        out_specs=pl.BlockSpec(memory_space=pl.ANY),
        scratch_shapes=[pltpu.VMEM((k, tn), BF16), pltpu.VMEM((k, tn), BF16),
                        pltpu.VMEM((2, tr, kw), xs_words.dtype), pltpu.VMEM((2, tr, tn), BF16),
                        pltpu.SemaphoreType.DMA((2,)), pltpu.SemaphoreType.DMA((2,))])
    return pl.pallas_call(
        functools.partial(_expert_rows_body, swiglu=True, tr=tr, tn=tn, n_rows=p),
        grid_spec=grid_spec,
        out_shape=jax.ShapeDtypeStruct((p, half), BF16),
        compiler_params=_cparams("arbitrary", "arbitrary"),
        name="expert_in",
    )(offs, counts, w_in, w_in, xs_words)


def _expert_out(gs, w_out, offs, counts, *, tr, tn=1024):
    p, k = gs.shape
    n_exp = w_out.shape[0]
    n = w_out.shape[2]
    grid_spec = pltpu.PrefetchScalarGridSpec(
        num_scalar_prefetch=2,
        grid=(n // tn, n_exp),
        in_specs=[pl.BlockSpec((None, k, tn), lambda j, e, off, cnt: (e, 0, j)),
                  pl.BlockSpec(memory_space=pl.ANY)],
        out_specs=pl.BlockSpec(memory_space=pl.ANY),
        scratch_shapes=[pltpu.VMEM((k, tn), BF16),
                        pltpu.VMEM((2, tr, k), gs.dtype), pltpu.VMEM((2, tr, tn), F32),
                        pltpu.SemaphoreType.DMA((2,)), pltpu.SemaphoreType.DMA((2,))])
    return pl.pallas_call(
        functools.partial(_expert_rows_body, swiglu=False, tr=tr, tn=tn, n_rows=p),
        grid_spec=grid_spec,
        out_shape=jax.ShapeDtypeStruct((p, n), F32),
        compiler_params=_cparams("arbitrary", "arbitrary"),
        name="expert_out",
    )(offs, counts, w_out, gs)


def _combine_copies(s1_ref, s2_ref, ys_ref, buf, sem, blk, slot, r, tb):
    t = blk * tb + r
    return (pltpu.make_async_copy(ys_ref.at[pl.ds(s1_ref[t], 1)], buf.at[slot, 0, pl.ds(r, 1)], sem.at[slot, 0]),
            pltpu.make_async_copy(ys_ref.at[pl.ds(s2_ref[t], 1)], buf.at[slot, 1, pl.ds(r, 1)], sem.at[slot, 1]))


def _combine_body(s1_ref, s2_ref, x_ref, info_ref, ys_ref, gate_ref, g_ref, b_ref, o_ref, buf, sem,
                  *, tb, alpha):
    i = pl.program_id(0)
    slot = lax.rem(i, 2)

    def gather(blk, slt):
        def issue(r, carry):
            for cp in _combine_copies(s1_ref, s2_ref, ys_ref, buf, sem, blk, slt, r, tb):
                cp.start()
            return carry
        lax.fori_loop(0, tb, issue, 0, unroll=DMA_ISSUE_UNROLL)

    @pl.when(i == 0)
    def _():
        gather(i, slot)

    @pl.when(i + 1 < pl.num_programs(0))
    def _():
        gather(i + 1, 1 - slot)

    def drain(r, carry):
        for cp in _combine_copies(s1_ref, s2_ref, ys_ref, buf, sem, i, slot, r, tb):
            cp.wait()
        return carry

    lax.fori_loop(0, tb, drain, 0, unroll=DMA_ISSUE_UNROLL)
    info = info_ref[...]
    y = info[:, 2:3] * buf[slot, 0] + info[:, 3:4] * buf[slot, 1]
    r = alpha * x_ref[...] + (1.0 + gate_ref[...]) * y
    o_ref[...] = _layer_norm_rows(r, g_ref[...], b_ref[...])


def _combine_postnorm(x, info, ys, slot1, slot2, gate, g, b, *, alpha, tb=128):
    s, d = x.shape
    vec = pl.BlockSpec((1, d), lambda i, a, c: (0, 0))
    grid_spec = pltpu.PrefetchScalarGridSpec(
        num_scalar_prefetch=2,
        grid=(s // tb,),
        in_specs=[pl.BlockSpec((tb, d), lambda i, a, c: (i, 0)),
                  pl.BlockSpec((tb, LANES), lambda i, a, c: (i, 0)),
                  pl.BlockSpec(memory_space=pl.ANY),
                  vec, vec, vec],
        out_specs=pl.BlockSpec((tb, d), lambda i, a, c: (i, 0)),
        scratch_shapes=[pltpu.VMEM((2, 2, tb, d), F32), pltpu.SemaphoreType.DMA((2, 2))])
    return pl.pallas_call(
        functools.partial(_combine_body, tb=tb, alpha=alpha),
        grid_spec=grid_spec,
        out_shape=jax.ShapeDtypeStruct((s, d), F32),
        compiler_params=_cparams("arbitrary"),
        name="moe_combine_postnorm",
    )(slot1, slot2, x, info, ys, gate.reshape(1, d), g.reshape(1, d), b.reshape(1, d))


def _moe_plan(info, counts, *, tr):
    e1 = info[:, 0].astype(I32)
    e2 = info[:, 1].astype(I32)
    r1 = info[:, 4].astype(I32)
    r2 = info[:, 5].astype(I32)
    cnt = counts[0, :N_EXPERTS].astype(I32)
    padded = (cnt + tr - 1) // tr * tr
    row_end = jnp.cumsum(padded)
    row_off = row_end - padded
    slot1 = row_off[e1] + r1
    slot2 = row_off[e2] + r2
    offs = jnp.concatenate([row_off, row_end[-1:]]).astype(I32)
    return slot1, slot2, offs, cnt


def kernel(x, c, ada_w, ada_b, ln_g, ln_b, even_w_in, lam_q1, lam_k1, lam_q2, lam_k2, subln_g, pool_w,
           pool_scale, even_w_out, ffn_w_in, ffn_w_out, odd_w_in, lb_raw, gnorm_g, odd_w_out, router_w,
           exp_w_in, exp_w_out):
    _, seq, d = x.shape
    depth = ada_w.shape[0]
    alpha = (2 * depth) ** 0.25
    a_width = d // 2
    a_heads = a_width // A_VDIM
    x2 = x.reshape(seq, d)

    lb_all = jnp.cumsum(jax.nn.softmax(lb_raw.astype(F32), axis=0), axis=0)
    lb_all = lb_all - lb_all[0]
    mod = _ada_mod(c, ada_w, ada_b)

    def mod_parts(l):
        return [mod[l, i * d:(i + 1) * d] for i in range(6)]

    sh1, sc1, g1, sh2, sc2, g2 = mod_parts(0)
    h = _modulate(x2, sc1, sh1)
    qkvu = _proj([h], even_w_in[0], col_off=0, n_cols=3 * a_width + (d - a_width), out_dtype=BF16,
                 epilogue="qscale", q_cols=a_width, q_scale=A_QKDIM ** -0.5 * math.log2(math.e),
                 name="even_in_proj")
    lam_init = 0.8 - 0.6 * math.exp(-0.3 * 0)
    v_t = _attn_value_rows(qkvu[:, 2 * a_width:3 * a_width], a_heads)
    o_a = _diff_attention(qkvu, v_t, lam_q1[0], lam_k1[0], lam_q2[0], lam_k2[0], subln_g[0],
                          n_heads=a_heads, lam_init=lam_init)
    o_b = _multiscale_pool(qkvu, pool_w[0].astype(BF16), pool_scale[0], col_off=3 * a_width)
    y = _proj([o_a, o_b], even_w_out[0], col_off=0, n_cols=d, out_dtype=BF16, name="even_out_proj")
    x2, h = _postnorm(x2, y, g1, ln_g[0, 0], ln_b[0, 0], sc2, sh2, alpha=alpha)

    ff = ffn_w_out.shape[1]
    gact = _swiglu_in(h, ffn_w_in[0])
    y = _mm_acc(gact, ffn_w_out[0], tk=ff // 2, out_dtype=BF16)
    sh1, sc1, g1n, sh2n, sc2n, g2n = mod_parts(1)
    x2, h = _postnorm(x2, y, g2, ln_g[0, 1], ln_b[0, 1], sc1, sh1, alpha=alpha)

    w_odd = odd_w_in[0]
    qs = _proj([h], w_odd, col_off=0, n_cols=d, out_dtype=BF16, epilogue="silu", name="odd_in_q")
    fg = _proj([h], w_odd, col_off=d, n_cols=d, out_dtype=F32, epilogue="forget", lb=lb_all[1], name="odd_in_f")
    vi = _proj([h], w_odd, col_off=2 * d, n_cols=d, out_dtype=BF16, name="odd_in_i")
    gs = _proj([h], w_odd, col_off=3 * d, n_cols=d, out_dtype=BF16, epilogue="silu", name="odd_in_g")
    o_c = _hgrn2(qs, fg, vi, gs, gnorm_g[0])
    y = _proj([o_c], odd_w_out[0], col_off=0, n_cols=d, out_dtype=BF16, name="odd_out_proj")
    x2, h_words = _postnorm(x2, y, g1n, ln_g[1, 0], ln_b[1, 0], sc2n, sh2n, alpha=alpha, pack=True)

    tr = 512
    n_rows = 2 * seq + N_EXPERTS * tr
    info, counts = _router(x2, sc2n, sh2n, router_w[0])
    slot1, slot2, offs, cnt = _moe_plan(info, counts, tr=tr)
    xs_words = _dispatch(h_words, slot1, slot2, n_rows)
    gexp = _expert_in(xs_words, exp_w_in[0], offs, cnt, tr=tr)
    ys = _expert_out(gexp, exp_w_out[0], offs, cnt, tr=tr)
    out = _combine_postnorm(x2, info, ys, slot1, slot2, g2n, ln_g[1, 1], ln_b[1, 1], alpha=alpha)
    return out.reshape(x.shape)
```

```python
import functools
import math

import numpy as np
import jax
import jax.numpy as jnp
from jax import lax
from jax.experimental import pallas as pl
from jax.experimental.pallas import tpu as pltpu

F32 = jnp.float32
BF16 = jnp.bfloat16
I32 = jnp.int32

A_VDIM = 128
A_QKDIM = 64
POOL_WINDOWS = (2, 4, 8, 16)
POOL_HALO = 16
C_KDIM = 128
C_VDIM = 128
HGRN_CHUNK = 64
N_EXPERTS = 8
LN_EPS = 1e-5
RMS_EPS = 1e-6
NEG_INF = -1e30

LANES = 128
VMEM_LIMIT_BYTES = 56 * 1024 * 1024
DMA_ISSUE_UNROLL = 8
ROW_TILE_DMA_PRIORITY = 1


def _cparams(*sem):
    return pltpu.CompilerParams(dimension_semantics=sem, vmem_limit_bytes=VMEM_LIMIT_BYTES)


def _silu(x):
    return x * (1.0 / (1.0 + jnp.exp(-x)))


def _dot(a, b):
    return jnp.dot(a, b, preferred_element_type=F32)


def _dot_t(a, b):
    return lax.dot_general(a, b, (((1,), (1,)), ((), ())), preferred_element_type=F32)


def _pack_bf16_pairs(h):
    half = h.shape[1] // 2
    bits = pltpu.bitcast(h.astype(BF16).astype(F32), I32)
    return jnp.bitwise_or(bits[:, :half], lax.shift_right_logical(bits[:, half:], 16))


def _unpack_bf16_pairs(w):
    hi = pltpu.bitcast(jnp.bitwise_and(w, -65536), F32).astype(BF16)
    lo = pltpu.bitcast(lax.shift_left(w, 16), F32).astype(BF16)
    return jnp.concatenate([hi, lo], axis=1)


def _ada_body(c_ref, w_ref, b_ref, o_ref):
    ca = _silu(c_ref[...]).astype(BF16)
    o_ref[...] = _dot(ca, w_ref[...].astype(BF16)) + b_ref[...]


def _ada_mod(c, ada_w, ada_b, *, tn=1024):
    depth, d, n = ada_w.shape
    c8 = jnp.broadcast_to(c.astype(F32), (8, d))
    out = pl.pallas_call(
        _ada_body,
        grid=(depth, n // tn),
        in_specs=[pl.BlockSpec((8, d), lambda l, j: (0, 0)),
                  pl.BlockSpec((None, d, tn), lambda l, j: (l, 0, j)),
                  pl.BlockSpec((None, 1, tn), lambda l, j: (l, 0, j))],
        out_specs=pl.BlockSpec((None, 8, tn), lambda l, j: (l, 0, j)),
        out_shape=jax.ShapeDtypeStruct((depth, 8, n), F32),
        compiler_params=_cparams("arbitrary", "arbitrary"),
        name="ada_mod",
    )(c8, ada_w, ada_b.reshape(depth, 1, n))
    return out[:, 0, :]


def _modulate_body(x_ref, sc_ref, sh_ref, o_ref):
    o_ref[...] = (x_ref[...] * (1.0 + sc_ref[...]) + sh_ref[...]).astype(o_ref.dtype)


def _modulate(x, scale, shift, *, tm=512):
    s, d = x.shape
    vec = pl.BlockSpec((1, d), lambda i: (0, 0))
    return pl.pallas_call(
        _modulate_body,
        grid=(s // tm,),
        in_specs=[pl.BlockSpec((tm, d), lambda i: (i, 0)), vec, vec],
        out_specs=pl.BlockSpec((tm, d), lambda i: (i, 0)),
        out_shape=jax.ShapeDtypeStruct((s, d), BF16),
        compiler_params=_cparams("arbitrary"),
        name="modulate",
    )(x, scale.reshape(1, d), shift.reshape(1, d))


def _layer_norm_rows(r, g, b):
    mu = jnp.mean(r, axis=-1, keepdims=True)
    rc = r - mu
    var = jnp.mean(rc * rc, axis=-1, keepdims=True)
    return rc * lax.rsqrt(var + LN_EPS) * g + b


def _postnorm_body(x_ref, y_ref, gate_ref, g_ref, b_ref, sc_ref, sh_ref, xo_ref, ho_ref, *, alpha, pack):
    r = alpha * x_ref[...] + (1.0 + gate_ref[...]) * y_ref[...].astype(F32)
    xn = _layer_norm_rows(r, g_ref[...], b_ref[...])
    xo_ref[...] = xn
    hn = xn * (1.0 + sc_ref[...]) + sh_ref[...]
    ho_ref[...] = _pack_bf16_pairs(hn) if pack else hn.astype(ho_ref.dtype)


def _postnorm(x, y, gate, g, b, nscale, nshift, *, alpha, pack=False, tm=256):
    s, d = x.shape
    row = pl.BlockSpec((tm, d), lambda i: (i, 0))
    vec = pl.BlockSpec((1, d), lambda i: (0, 0))
    hd, hdt = (d // 2, I32) if pack else (d, BF16)
    return pl.pallas_call(
        functools.partial(_postnorm_body, alpha=alpha, pack=pack),
        grid=(s // tm,),
        in_specs=[row, row, vec, vec, vec, vec, vec],
        out_specs=[row, pl.BlockSpec((tm, hd), lambda i: (i, 0))],
        out_shape=[jax.ShapeDtypeStruct((s, d), F32), jax.ShapeDtypeStruct((s, hd), hdt)],
        compiler_params=_cparams("arbitrary"),
        name="postnorm",
    )(x, y, gate.reshape(1, d), g.reshape(1, d), b.reshape(1, d), nscale.reshape(1, d), nshift.reshape(1, d))


def _proj_body(*refs, n_a, epilogue, tn, q_cols, q_scale):
    a_refs, w_ref, extra = refs[:n_a], refs[n_a], refs[n_a + 1:-2]
    o_ref, wb_s = refs[-2], refs[-1]

    @pl.when(pl.program_id(1) == 0)
    def _():
        wb_s[...] = w_ref[...].astype(BF16)

    acc, off = None, 0
    for a_ref in a_refs:
        kp = a_ref.shape[1]
        part = _dot(a_ref[...], wb_s[off:off + kp, :])
        acc = part if acc is None else acc + part
        off += kp
    if epilogue == "silu":
        acc = _silu(acc)
    elif epilogue == "forget":
        lb = extra[0][...]
        acc = lb + (1.0 - lb) * (1.0 / (1.0 + jnp.exp(-acc)))
    elif epilogue == "qscale":
        acc = acc * jnp.where(pl.program_id(0) * tn < q_cols, q_scale, 1.0)
    o_ref[...] = acc.astype(o_ref.dtype)


def _proj(a_parts, w, *, col_off, n_cols, out_dtype, epilogue="id", lb=None, q_cols=0, q_scale=1.0,
          tm=1024, tn=512, name="proj"):
    m = a_parts[0].shape[0]
    k = w.shape[0]
    joff = col_off // tn
    in_specs = [pl.BlockSpec((tm, a.shape[1]), lambda j, i: (i, 0)) for a in a_parts]
    in_specs.append(pl.BlockSpec((k, tn), lambda j, i: (0, j + joff)))
    args = list(a_parts) + [w]
    if epilogue == "forget":
        in_specs.append(pl.BlockSpec((1, tn), lambda j, i: (0, j)))
        args.append(lb.reshape(1, n_cols))
    return pl.pallas_call(
        functools.partial(_proj_body, n_a=len(a_parts), epilogue=epilogue, tn=tn, q_cols=q_cols, q_scale=q_scale),
        grid=(n_cols // tn, m // tm),
        in_specs=in_specs,
        out_specs=pl.BlockSpec((tm, tn), lambda j, i: (i, j)),
        out_shape=jax.ShapeDtypeStruct((m, n_cols), out_dtype),
        scratch_shapes=[pltpu.VMEM((k, tn), BF16)],
        compiler_params=_cparams("arbitrary", "arbitrary"),
        name=name,
    )(*args)


def _swiglu_in_body(a_ref, wa_ref, wb_ref, o_ref, wa_s, wb_s):
    @pl.when(pl.program_id(1) == 0)
    def _():
        wa_s[...] = wa_ref[...].astype(BF16)
        wb_s[...] = wb_ref[...].astype(BF16)

    a = a_ref[...]
    o_ref[...] = (_silu(_dot(a, wa_s[...])) * _dot(a, wb_s[...])).astype(o_ref.dtype)


def _swiglu_in(a, w, *, tm=1024, tn=256):
    m, k = a.shape
    half = w.shape[1] // 2
    hoff = half // tn
    return pl.pallas_call(
        _swiglu_in_body,
        grid=(half // tn, m // tm),
        in_specs=[pl.BlockSpec((tm, k), lambda j, i: (i, 0)),
                  pl.BlockSpec((k, tn), lambda j, i: (0, j)),
                  pl.BlockSpec((k, tn), lambda j, i: (0, j + hoff))],
        out_specs=pl.BlockSpec((tm, tn), lambda j, i: (i, j)),
        out_shape=jax.ShapeDtypeStruct((m, half), BF16),
        scratch_shapes=[pltpu.VMEM((k, tn), BF16), pltpu.VMEM((k, tn), BF16)],
        compiler_params=_cparams("arbitrary", "arbitrary"),
        name="swiglu_in",
    )(a, w, w)


def _mm_acc_body(a_ref, w_ref, o_ref, acc_ref, *, nk):
    kk = pl.program_id(2)
    part = _dot(a_ref[...], w_ref[...].astype(BF16))

    @pl.when(kk == 0)
    def _():
        acc_ref[...] = part

    @pl.when(kk > 0)
    def _():
        acc_ref[...] += part

    @pl.when(kk == nk - 1)
    def _():
        o_ref[...] = acc_ref[...].astype(o_ref.dtype)


def _mm_acc(a, w, *, tm=1024, tn=512, tk, out_dtype=F32):
    m, k = a.shape
    n = w.shape[1]
    nk = k // tk
    return pl.pallas_call(
        functools.partial(_mm_acc_body, nk=nk),
        grid=(m // tm, n // tn, nk),
        in_specs=[pl.BlockSpec((tm, tk), lambda i, j, kk: (i, kk)),
                  pl.BlockSpec((tk, tn), lambda i, j, kk: (kk, j))],
        out_specs=pl.BlockSpec((tm, tn), lambda i, j, kk: (i, j)),
        out_shape=jax.ShapeDtypeStruct((m, n), out_dtype),
        scratch_shapes=[pltpu.VMEM((tm, tn), F32)],
        compiler_params=_cparams("arbitrary", "arbitrary", "arbitrary"),
        name="mm_acc",
    )(a, w)


ATTN_ONES_ROWS = 16
ATTN_SLAB = 256


def _attn_body(qi_tab, kv_tab, slopes, q_ref, k_ref, vt_ref, pos_ref, lq1_ref, lk1_ref, lq2_ref, lk2_ref, g_ref,
               o_ref, q1_s, q2_s, m1_s, a1_s, m2_s, a2_s, jmi_s, *, tq, tk, lam_init):
    h = pl.program_id(0)
    t = pl.program_id(1)
    qi = qi_tab[t]
    kv = kv_tab[t]
    slope = slopes[h]
    n_sub = tq // tk
    dv = A_VDIM

    @pl.when(t == 0)
    def _():
        jmi_s[...] = lax.broadcasted_iota(I32, (tk, tq), 0) - lax.broadcasted_iota(I32, (tk, tq), 1)

    @pl.when(kv == 0)
    def _():
        q = q_ref[...]
        lane = lax.broadcasted_iota(I32, q.shape, 1)
        zero = jnp.zeros_like(q)
        sv = jnp.full(q.shape, slope, F32)
        s_hi = sv.astype(BF16).astype(F32)
        s_lo = sv - s_hi
        coef = jnp.where(lane == 0, s_hi * 16.0, jnp.where(lane == 1, s_lo * 16.0,
                         jnp.where(lane == 2, s_hi, jnp.where(lane == 3, s_lo, 0.0)))).astype(BF16)
        q1_s[...] = jnp.concatenate([jnp.where(lane < A_QKDIM, q, zero), coef], axis=1)
        q2_s[...] = jnp.concatenate([jnp.where(lane >= A_QKDIM, q, zero), coef], axis=1)
        for m_s, a_s in ((m1_s, a1_s), (m2_s, a2_s)):
            m_s[...] = jnp.full(m_s.shape, -jnp.inf, F32)
            a_s[...] = jnp.zeros(a_s.shape, F32)

    shift = kv * tk - qi * tq

    def step(key_off):
        k_aug = jnp.concatenate([k_ref[...], pos_ref[...]], axis=1)
        vt = vt_ref[...]
        c = slope * shift.astype(F32)
        chains = []
        for c0 in range(0, tq, ATTN_SLAB):
            if key_off is not None and c0 + ATTN_SLAB - 1 < key_off:
                continue
            masked = key_off is not None and c0 < key_off + tk - 1
            for q_s, m_s, a_s in ((q1_s, m1_s, a1_s), (q2_s, m2_s, a2_s)):
                chains.append((q_s, m_s, a_s, slice(c0, c0 + ATTN_SLAB), masked))
        scores = [_dot_t(k_aug, q_s[cols, :]) for q_s, _, _, cols, _ in chains]
        for s, (_, m_s, a_s, cols, masked) in zip(scores, chains):
            if masked:
                s = jnp.where(jmi_s[:, cols] <= -key_off, s, NEG_INF)
            m_prev = m_s[:, cols]
            m_new = jnp.maximum(m_prev, jnp.max(s, axis=0, keepdims=True) + c)
            p = jnp.exp2(s - (m_new - c))
            a_s[:, cols] = jnp.exp2(m_prev - m_new) * a_s[:, cols] + _dot(vt, p.astype(BF16))
            m_s[:, cols] = m_new

    @pl.when(kv < qi * n_sub)
    def _():
        step(None)

    for r in range(n_sub):
        @pl.when(kv == qi * n_sub + r)
        def _():
            step(r * tk)

    @pl.when(kv == (qi + 1) * n_sub - 1)
    def _():
        lam = (jnp.exp(jnp.sum(lq1_ref[...] * lk1_ref[...], axis=-1, keepdims=True))
               - jnp.exp(jnp.sum(lq2_ref[...] * lk2_ref[...], axis=-1, keepdims=True)) + lam_init)
        a1 = a1_s[...]
        a2 = a2_s[...]
        o = a1[:dv] / a1[dv:dv + 1] - lam * (a2[:dv] / a2[dv:dv + 1])
        ms = jnp.mean(o * o, axis=0, keepdims=True)
        on = o * lax.rsqrt(ms + RMS_EPS) * g_ref[...] * (1.0 - lam_init)
        o_ref[...] = on.T.astype(o_ref.dtype)


def _diff_attention(qkvu, v_t, lam_q1, lam_k1, lam_q2, lam_k2, subln_g, *, n_heads, lam_init, tq=2048, tk=512):
    s = qkvu.shape[0]
    n_sub = tq // tk
    pairs = [(i, j) for i in range(s // tq) for j in range((i + 1) * n_sub)]
    qi_tab = jnp.asarray([p[0] for p in pairs], I32)
    kv_tab = jnp.asarray([p[1] for p in pairs], I32)
    slopes = jnp.asarray([math.log2(math.e) * 2.0 ** (-8.0 * (i + 1) / n_heads) for i in range(n_heads)], F32)
    hd = A_VDIM
    hv = hd + ATTN_ONES_ROWS
    j = np.arange(tk)
    pos = np.zeros((tk, hd), np.float32)
    pos[:, 0] = pos[:, 1] = j // 16
    pos[:, 2] = pos[:, 3] = j % 16
    small = pl.BlockSpec((1, A_QKDIM), lambda h, t, qt, kt, sl: (0, 0))
    grid_spec = pltpu.PrefetchScalarGridSpec(
        num_scalar_prefetch=3,
        grid=(n_heads, len(pairs)),
        in_specs=[pl.BlockSpec((tq, hd), lambda h, t, qt, kt, sl: (qt[t], h)),
                  pl.BlockSpec((tk, hd), lambda h, t, qt, kt, sl: (kt[t], n_heads + h)),
                  pl.BlockSpec((hv, tk), lambda h, t, qt, kt, sl: (h, kt[t])),
                  pl.BlockSpec((tk, hd), lambda h, t, qt, kt, sl: (0, 0)),
                  small, small, small, small,
                  pl.BlockSpec((hd, 1), lambda h, t, qt, kt, sl: (0, 0))],
        out_specs=pl.BlockSpec((tq, hd), lambda h, t, qt, kt, sl: (qt[t], h)),
        scratch_shapes=[pltpu.VMEM((tq, 2 * hd), BF16), pltpu.VMEM((tq, 2 * hd), BF16),
                        pltpu.VMEM((1, tq), F32), pltpu.VMEM((hv, tq), F32),
                        pltpu.VMEM((1, tq), F32), pltpu.VMEM((hv, tq), F32),
                        pltpu.VMEM((tk, tq), I32)])
    return pl.pallas_call(
        functools.partial(_attn_body, tq=tq, tk=tk, lam_init=lam_init),
        grid_spec=grid_spec,
        out_shape=jax.ShapeDtypeStruct((s, n_heads * hd), BF16),
        compiler_params=_cparams("arbitrary", "arbitrary"),
        name="diff_attention",
    )(qi_tab, kv_tab, slopes, qkvu, qkvu, v_t, jnp.asarray(pos, BF16),
      lam_q1.reshape(1, -1), lam_k1.reshape(1, -1), lam_q2.reshape(1, -1), lam_k2.reshape(1, -1),
      subln_g.reshape(-1, 1))


def _attn_value_rows(v, n_heads):
    s = v.shape[0]
    vt = v.T.reshape(n_heads, A_VDIM, s)
    ones = jnp.ones((n_heads, ATTN_ONES_ROWS, s), v.dtype)
    return jnp.concatenate([vt, ones], axis=1).reshape(n_heads * (A_VDIM + ATTN_ONES_ROWS), s)


def _pool_body(ucur_ref, uprev_ref, w_ref, sc_ref, o_ref, *, t_blk):
    g = pl.program_id(0)
    i = pl.program_id(1)
    win = jnp.left_shift(2, g)
    row = lax.broadcasted_iota(I32, (t_blk, t_blk), 0)
    col = lax.broadcasted_iota(I32, (t_blk, t_blk), 1)
    d = row - col
    band = jnp.where(jnp.logical_and(d >= 0, d < win), 1.0, 0.0).astype(BF16)
    rowp = lax.broadcasted_iota(I32, (t_blk, POOL_HALO), 0)
    colp = lax.broadcasted_iota(I32, (t_blk, POOL_HALO), 1)
    dp = rowp + POOL_HALO - colp
    bandp = jnp.where(jnp.logical_and(dp < win, i > 0), 1.0, 0.0).astype(BF16)
    u = ucur_ref[...]
    usum = _dot(band, u) + _dot(bandp, uprev_ref[...])
    tpos = i * t_blk + lax.broadcasted_iota(I32, (t_blk, 1), 0)
    cnt = jnp.minimum(tpos + 1, win).astype(F32)
    dev = usum / cnt - u.astype(F32)
    y = _dot(dev.astype(BF16), w_ref[...]) * sc_ref[...]
    o_ref[...] = y.astype(o_ref.dtype)


def _multiscale_pool(qkvu, pool_w, pool_scale, *, col_off, t_blk=256):
    s = qkvu.shape[0]
    ng, gd, _ = pool_w.shape
    goff = col_off // gd
    hb = t_blk // POOL_HALO
    return pl.pallas_call(
        functools.partial(_pool_body, t_blk=t_blk),
        grid=(ng, s // t_blk),
        in_specs=[pl.BlockSpec((t_blk, gd), lambda g, i: (i, goff + g)),
                  pl.BlockSpec((POOL_HALO, gd), lambda g, i: (jnp.maximum(i * hb - 1, 0), goff + g)),
                  pl.BlockSpec((None, gd, gd), lambda g, i: (g, 0, 0)),
                  pl.BlockSpec((1, gd), lambda g, i: (0, g))],
        out_specs=pl.BlockSpec((t_blk, gd), lambda g, i: (i, g)),
        out_shape=jax.ShapeDtypeStruct((s, ng * gd), BF16),
        compiler_params=_cparams("arbitrary", "arbitrary"),
        name="multiscale_pool",
    )(qkvu, qkvu, pool_w, pool_scale.reshape(1, -1))


def _hgrn_tables():
    c = HGRN_CHUNK
    idx = np.arange(c)
    mats, masks = [], []
    h = c // 2
    while h >= 1:
        upper = (idx % (2 * h)) >= h
        e = idx - (idx % (2 * h)) + h - 1
        u = idx[None, :]
        pq = (upper[:, None] & (u > e[:, None]) & (u <= idx[:, None]))
        pk = ((~upper)[:, None] & (u > idx[:, None]) & (u <= e[:, None]))
        mats += [pq, pk]
        masks.append(upper[:, None] & (~upper)[None, :] & ((idx[:, None] // (2 * h)) == (idx[None, :] // (2 * h))))
        h //= 2
    masks.append(idx[:, None] == idx[None, :])
    mats.append(idx[None, :] <= idx[:, None])
    mats.append(idx[None, :] > idx[:, None])
    return (np.concatenate(mats, axis=0).astype(np.float32),
            np.stack(masks, axis=0).astype(np.float32))


def _hgrn_body(q_ref, f_ref, v_ref, gs_ref, p_ref, mask_ref, gn_ref, o_ref, st_ref, *, t_blk, n_grp):
    c = HGRN_CHUNK
    n_lvl = mask_ref.shape[0] - 1
    heads = range(n_grp)

    @pl.when(pl.program_id(1) == 0)
    def _():
        st_ref[...] = jnp.zeros(st_ref.shape, F32)

    def col(x, hg):
        return x[:, hg * C_KDIM:(hg + 1) * C_KDIM]

    def stack(parts):
        return jnp.concatenate(parts, axis=0)

    def block_diag(parts):
        zero = jnp.zeros_like(parts[0])
        return stack([jnp.concatenate([parts[hg] if j == hg else zero for j in heads], axis=1) for hg in heads])

    def chunk(ci, carry):
        rows = pl.ds(pl.multiple_of(ci * c, c), c)
        f = f_ref[rows, :]
        lf = jnp.log(f)
        kk = 1.0 - f
        hi = lf.astype(BF16)
        r1 = lf - hi.astype(F32)
        mid = r1.astype(BF16)
        lo = (r1 - mid.astype(F32)).astype(BF16)
        ed = jnp.exp(_dot(p_ref[...], stack([hi, mid, lo])))
        e = [col(ed, hg) for hg in heads]
        q = q_ref[rows, :].astype(F32)
        qh = [col(q, hg) for hg in heads]
        kh = [col(kk, hg) for hg in heads]
        attn = mask_ref[n_lvl] * _dot_t(stack([x.astype(BF16) for x in qh]), stack([x.astype(BF16) for x in kh]))
        for lv in range(n_lvl):
            qs = stack([(qh[hg] * e[hg][2 * lv * c:(2 * lv + 1) * c]).astype(BF16) for hg in heads])
            ks = stack([(kh[hg] * e[hg][(2 * lv + 1) * c:(2 * lv + 2) * c]).astype(BF16) for hg in heads])
            attn = attn + mask_ref[lv] * _dot_t(qs, ks)
        eb = [e[hg][2 * n_lvl * c:(2 * n_lvl + 1) * c] for hg in heads]
        el = [e[hg][(2 * n_lvl + 1) * c:(2 * n_lvl + 2) * c] for hg in heads]
        v = v_ref[rows, :]
        v_st = stack([col(v, hg) for hg in heads])
        st = st_ref[...]
        q_bd = block_diag([(qh[hg] * eb[hg]).astype(BF16) for hg in heads])
        o = _dot(attn.astype(BF16), v_st) + _dot_t(q_bd, st.astype(BF16))
        k_bd = block_diag([(kh[hg] * el[hg]).astype(BF16) for hg in heads])
        decay = jnp.concatenate([eb[hg][c - 1:c, :] for hg in heads], axis=1)
        st_ref[...] = st * decay + lax.dot_general(v_st, k_bd, (((0,), (0,)), ((), ())),
                                                   preferred_element_type=F32)
        ms = jnp.mean(o * o, axis=-1, keepdims=True)
        on = o * lax.rsqrt(ms + RMS_EPS) * gn_ref[...]
        gs = gs_ref[rows, :].astype(F32)
        for hg in heads:
            o_ref[rows, hg * C_VDIM:(hg + 1) * C_VDIM] = (on[hg * c:(hg + 1) * c] * col(gs, hg)).astype(o_ref.dtype)
        return carry

    lax.fori_loop(0, t_blk // c, chunk, 0)


def _hgrn2(qs, forget, v, gs, gnorm_g, *, t_blk=512, n_grp=4):
    s, d = qs.shape
    n_heads = d // C_KDIM
    p_np, mask_np = _hgrn_tables()
    p_mat = jnp.asarray(np.concatenate([p_np] * 3, axis=1), BF16)
    masks = jnp.asarray(np.stack([np.kron(np.eye(n_grp, dtype=np.float32), m) for m in mask_np]), F32)
    blk = lambda: pl.BlockSpec((t_blk, n_grp * C_KDIM), lambda h, i: (i, h))
    return pl.pallas_call(
        functools.partial(_hgrn_body, t_blk=t_blk, n_grp=n_grp),
        grid=(n_heads // n_grp, s // t_blk),
        in_specs=[blk(), blk(), blk(), blk(),
                  pl.BlockSpec(p_mat.shape, lambda h, i: (0, 0)),
                  pl.BlockSpec(masks.shape, lambda h, i: (0, 0, 0)),
                  pl.BlockSpec((1, C_VDIM), lambda h, i: (0, 0))],
        out_specs=blk(),
        out_shape=jax.ShapeDtypeStruct((s, d), BF16),
        scratch_shapes=[pltpu.VMEM((C_VDIM, n_grp * C_KDIM), F32)],
        compiler_params=_cparams("arbitrary", "arbitrary"),
        name="hgrn2",
    )(qs, forget, v, gs, p_mat, masks, gnorm_g.reshape(1, -1))


def _router_body(x_ref, sc_ref, sh_ref, whi_ref, wlo_ref, info_ref, cnt_ref, carry_s, *, tm):
    @pl.when(pl.program_id(0) == 0)
    def _():
        carry_s[...] = jnp.zeros(carry_s.shape, F32)

    hmod = x_ref[...] * (1.0 + sc_ref[...]) + sh_ref[...]
    hh = hmod.astype(BF16)
    hl = (hmod - hh.astype(F32)).astype(BF16)
    whi = whi_ref[...]
    logits = _dot(hh, whi) + _dot(hl, whi) + _dot(hh, wlo_ref[...])
    lane = lax.broadcasted_iota(I32, (tm, LANES), 1)
    logits = jnp.where(lane < N_EXPERTS, logits, -jnp.inf)
    m1 = jnp.max(logits, axis=-1, keepdims=True)
    e1 = jnp.min(jnp.where(logits == m1, lane, LANES), axis=-1, keepdims=True)
    rest = jnp.where(lane == e1, -jnp.inf, logits)
    m2 = jnp.max(rest, axis=-1, keepdims=True)
    e2 = jnp.min(jnp.where(rest == m2, lane, LANES), axis=-1, keepdims=True)
    ex = jnp.exp(m2 - m1)
    g1 = 1.0 / (1.0 + ex)
    g2 = ex / (1.0 + ex)
    onehot = jnp.where(jnp.logical_or(lane == e1, lane == e2), 1.0, 0.0)
    row = lax.broadcasted_iota(I32, (tm, tm), 0)
    col = lax.broadcasted_iota(I32, (tm, tm), 1)
    before = jnp.where(row > col, 1.0, 0.0).astype(BF16)
    cum = _dot(before, onehot.astype(BF16)) + carry_s[...]
    r1 = jnp.sum(jnp.where(lane == e1, cum, 0.0), axis=-1, keepdims=True)
    r2 = jnp.sum(jnp.where(lane == e2, cum, 0.0), axis=-1, keepdims=True)
    carry_s[...] = carry_s[...] + jnp.sum(onehot, axis=0, keepdims=True)
    cnt_ref[...] = carry_s[...]
    info = jnp.where(lane == 0, e1.astype(F32), 0.0)
    info = jnp.where(lane == 1, e2.astype(F32), info)
    info = jnp.where(lane == 2, g1, info)
    info = jnp.where(lane == 3, g2, info)
    info = jnp.where(lane == 4, r1, info)
    info = jnp.where(lane == 5, r2, info)
    info_ref[...] = info


def _router(x, scale, shift, router_w, *, tm=256):
    s, d = x.shape
    wpad = jnp.zeros((d, LANES), F32).at[:, :N_EXPERTS].set(router_w)
    whi = wpad.astype(BF16)
    wlo = (wpad - whi.astype(F32)).astype(BF16)
    vec = pl.BlockSpec((1, d), lambda i: (0, 0))
    wspec = pl.BlockSpec((d, LANES), lambda i: (0, 0))
    return pl.pallas_call(
        functools.partial(_router_body, tm=tm),
        grid=(s // tm,),
        in_specs=[pl.BlockSpec((tm, d), lambda i: (i, 0)), vec, vec, wspec, wspec],
        out_specs=[pl.BlockSpec((tm, LANES), lambda i: (i, 0)),
                   pl.BlockSpec((1, LANES), lambda i: (0, 0))],
        out_shape=[jax.ShapeDtypeStruct((s, LANES), F32), jax.ShapeDtypeStruct((1, LANES), F32)],
        scratch_shapes=[pltpu.VMEM((1, LANES), F32)],
        compiler_params=_cparams("arbitrary"),
        name="router",
    )(x, scale.reshape(1, d), shift.reshape(1, d), whi, wlo)


def _dispatch_copies(s1_ref, s2_ref, h_ref, out_ref, sem, base, r):
    t = base + r
    src = h_ref.at[pl.ds(r, 1)]
    return (pltpu.make_async_copy(src, out_ref.at[pl.ds(s1_ref[t], 1)], sem.at[0]),
            pltpu.make_async_copy(src, out_ref.at[pl.ds(s2_ref[t], 1)], sem.at[1]))


def _dispatch_body(s1_ref, s2_ref, h_ref, init_ref, out_ref, sem, *, tb):
    del init_ref
    base = pl.program_id(0) * tb

    def issue(r, carry):
        for prio, cp in enumerate(_dispatch_copies(s1_ref, s2_ref, h_ref, out_ref, sem, base, r)):
            cp.start(priority=prio)
        return carry

    def drain(r, carry):
        for cp in _dispatch_copies(s1_ref, s2_ref, h_ref, out_ref, sem, base, r):
            cp.wait()
        return carry

    lax.fori_loop(0, tb, issue, 0, unroll=DMA_ISSUE_UNROLL)
    lax.fori_loop(0, tb, drain, 0, unroll=DMA_ISSUE_UNROLL)


def _dispatch(h_words, slot1, slot2, n_rows, *, tb=256):
    s, dw = h_words.shape
    init = jnp.zeros((n_rows, dw), h_words.dtype)
    grid_spec = pltpu.PrefetchScalarGridSpec(
        num_scalar_prefetch=2,
        grid=(s // tb,),
        in_specs=[pl.BlockSpec((tb, dw), lambda i, a, b: (i, 0)),
                  pl.BlockSpec(memory_space=pl.ANY)],
        out_specs=pl.BlockSpec(memory_space=pl.ANY),
        scratch_shapes=[pltpu.SemaphoreType.DMA((2,))])
    return pl.pallas_call(
        functools.partial(_dispatch_body, tb=tb),
        grid_spec=grid_spec,
        out_shape=jax.ShapeDtypeStruct((n_rows, dw), h_words.dtype),
        input_output_aliases={3: 0},
        compiler_params=_cparams("arbitrary"),
        name="moe_dispatch",
    )(slot1, slot2, h_words, init)


def _expert_rows_body(off_ref, cnt_ref, *refs, swiglu, tr, tn, n_rows):
    if swiglu:
        wa_ref, wb_ref, x_hbm, o_hbm, wa_s, wb_s, xbuf, obuf, xsem, osem = refs
    else:
        w_ref, x_hbm, o_hbm, w_s, xbuf, obuf, xsem, osem = refs
    j = pl.program_id(0)
    e = pl.program_id(1)
    n = (cnt_ref[e] + (tr - 1)) // tr
    col0 = pl.multiple_of(j * tn, tn)

    def row(t):
        return pl.multiple_of(off_ref[e] + t * tr, tr)

    def x_copy(row0, slot):
        return pltpu.make_async_copy(x_hbm.at[pl.ds(row0, tr)], xbuf.at[slot], xsem.at[slot])

    def o_copy(row0, slot):
        return pltpu.make_async_copy(obuf.at[slot], o_hbm.at[pl.ds(row0, tr), pl.ds(col0, tn)], osem.at[slot])

    @pl.when(n > 0)
    def _():
        x_copy(row(0), 0).start(priority=ROW_TILE_DMA_PRIORITY)

    if swiglu:
        wa_s[...] = wa_ref[...].astype(BF16)
        wb_s[...] = wb_ref[...].astype(BF16)
    else:
        w_s[...] = w_ref[...].astype(BF16)

    def tile(t, carry):
        slot = lax.rem(t, 2)

        @pl.when(t + 1 < n)
        def _():
            x_copy(row(t + 1), 1 - slot).start(priority=ROW_TILE_DMA_PRIORITY)

        x_copy(row(t), slot).wait()

        @pl.when(t >= 2)
        def _():
            o_copy(row(t - 2), slot).wait()

        if swiglu:
            x = _unpack_bf16_pairs(xbuf[slot])
            res = _silu(_dot(x, wa_s[...])) * _dot(x, wb_s[...])
        else:
            res = _dot(xbuf[slot], w_s[...])
        obuf[slot] = res.astype(obuf.dtype)
        o_copy(row(t), slot).start()
        return carry

    lax.fori_loop(0, n, tile, 0)

    @pl.when(n >= 2)
    def _():
        o_copy(row(n - 2), lax.rem(n, 2)).wait()

    @pl.when(n >= 1)
    def _():
        o_copy(row(n - 1), lax.rem(n - 1, 2)).wait()

    @pl.when(e == pl.num_programs(1) - 1)
    def _():
        obuf[0] = jnp.zeros(obuf.shape[1:], obuf.dtype)
        end = off_ref[e] + n * tr

        def fill(t, carry):
            cp = o_copy(pl.multiple_of(end + t * tr, tr), 0)
            cp.start()
            cp.wait()
            return carry

        lax.fori_loop(0, (n_rows - end) // tr, fill, 0)


def _expert_in(xs_words, w_in, offs, counts, *, tr, tn=512):
    p, kw = xs_words.shape
    k = 2 * kw
    n_exp = w_in.shape[0]
    half = w_in.shape[2] // 2
    hoff = half // tn
    grid_spec = pltpu.PrefetchScalarGridSpec(
        num_scalar_prefetch=2,
        grid=(half // tn, n_exp),
        in_specs=[pl.BlockSpec((None, k, tn), lambda j, e, off, cnt: (e, 0, j)),
                  pl.BlockSpec((None, k, tn), lambda j, e, off, cnt: (e, 0, j + hoff)),
                  pl.BlockSpec(memory_space=pl.ANY)],
        out_specs=pl.BlockSpec(memory_space=pl.ANY),
        scratch_shapes=[pltpu.VMEM((k, tn), BF16), pltpu.VMEM((k, tn), BF16),
                        pltpu.VMEM((2, tr, kw), xs_words.dtype), pltpu.VMEM((2, tr, tn), BF16),
                        pltpu.SemaphoreType.DMA((2,)), pltpu.SemaphoreType.DMA((2,))])
    return pl.pallas_call(
        functools.partial(_expert_rows_body, swiglu=True, tr=tr, tn=tn, n_rows=p),
        grid_spec=grid_spec,
        out_shape=jax.ShapeDtypeStruct((p, half), BF16),
        compiler_params=_cparams("arbitrary", "arbitrary"),
        name="expert_in",
    )(offs, counts, w_in, w_in, xs_words)


def _expert_out(gs, w_out, offs, counts, *, tr, tn=1024):
    p, k = gs.shape
    n_exp = w_out.shape[0]
    n = w_out.shape[2]
    grid_spec = pltpu.PrefetchScalarGridSpec(
        num_scalar_prefetch=2,
        grid=(n // tn, n_exp),
        in_specs=[pl.BlockSpec((None, k, tn), lambda j, e, off, cnt: (e, 0, j)),
                  pl.BlockSpec(memory_space=pl.ANY)],
        out_specs=pl.BlockSpec(memory_space=pl.ANY),
        scratch_shapes=[pltpu.VMEM((k, tn), BF16),
                        pltpu.VMEM((2, tr, k), gs.dtype), pltpu.VMEM((2, tr, tn), F32),
                        pltpu.SemaphoreType.DMA((2,)), pltpu.SemaphoreType.DMA((2,))])
    return pl.pallas_call(
        functools.partial(_expert_rows_body, swiglu=False, tr=tr, tn=tn, n_rows=p),
        grid_spec=grid_spec,
        out_shape=jax.ShapeDtypeStruct((p, n), F32),
        compiler_params=_cparams("arbitrary", "arbitrary"),
        name="expert_out",
    )(offs, counts, w_out, gs)


def _combine_copies(s1_ref, s2_ref, ys_ref, buf, sem, blk, slot, r, tb):
    t = blk * tb + r
    return (pltpu.make_async_copy(ys_ref.at[pl.ds(s1_ref[t], 1)], buf.at[slot, 0, pl.ds(r, 1)], sem.at[slot, 0]),
            pltpu.make_async_copy(ys_ref.at[pl.ds(s2_ref[t], 1)], buf.at[slot, 1, pl.ds(r, 1)], sem.at[slot, 1]))


def _combine_body(s1_ref, s2_ref, x_ref, info_ref, ys_ref, gate_ref, g_ref, b_ref, o_ref, buf, sem,
                  *, tb, alpha):
    i = pl.program_id(0)
    slot = lax.rem(i, 2)

    def gather(blk, slt):
        def issue(r, carry):
            for prio, cp in enumerate(_combine_copies(s1_ref, s2_ref, ys_ref, buf, sem, blk, slt, r, tb)):
                cp.start(priority=prio)
            return carry
        lax.fori_loop(0, tb, issue, 0, unroll=DMA_ISSUE_UNROLL)

    @pl.when(i == 0)
    def _():
        gather(i, slot)

    @pl.when(i + 1 < pl.num_programs(0))
    def _():
        gather(i + 1, 1 - slot)

    def drain(r, carry):
        for cp in _combine_copies(s1_ref, s2_ref, ys_ref, buf, sem, i, slot, r, tb):
            cp.wait()
        return carry

    lax.fori_loop(0, tb, drain, 0, unroll=DMA_ISSUE_UNROLL)
    info = info_ref[...]
    y = info[:, 2:3] * buf[slot, 0] + info[:, 3:4] * buf[slot, 1]
    r = alpha * x_ref[...] + (1.0 + gate_ref[...]) * y
    o_ref[...] = _layer_norm_rows(r, g_ref[...], b_ref[...])


def _combine_postnorm(x, info, ys, slot1, slot2, gate, g, b, *, alpha, tb=128):
    s, d = x.shape
    vec = pl.BlockSpec((1, d), lambda i, a, c: (0, 0))
    grid_spec = pltpu.PrefetchScalarGridSpec(
        num_scalar_prefetch=2,
        grid=(s // tb,),
        in_specs=[pl.BlockSpec((tb, d), lambda i, a, c: (i, 0)),
                  pl.BlockSpec((tb, LANES), lambda i, a, c: (i, 0)),
                  pl.BlockSpec(memory_space=pl.ANY),
                  vec, vec, vec],
        out_specs=pl.BlockSpec((tb, d), lambda i, a, c: (i, 0)),
        scratch_shapes=[pltpu.VMEM((2, 2, tb, d), F32), pltpu.SemaphoreType.DMA((2, 2))])
    return pl.pallas_call(
        functools.partial(_combine_body, tb=tb, alpha=alpha),
        grid_spec=grid_spec,
        out_shape=jax.ShapeDtypeStruct((s, d), F32),
        compiler_params=_cparams("arbitrary"),
        name="moe_combine_postnorm",
    )(slot1, slot2, x, info, ys, gate.reshape(1, d), g.reshape(1, d), b.reshape(1, d))


def _moe_plan(info, counts, *, tr):
    e1 = info[:, 0].astype(I32)
    e2 = info[:, 1].astype(I32)
    r1 = info[:, 4].astype(I32)
    r2 = info[:, 5].astype(I32)
    cnt = counts[0, :N_EXPERTS].astype(I32)
    padded = (cnt + tr - 1) // tr * tr
    row_end = jnp.cumsum(padded)
    row_off = row_end - padded
    slot1 = row_off[e1] + r1
    slot2 = row_off[e2] + r2
    offs = jnp.concatenate([row_off, row_end[-1:]]).astype(I32)
    return slot1, slot2, offs, cnt


def kernel(x, c, ada_w, ada_b, ln_g, ln_b, even_w_in, lam_q1, lam_k1, lam_q2, lam_k2, subln_g, pool_w,
           pool_scale, even_w_out, ffn_w_in, ffn_w_out, odd_w_in, lb_raw, gnorm_g, odd_w_out, router_w,
           exp_w_in, exp_w_out):
    _, seq, d = x.shape
    depth = ada_w.shape[0]
    alpha = (2 * depth) ** 0.25
    a_width = d // 2
    a_heads = a_width // A_VDIM
    x2 = x.reshape(seq, d)

    lb_all = jnp.cumsum(jax.nn.softmax(lb_raw.astype(F32), axis=0), axis=0)
    lb_all = lb_all - lb_all[0]
    mod = _ada_mod(c, ada_w, ada_b)

    def mod_parts(l):
        return [mod[l, i * d:(i + 1) * d] for i in range(6)]

    sh1, sc1, g1, sh2, sc2, g2 = mod_parts(0)
    h = _modulate(x2, sc1, sh1)
    qkvu = _proj([h], even_w_in[0], col_off=0, n_cols=3 * a_width + (d - a_width), out_dtype=BF16,
                 epilogue="qscale", q_cols=a_width, q_scale=A_QKDIM ** -0.5 * math.log2(math.e),
                 name="even_in_proj")
    lam_init = 0.8 - 0.6 * math.exp(-0.3 * 0)
    v_t = _attn_value_rows(qkvu[:, 2 * a_width:3 * a_width], a_heads)
    o_a = _diff_attention(qkvu, v_t, lam_q1[0], lam_k1[0], lam_q2[0], lam_k2[0], subln_g[0],
                          n_heads=a_heads, lam_init=lam_init)
    o_b = _multiscale_pool(qkvu, pool_w[0].astype(BF16), pool_scale[0], col_off=3 * a_width)
    y = _proj([o_a, o_b], even_w_out[0], col_off=0, n_cols=d, out_dtype=BF16, name="even_out_proj")
    x2, h = _postnorm(x2, y, g1, ln_g[0, 0], ln_b[0, 0], sc2, sh2, alpha=alpha)

    ff = ffn_w_out.shape[1]
    gact = _swiglu_in(h, ffn_w_in[0])
    y = _mm_acc(gact, ffn_w_out[0], tk=ff // 2, out_dtype=BF16)
    sh1, sc1, g1n, sh2n, sc2n, g2n = mod_parts(1)
    x2, h = _postnorm(x2, y, g2, ln_g[0, 1], ln_b[0, 1], sc1, sh1, alpha=alpha)

    w_odd = odd_w_in[0]
    qs = _proj([h], w_odd, col_off=0, n_cols=d, out_dtype=BF16, epilogue="silu", name="odd_in_q")
    fg = _proj([h], w_odd, col_off=d, n_cols=d, out_dtype=F32, epilogue="forget", lb=lb_all[1], name="odd_in_f")
    vi = _proj([h], w_odd, col_off=2 * d, n_cols=d, out_dtype=BF16, name="odd_in_i")
    gs = _proj([h], w_odd, col_off=3 * d, n_cols=d, out_dtype=BF16, epilogue="silu", name="odd_in_g")
    o_c = _hgrn2(qs, fg, vi, gs, gnorm_g[0])
    y = _proj([o_c], odd_w_out[0], col_off=0, n_cols=d, out_dtype=BF16, name="odd_out_proj")
    x2, h_words = _postnorm(x2, y, g1n, ln_g[1, 0], ln_b[1, 0], sc2n, sh2n, alpha=alpha, pack=True)

    tr = 512
    n_rows = 2 * seq + N_EXPERTS * tr
    info, counts = _router(x2, sc2n, sh2n, router_w[0])
    slot1, slot2, offs, cnt = _moe_plan(info, counts, tr=tr)
    xs_words = _dispatch(h_words, slot1, slot2, n_rows)
    gexp = _expert_in(xs_words, exp_w_in[0], offs, cnt, tr=tr)
    ys = _expert_out(gexp, exp_w_out[0], offs, cnt, tr=tr)
    out = _combine_postnorm(x2, info, ys, slot1, slot2, g2n, ln_g[1, 1], ln_b[1, 1], alpha=alpha)
    return out.reshape(x.shape)
```

```python
import functools
import math

import numpy as np
import jax
import jax.numpy as jnp
from jax import lax
from jax.experimental import pallas as pl
from jax.experimental.pallas import tpu as pltpu

F32 = jnp.float32
BF16 = jnp.bfloat16
I32 = jnp.int32

A_VDIM = 128
A_QKDIM = 64
POOL_WINDOWS = (2, 4, 8, 16)
POOL_HALO = 16
C_KDIM = 128
C_VDIM = 128
HGRN_CHUNK = 64
N_EXPERTS = 8
LN_EPS = 1e-5
RMS_EPS = 1e-6
NEG_INF = -1e30

LANES = 128
VMEM_LIMIT_BYTES = 56 * 1024 * 1024
DMA_ISSUE_UNROLL = 8
ROW_TILE_DMA_PRIORITY = 1


def _cparams(*sem):
    return pltpu.CompilerParams(dimension_semantics=sem, vmem_limit_bytes=VMEM_LIMIT_BYTES)


def _silu(x):
    return x * (1.0 / (1.0 + jnp.exp(-x)))


def _dot(a, b):
    return jnp.dot(a, b, preferred_element_type=F32)


def _dot_t(a, b):
    return lax.dot_general(a, b, (((1,), (1,)), ((), ())), preferred_element_type=F32)


def _pack_bf16_pairs(h):
    half = h.shape[1] // 2
    bits = pltpu.bitcast(h.astype(BF16).astype(F32), I32)
    return jnp.bitwise_or(bits[:, :half], lax.shift_right_logical(bits[:, half:], 16))


def _unpack_bf16_pairs(w):
    hi = pltpu.bitcast(jnp.bitwise_and(w, -65536), F32).astype(BF16)
    lo = pltpu.bitcast(lax.shift_left(w, 16), F32).astype(BF16)
    return jnp.concatenate([hi, lo], axis=1)


def _ada_body(c_ref, w_ref, b_ref, o_ref):
    ca = _silu(c_ref[...]).astype(BF16)
    o_ref[...] = _dot(ca, w_ref[...].astype(BF16)) + b_ref[...]


def _ada_mod(c, ada_w, ada_b, *, tn=1024):
    depth, d, n = ada_w.shape
    c8 = jnp.broadcast_to(c.astype(F32), (8, d))
    out = pl.pallas_call(
        _ada_body,
        grid=(depth, n // tn),
        in_specs=[pl.BlockSpec((8, d), lambda l, j: (0, 0)),
                  pl.BlockSpec((None, d, tn), lambda l, j: (l, 0, j)),
                  pl.BlockSpec((None, 1, tn), lambda l, j: (l, 0, j))],
        out_specs=pl.BlockSpec((None, 8, tn), lambda l, j: (l, 0, j)),
        out_shape=jax.ShapeDtypeStruct((depth, 8, n), F32),
        compiler_params=_cparams("arbitrary", "arbitrary"),
        name="ada_mod",
    )(c8, ada_w, ada_b.reshape(depth, 1, n))
    return out[:, 0, :]


def _modulate_body(x_ref, sc_ref, sh_ref, o_ref):
    o_ref[...] = (x_ref[...] * (1.0 + sc_ref[...]) + sh_ref[...]).astype(o_ref.dtype)


def _modulate(x, scale, shift, *, tm=512):
    s, d = x.shape
    vec = pl.BlockSpec((1, d), lambda i: (0, 0))
    return pl.pallas_call(
        _modulate_body,
        grid=(s // tm,),
        in_specs=[pl.BlockSpec((tm, d), lambda i: (i, 0)), vec, vec],
        out_specs=pl.BlockSpec((tm, d), lambda i: (i, 0)),
        out_shape=jax.ShapeDtypeStruct((s, d), BF16),
        compiler_params=_cparams("arbitrary"),
        name="modulate",
    )(x, scale.reshape(1, d), shift.reshape(1, d))


def _layer_norm_rows(r, g, b):
    mu = jnp.mean(r, axis=-1, keepdims=True)
    rc = r - mu
    var = jnp.mean(rc * rc, axis=-1, keepdims=True)
    return rc * lax.rsqrt(var + LN_EPS) * g + b


def _postnorm_body(x_ref, y_ref, gate_ref, g_ref, b_ref, sc_ref, sh_ref, xo_ref, ho_ref, *, alpha, pack):
    r = alpha * x_ref[...] + (1.0 + gate_ref[...]) * y_ref[...].astype(F32)
    xn = _layer_norm_rows(r, g_ref[...], b_ref[...])
    xo_ref[...] = xn
    hn = xn * (1.0 + sc_ref[...]) + sh_ref[...]
    ho_ref[...] = _pack_bf16_pairs(hn) if pack else hn.astype(ho_ref.dtype)


def _postnorm(x, y, gate, g, b, nscale, nshift, *, alpha, pack=False, tm=256):
    s, d = x.shape
    row = pl.BlockSpec((tm, d), lambda i: (i, 0))
    vec = pl.BlockSpec((1, d), lambda i: (0, 0))
    hd, hdt = (d // 2, I32) if pack else (d, BF16)
    return pl.pallas_call(
        functools.partial(_postnorm_body, alpha=alpha, pack=pack),
        grid=(s // tm,),
        in_specs=[row, row, vec, vec, vec, vec, vec],
        out_specs=[row, pl.BlockSpec((tm, hd), lambda i: (i, 0))],
        out_shape=[jax.ShapeDtypeStruct((s, d), F32), jax.ShapeDtypeStruct((s, hd), hdt)],
        compiler_params=_cparams("arbitrary"),
        name="postnorm",
    )(x, y, gate.reshape(1, d), g.reshape(1, d), b.reshape(1, d), nscale.reshape(1, d), nshift.reshape(1, d))


def _proj_body(*refs, n_a, epilogue, tn, q_cols, q_scale):
    a_refs, w_ref, extra = refs[:n_a], refs[n_a], refs[n_a + 1:-2]
    o_ref, wb_s = refs[-2], refs[-1]

    @pl.when(pl.program_id(1) == 0)
    def _():
        wb_s[...] = w_ref[...].astype(BF16)

    acc, off = None, 0
    for a_ref in a_refs:
        kp = a_ref.shape[1]
        part = _dot(a_ref[...], wb_s[off:off + kp, :])
        acc = part if acc is None else acc + part
        off += kp
    if epilogue == "silu":
        acc = _silu(acc)
    elif epilogue == "forget":
        lb = extra[0][...]
        acc = lb + (1.0 - lb) * (1.0 / (1.0 + jnp.exp(-acc)))
    elif epilogue == "qscale":
        acc = acc * jnp.where(pl.program_id(0) * tn < q_cols, q_scale, 1.0)
    o_ref[...] = acc.astype(o_ref.dtype)


def _proj(a_parts, w, *, col_off, n_cols, out_dtype, epilogue="id", lb=None, q_cols=0, q_scale=1.0,
          tm=1024, tn=512, name="proj"):
    m = a_parts[0].shape[0]
    k = w.shape[0]
    joff = col_off // tn
    in_specs = [pl.BlockSpec((tm, a.shape[1]), lambda j, i: (i, 0)) for a in a_parts]
    in_specs.append(pl.BlockSpec((k, tn), lambda j, i: (0, j + joff)))
    args = list(a_parts) + [w]
    if epilogue == "forget":
        in_specs.append(pl.BlockSpec((1, tn), lambda j, i: (0, j)))
        args.append(lb.reshape(1, n_cols))
    return pl.pallas_call(
        functools.partial(_proj_body, n_a=len(a_parts), epilogue=epilogue, tn=tn, q_cols=q_cols, q_scale=q_scale),
        grid=(n_cols // tn, m // tm),
        in_specs=in_specs,
        out_specs=pl.BlockSpec((tm, tn), lambda j, i: (i, j)),
        out_shape=jax.ShapeDtypeStruct((m, n_cols), out_dtype),
        scratch_shapes=[pltpu.VMEM((k, tn), BF16)],
        compiler_params=_cparams("arbitrary", "arbitrary"),
        name=name,
    )(*args)


def _swiglu_in_body(a_ref, wa_ref, wb_ref, o_ref, wa_s, wb_s):
    @pl.when(pl.program_id(1) == 0)
    def _():
        wa_s[...] = wa_ref[...].astype(BF16)
        wb_s[...] = wb_ref[...].astype(BF16)

    a = a_ref[...]
    o_ref[...] = (_silu(_dot(a, wa_s[...])) * _dot(a, wb_s[...])).astype(o_ref.dtype)


def _swiglu_in(a, w, *, tm=1024, tn=256):
    m, k = a.shape
    half = w.shape[1] // 2
    hoff = half // tn
    return pl.pallas_call(
        _swiglu_in_body,
        grid=(half // tn, m // tm),
        in_specs=[pl.BlockSpec((tm, k), lambda j, i: (i, 0)),
                  pl.BlockSpec((k, tn), lambda j, i: (0, j)),
                  pl.BlockSpec((k, tn), lambda j, i: (0, j + hoff))],
        out_specs=pl.BlockSpec((tm, tn), lambda j, i: (i, j)),
        out_shape=jax.ShapeDtypeStruct((m, half), BF16),
        scratch_shapes=[pltpu.VMEM((k, tn), BF16), pltpu.VMEM((k, tn), BF16)],
        compiler_params=_cparams("arbitrary", "arbitrary"),
        name="swiglu_in",
    )(a, w, w)


def _mm_acc_body(a_ref, w_ref, o_ref, acc_ref, *, nk):
    kk = pl.program_id(2)
    part = _dot(a_ref[...], w_ref[...].astype(BF16))

    @pl.when(kk == 0)
    def _():
        acc_ref[...] = part

    @pl.when(kk > 0)
    def _():
        acc_ref[...] += part

    @pl.when(kk == nk - 1)
    def _():
        o_ref[...] = acc_ref[...].astype(o_ref.dtype)


def _mm_acc(a, w, *, tm=1024, tn=512, tk, out_dtype=F32):
    m, k = a.shape
    n = w.shape[1]
    nk = k // tk
    return pl.pallas_call(
        functools.partial(_mm_acc_body, nk=nk),
        grid=(m // tm, n // tn, nk),
        in_specs=[pl.BlockSpec((tm, tk), lambda i, j, kk: (i, kk)),
                  pl.BlockSpec((tk, tn), lambda i, j, kk: (kk, j))],
        out_specs=pl.BlockSpec((tm, tn), lambda i, j, kk: (i, j)),
        out_shape=jax.ShapeDtypeStruct((m, n), out_dtype),
        scratch_shapes=[pltpu.VMEM((tm, tn), F32)],
        compiler_params=_cparams("arbitrary", "arbitrary", "arbitrary"),
        name="mm_acc",
    )(a, w)


ATTN_ONES_ROWS = 16
ATTN_SLAB = 256


def _attn_body(qi_tab, kv_tab, slopes, q_ref, k_ref, vt_ref, pos_ref, lq1_ref, lk1_ref, lq2_ref, lk2_ref, g_ref,
               o_ref, q1_s, q2_s, m1_s, a1_s, m2_s, a2_s, jmi_s, *, tq, tk, lam_init):
    h = pl.program_id(0)
    t = pl.program_id(1)
    qi = qi_tab[t]
    kv = kv_tab[t]
    slope = slopes[h]
    n_sub = tq // tk
    dv = A_VDIM

    @pl.when(t == 0)
    def _():
        jmi_s[...] = lax.broadcasted_iota(I32, (tk, tq), 0) - lax.broadcasted_iota(I32, (tk, tq), 1)

    @pl.when(kv == 0)
    def _():
        q = q_ref[...]
        lane = lax.broadcasted_iota(I32, q.shape, 1)
        zero = jnp.zeros_like(q)
        sv = jnp.full(q.shape, slope, F32)
        s_hi = sv.astype(BF16).astype(F32)
        s_lo = sv - s_hi
        coef = jnp.where(lane == 0, s_hi * 16.0, jnp.where(lane == 1, s_lo * 16.0,
                         jnp.where(lane == 2, s_hi, jnp.where(lane == 3, s_lo, 0.0)))).astype(BF16)
        q1_s[...] = jnp.concatenate([jnp.where(lane < A_QKDIM, q, zero), coef], axis=1)
        q2_s[...] = jnp.concatenate([jnp.where(lane >= A_QKDIM, q, zero), coef], axis=1)
        for m_s, a_s in ((m1_s, a1_s), (m2_s, a2_s)):
            m_s[...] = jnp.full(m_s.shape, -jnp.inf, F32)
            a_s[...] = jnp.zeros(a_s.shape, F32)

    shift = kv * tk - qi * tq

    def step(key_off):
        k_aug = jnp.concatenate([k_ref[...], pos_ref[...]], axis=1)
        vt = vt_ref[...]
        c = slope * shift.astype(F32)
        chains = []
        for c0 in range(0, tq, ATTN_SLAB):
            if key_off is not None and c0 + ATTN_SLAB - 1 < key_off:
                continue
            masked = key_off is not None and c0 < key_off + tk - 1
            for q_s, m_s, a_s in ((q1_s, m1_s, a1_s), (q2_s, m2_s, a2_s)):
                chains.append((q_s, m_s, a_s, slice(c0, c0 + ATTN_SLAB), masked))
        scores = [_dot_t(k_aug, q_s[cols, :]) for q_s, _, _, cols, _ in chains]
        for s, (_, m_s, a_s, cols, masked) in zip(scores, chains):
            if masked:
                s = jnp.where(jmi_s[:, cols] <= -key_off, s, NEG_INF)
            m_prev = m_s[:, cols]
            m_new = jnp.maximum(m_prev, jnp.max(s, axis=0, keepdims=True) + c)
            p = jnp.exp2(s - (m_new - c))
            a_s[:, cols] = jnp.exp2(m_prev - m_new) * a_s[:, cols] + _dot(vt, p.astype(BF16))
            m_s[:, cols] = m_new

    @pl.when(kv < qi * n_sub)
    def _():
        step(None)

    for r in range(n_sub):
        @pl.when(kv == qi * n_sub + r)
        def _():
            step(r * tk)

    @pl.when(kv == (qi + 1) * n_sub - 1)
    def _():
        lam = (jnp.exp(jnp.sum(lq1_ref[...] * lk1_ref[...], axis=-1, keepdims=True))
               - jnp.exp(jnp.sum(lq2_ref[...] * lk2_ref[...], axis=-1, keepdims=True)) + lam_init)
        a1 = a1_s[...]
        a2 = a2_s[...]
        o = a1[:dv] / a1[dv:dv + 1] - lam * (a2[:dv] / a2[dv:dv + 1])
        ms = jnp.mean(o * o, axis=0, keepdims=True)
        on = o * lax.rsqrt(ms + RMS_EPS) * g_ref[...] * (1.0 - lam_init)
        o_ref[...] = on.T.astype(o_ref.dtype)


def _diff_attention(qkvu, v_t, lam_q1, lam_k1, lam_q2, lam_k2, subln_g, *, n_heads, lam_init, tq=2048, tk=512):
    s = qkvu.shape[0]
    n_sub = tq // tk
    pairs = [(i, j) for i in range(s // tq) for j in range((i + 1) * n_sub)]
    qi_tab = jnp.asarray([p[0] for p in pairs], I32)
    kv_tab = jnp.asarray([p[1] for p in pairs], I32)
    slopes = jnp.asarray([math.log2(math.e) * 2.0 ** (-8.0 * (i + 1) / n_heads) for i in range(n_heads)], F32)
    hd = A_VDIM
    hv = hd + ATTN_ONES_ROWS
    j = np.arange(tk)
    pos = np.zeros((tk, hd), np.float32)
    pos[:, 0] = pos[:, 1] = j // 16
    pos[:, 2] = pos[:, 3] = j % 16
    small = pl.BlockSpec((1, A_QKDIM), lambda h, t, qt, kt, sl: (0, 0))
    grid_spec = pltpu.PrefetchScalarGridSpec(
        num_scalar_prefetch=3,
        grid=(n_heads, len(pairs)),
        in_specs=[pl.BlockSpec((tq, hd), lambda h, t, qt, kt, sl: (qt[t], h)),
                  pl.BlockSpec((tk, hd), lambda h, t, qt, kt, sl: (kt[t], n_heads + h)),
                  pl.BlockSpec((hv, tk), lambda h, t, qt, kt, sl: (h, kt[t])),
                  pl.BlockSpec((tk, hd), lambda h, t, qt, kt, sl: (0, 0)),
                  small, small, small, small,
                  pl.BlockSpec((hd, 1), lambda h, t, qt, kt, sl: (0, 0))],
        out_specs=pl.BlockSpec((tq, hd), lambda h, t, qt, kt, sl: (qt[t], h)),
        scratch_shapes=[pltpu.VMEM((tq, 2 * hd), BF16), pltpu.VMEM((tq, 2 * hd), BF16),
                        pltpu.VMEM((1, tq), F32), pltpu.VMEM((hv, tq), F32),
                        pltpu.VMEM((1, tq), F32), pltpu.VMEM((hv, tq), F32),
                        pltpu.VMEM((tk, tq), I32)])
    return pl.pallas_call(
        functools.partial(_attn_body, tq=tq, tk=tk, lam_init=lam_init),
        grid_spec=grid_spec,
        out_shape=jax.ShapeDtypeStruct((s, n_heads * hd), BF16),
        compiler_params=_cparams("arbitrary", "arbitrary"),
        name="diff_attention",
    )(qi_tab, kv_tab, slopes, qkvu, qkvu, v_t, jnp.asarray(pos, BF16),
      lam_q1.reshape(1, -1), lam_k1.reshape(1, -1), lam_q2.reshape(1, -1), lam_k2.reshape(1, -1),
      subln_g.reshape(-1, 1))


def _attn_value_rows(v, n_heads):
    s = v.shape[0]
    vt = v.T.reshape(n_heads, A_VDIM, s)
    ones = jnp.ones((n_heads, ATTN_ONES_ROWS, s), v.dtype)
    return jnp.concatenate([vt, ones], axis=1).reshape(n_heads * (A_VDIM + ATTN_ONES_ROWS), s)


def _pool_body(ucur_ref, uprev_ref, w_ref, sc_ref, o_ref, *, t_blk):
    g = pl.program_id(0)
    i = pl.program_id(1)
    win = jnp.left_shift(2, g)
    row = lax.broadcasted_iota(I32, (t_blk, t_blk), 0)
    col = lax.broadcasted_iota(I32, (t_blk, t_blk), 1)
    d = row - col
    band = jnp.where(jnp.logical_and(d >= 0, d < win), 1.0, 0.0).astype(BF16)
    rowp = lax.broadcasted_iota(I32, (t_blk, POOL_HALO), 0)
    colp = lax.broadcasted_iota(I32, (t_blk, POOL_HALO), 1)
    dp = rowp + POOL_HALO - colp
    bandp = jnp.where(jnp.logical_and(dp < win, i > 0), 1.0, 0.0).astype(BF16)
    u = ucur_ref[...]
    usum = _dot(band, u) + _dot(bandp, uprev_ref[...])
    tpos = i * t_blk + lax.broadcasted_iota(I32, (t_blk, 1), 0)
    cnt = jnp.minimum(tpos + 1, win).astype(F32)
    dev = usum / cnt - u.astype(F32)
    y = _dot(dev.astype(BF16), w_ref[...]) * sc_ref[...]
    o_ref[...] = y.astype(o_ref.dtype)


def _multiscale_pool(qkvu, pool_w, pool_scale, *, col_off, t_blk=256):
    s = qkvu.shape[0]
    ng, gd, _ = pool_w.shape
    goff = col_off // gd
    hb = t_blk // POOL_HALO
    return pl.pallas_call(
        functools.partial(_pool_body, t_blk=t_blk),
        grid=(ng, s // t_blk),
        in_specs=[pl.BlockSpec((t_blk, gd), lambda g, i: (i, goff + g)),
                  pl.BlockSpec((POOL_HALO, gd), lambda g, i: (jnp.maximum(i * hb - 1, 0), goff + g)),
                  pl.BlockSpec((None, gd, gd), lambda g, i: (g, 0, 0)),
                  pl.BlockSpec((1, gd), lambda g, i: (0, g))],
        out_specs=pl.BlockSpec((t_blk, gd), lambda g, i: (i, g)),
        out_shape=jax.ShapeDtypeStruct((s, ng * gd), BF16),
        compiler_params=_cparams("arbitrary", "arbitrary"),
        name="multiscale_pool",
    )(qkvu, qkvu, pool_w, pool_scale.reshape(1, -1))


def _hgrn_tables():
    c = HGRN_CHUNK
    idx = np.arange(c)
    mats, masks = [], []
    h = c // 2
    while h >= 1:
        upper = (idx % (2 * h)) >= h
        e = idx - (idx % (2 * h)) + h - 1
        u = idx[None, :]
        pq = (upper[:, None] & (u > e[:, None]) & (u <= idx[:, None]))
        pk = ((~upper)[:, None] & (u > idx[:, None]) & (u <= e[:, None]))
        mats += [pq, pk]
        masks.append(upper[:, None] & (~upper)[None, :] & ((idx[:, None] // (2 * h)) == (idx[None, :] // (2 * h))))
        h //= 2
    masks.append(idx[:, None] == idx[None, :])
    mats.append(idx[None, :] <= idx[:, None])
    mats.append(idx[None, :] > idx[:, None])
    return (np.concatenate(mats, axis=0).astype(np.float32),
            np.stack(masks, axis=0).astype(np.float32))


def _hgrn_body(q_ref, f_ref, v_ref, gs_ref, p_ref, mask_ref, gn_ref, o_ref, st_ref, *, t_blk, n_grp):
    c = HGRN_CHUNK
    n_lvl = mask_ref.shape[0] - 1
    heads = range(n_grp)

    @pl.when(pl.program_id(1) == 0)
    def _():
        st_ref[...] = jnp.zeros(st_ref.shape, F32)

    def col(x, hg):
        return x[:, hg * C_KDIM:(hg + 1) * C_KDIM]

    def stack(parts):
        return jnp.concatenate(parts, axis=0)

    def block_diag(parts):
        zero = jnp.zeros_like(parts[0])
        return stack([jnp.concatenate([parts[hg] if j == hg else zero for j in heads], axis=1) for hg in heads])

    def chunk(ci, carry):
        rows = pl.ds(pl.multiple_of(ci * c, c), c)
        f = f_ref[rows, :]
        lf = jnp.log(f)
        kk = 1.0 - f
        hi = lf.astype(BF16)
        r1 = lf - hi.astype(F32)
        mid = r1.astype(BF16)
        lo = (r1 - mid.astype(F32)).astype(BF16)
        ed = jnp.exp(_dot(p_ref[...], stack([hi, mid, lo])))
        e = [col(ed, hg) for hg in heads]
        q = q_ref[rows, :].astype(F32)
        qh = [col(q, hg) for hg in heads]
        kh = [col(kk, hg) for hg in heads]
        attn = mask_ref[n_lvl] * _dot_t(stack([x.astype(BF16) for x in qh]), stack([x.astype(BF16) for x in kh]))
        for lv in range(n_lvl):
            qs = stack([(qh[hg] * e[hg][2 * lv * c:(2 * lv + 1) * c]).astype(BF16) for hg in heads])
            ks = stack([(kh[hg] * e[hg][(2 * lv + 1) * c:(2 * lv + 2) * c]).astype(BF16) for hg in heads])
            attn = attn + mask_ref[lv] * _dot_t(qs, ks)
        eb = [e[hg][2 * n_lvl * c:(2 * n_lvl + 1) * c] for hg in heads]
        el = [e[hg][(2 * n_lvl + 1) * c:(2 * n_lvl + 2) * c] for hg in heads]
        v = v_ref[rows, :]
        v_st = stack([col(v, hg) for hg in heads])
        st = st_ref[...]
        q_bd = block_diag([(qh[hg] * eb[hg]).astype(BF16) for hg in heads])
        o = _dot(attn.astype(BF16), v_st) + _dot_t(q_bd, st.astype(BF16))
        k_bd = block_diag([(kh[hg] * el[hg]).astype(BF16) for hg in heads])
        decay = jnp.concatenate([eb[hg][c - 1:c, :] for hg in heads], axis=1)
        st_ref[...] = st * decay + lax.dot_general(v_st, k_bd, (((0,), (0,)), ((), ())),
                                                   preferred_element_type=F32)
        ms = jnp.mean(o * o, axis=-1, keepdims=True)
        on = o * lax.rsqrt(ms + RMS_EPS) * gn_ref[...]
        gs = gs_ref[rows, :].astype(F32)
        for hg in heads:
            o_ref[rows, hg * C_VDIM:(hg + 1) * C_VDIM] = (on[hg * c:(hg + 1) * c] * col(gs, hg)).astype(o_ref.dtype)
        return carry

    lax.fori_loop(0, t_blk // c, chunk, 0)


def _hgrn2(qs, forget, v, gs, gnorm_g, *, t_blk=512, n_grp=4):
    s, d = qs.shape
    n_heads = d // C_KDIM
    p_np, mask_np = _hgrn_tables()
    p_mat = jnp.asarray(np.concatenate([p_np] * 3, axis=1), BF16)
    masks = jnp.asarray(np.stack([np.kron(np.eye(n_grp, dtype=np.float32), m) for m in mask_np]), F32)
    blk = lambda: pl.BlockSpec((t_blk, n_grp * C_KDIM), lambda h, i: (i, h))
    return pl.pallas_call(
        functools.partial(_hgrn_body, t_blk=t_blk, n_grp=n_grp),
        grid=(n_heads // n_grp, s // t_blk),
        in_specs=[blk(), blk(), blk(), blk(),
                  pl.BlockSpec(p_mat.shape, lambda h, i: (0, 0)),
                  pl.BlockSpec(masks.shape, lambda h, i: (0, 0, 0)),
                  pl.BlockSpec((1, C_VDIM), lambda h, i: (0, 0))],
        out_specs=blk(),
        out_shape=jax.ShapeDtypeStruct((s, d), BF16),
        scratch_shapes=[pltpu.VMEM((C_VDIM, n_grp * C_KDIM), F32)],
        compiler_params=_cparams("arbitrary", "arbitrary"),
        name="hgrn2",
    )(qs, forget, v, gs, p_mat, masks, gnorm_g.reshape(1, -1))


def _router_body(x_ref, sc_ref, sh_ref, whi_ref, wlo_ref, info_ref, cnt_ref, carry_s, *, tm):
    @pl.when(pl.program_id(0) == 0)
    def _():
        carry_s[...] = jnp.zeros(carry_s.shape, F32)

    hmod = x_ref[...] * (1.0 + sc_ref[...]) + sh_ref[...]
    hh = hmod.astype(BF16)
    hl = (hmod - hh.astype(F32)).astype(BF16)
    whi = whi_ref[...]
    logits = _dot(hh, whi) + _dot(hl, whi) + _dot(hh, wlo_ref[...])
    lane = lax.broadcasted_iota(I32, (tm, LANES), 1)
    logits = jnp.where(lane < N_EXPERTS, logits, -jnp.inf)
    m1 = jnp.max(logits, axis=-1, keepdims=True)
    e1 = jnp.min(jnp.where(logits == m1, lane, LANES), axis=-1, keepdims=True)
    rest = jnp.where(lane == e1, -jnp.inf, logits)
    m2 = jnp.max(rest, axis=-1, keepdims=True)
    e2 = jnp.min(jnp.where(rest == m2, lane, LANES), axis=-1, keepdims=True)
    ex = jnp.exp(m2 - m1)
    g1 = 1.0 / (1.0 + ex)
    g2 = ex / (1.0 + ex)
    onehot = jnp.where(jnp.logical_or(lane == e1, lane == e2), 1.0, 0.0)
    row = lax.broadcasted_iota(I32, (tm, tm), 0)
    col = lax.broadcasted_iota(I32, (tm, tm), 1)
    before = jnp.where(row > col, 1.0, 0.0).astype(BF16)
    cum = _dot(before, onehot.astype(BF16)) + carry_s[...]
    r1 = jnp.sum(jnp.where(lane == e1, cum, 0.0), axis=-1, keepdims=True)
    r2 = jnp.sum(jnp.where(lane == e2, cum, 0.0), axis=-1, keepdims=True)
    carry_s[...] = carry_s[...] + jnp.sum(onehot, axis=0, keepdims=True)
    cnt_ref[...] = carry_s[...]
    info = jnp.where(lane == 0, e1.astype(F32), 0.0)
    info = jnp.where(lane == 1, e2.astype(F32), info)
    info = jnp.where(lane == 2, g1, info)
    info = jnp.where(lane == 3, g2, info)
    info = jnp.where(lane == 4, r1, info)
    info = jnp.where(lane == 5, r2, info)
    info_ref[...] = info


def _router(x, scale, shift, router_w, *, tm=256):
    s, d = x.shape
    wpad = jnp.zeros((d, LANES), F32).at[:, :N_EXPERTS].set(router_w)
    whi = wpad.astype(BF16)
    wlo = (wpad - whi.astype(F32)).astype(BF16)
    vec = pl.BlockSpec((1, d), lambda i: (0, 0))
    wspec = pl.BlockSpec((d, LANES), lambda i: (0, 0))
    return pl.pallas_call(
        functools.partial(_router_body, tm=tm),
        grid=(s // tm,),
        in_specs=[pl.BlockSpec((tm, d), lambda i: (i, 0)), vec, vec, wspec, wspec],
        out_specs=[pl.BlockSpec((tm, LANES), lambda i: (i, 0)),
                   pl.BlockSpec((1, LANES), lambda i: (0, 0))],
        out_shape=[jax.ShapeDtypeStruct((s, LANES), F32), jax.ShapeDtypeStruct((1, LANES), F32)],
        scratch_shapes=[pltpu.VMEM((1, LANES), F32)],
        compiler_params=_cparams("arbitrary"),
        name="router",
    )(x, scale.reshape(1, d), shift.reshape(1, d), whi, wlo)


def _dispatch_copies(s1_ref, s2_ref, h_ref, out_ref, sem, base, r):
    t = base + r
    src = h_ref.at[pl.ds(r, 1)]
    return (pltpu.make_async_copy(src, out_ref.at[pl.ds(s1_ref[t], 1)], sem.at[0]),
            pltpu.make_async_copy(src, out_ref.at[pl.ds(s2_ref[t], 1)], sem.at[1]))


def _dispatch_body(s1_ref, s2_ref, h_ref, init_ref, out_ref, sem, *, tb):
    del init_ref
    base = pl.program_id(0) * tb

    def issue(r, carry):
        for prio, cp in enumerate(_dispatch_copies(s1_ref, s2_ref, h_ref, out_ref, sem, base, r)):
            cp.start(priority=prio)
        return carry

    def drain(r, carry):
        for cp in _dispatch_copies(s1_ref, s2_ref, h_ref, out_ref, sem, base, r):
            cp.wait()
        return carry

    lax.fori_loop(0, tb, issue, 0, unroll=DMA_ISSUE_UNROLL)
    lax.fori_loop(0, tb, drain, 0, unroll=DMA_ISSUE_UNROLL)


def _dispatch(h_words, slot1, slot2, n_rows, *, tb=256):
    s, dw = h_words.shape
    init = jnp.zeros((n_rows, dw), h_words.dtype)
    grid_spec = pltpu.PrefetchScalarGridSpec(
        num_scalar_prefetch=2,
        grid=(s // tb,),
        in_specs=[pl.BlockSpec((tb, dw), lambda i, a, b: (i, 0)),
                  pl.BlockSpec(memory_space=pl.ANY)],
        out_specs=pl.BlockSpec(memory_space=pl.ANY),
        scratch_shapes=[pltpu.SemaphoreType.DMA((2,))])
    return pl.pallas_call(
        functools.partial(_dispatch_body, tb=tb),
        grid_spec=grid_spec,
        out_shape=jax.ShapeDtypeStruct((n_rows, dw), h_words.dtype),
        input_output_aliases={3: 0},
        compiler_params=_cparams("arbitrary"),
        name="moe_dispatch",
    )(slot1, slot2, h_words, init)


def _expert_rows_body(off_ref, cnt_ref, *refs, swiglu, tr, tn, n_rows):
    if swiglu:
        w_hbm, x_hbm, o_hbm, wbuf, wa_s, wb_s, xbuf, obuf, wsem, xsem, osem = refs
        w_bf16 = (wa_s, wb_s)
    else:
        w_hbm, x_hbm, o_hbm, wbuf, w_s, xbuf, obuf, wsem, xsem, osem = refs
        w_bf16 = (w_s,)
    j = pl.program_id(0)
    e = pl.program_id(1)
    n_exp = pl.num_programs(1)
    step = j * n_exp + e
    wslot = lax.rem(step, 2)
    n = (cnt_ref[e] + (tr - 1)) // tr
    col0 = pl.multiple_of(j * tn, tn)
    half = w_hbm.shape[2] // len(w_bf16)

    def row(t):
        return pl.multiple_of(off_ref[e] + t * tr, tr)

    def x_copy(row0, slot):
        return pltpu.make_async_copy(x_hbm.at[pl.ds(row0, tr)], xbuf.at[slot], xsem.at[slot])

    def o_copy(row0, slot):
        return pltpu.make_async_copy(obuf.at[slot], o_hbm.at[pl.ds(row0, tr), pl.ds(col0, tn)], osem.at[slot])

    def w_copies(jj, ee, slot):
        return [pltpu.make_async_copy(w_hbm.at[ee, :, pl.ds(pl.multiple_of(part * half + jj * tn, tn), tn)],
                                      wbuf.at[slot, part], wsem.at[slot, part])
                for part in range(len(w_bf16))]

    @pl.when(n > 0)
    def _():
        x_copy(row(0), 0).start(priority=ROW_TILE_DMA_PRIORITY)

    @pl.when(step == 0)
    def _():
        for cp in w_copies(j, e, wslot):
            cp.start()

    @pl.when(step + 1 < pl.num_programs(0) * n_exp)
    def _():
        for cp in w_copies((step + 1) // n_exp, lax.rem(step + 1, n_exp), 1 - wslot):
            cp.start()

    for part, (cp, dst) in enumerate(zip(w_copies(j, e, wslot), w_bf16)):
        cp.wait()
        dst[...] = wbuf[wslot, part].astype(BF16)

    def tile(t, carry):
        slot = lax.rem(t, 2)

        @pl.when(t + 1 < n)
        def _():
            x_copy(row(t + 1), 1 - slot).start(priority=ROW_TILE_DMA_PRIORITY)

        x_copy(row(t), slot).wait()

        @pl.when(t >= 2)
        def _():
            o_copy(row(t - 2), slot).wait()

        if swiglu:
            x = _unpack_bf16_pairs(xbuf[slot])
            res = _silu(_dot(x, wa_s[...])) * _dot(x, wb_s[...])
        else:
            res = _dot(xbuf[slot], w_s[...])
        obuf[slot] = res.astype(obuf.dtype)
        o_copy(row(t), slot).start()
        return carry

    lax.fori_loop(0, n, tile, 0)

    @pl.when(n >= 2)
    def _():
        o_copy(row(n - 2), lax.rem(n, 2)).wait()

    @pl.when(n >= 1)
    def _():
        o_copy(row(n - 1), lax.rem(n - 1, 2)).wait()

    @pl.when(e == pl.num_programs(1) - 1)
    def _():
        obuf[0] = jnp.zeros(obuf.shape[1:], obuf.dtype)
        end = off_ref[e] + n * tr

        def fill(t, carry):
            cp = o_copy(pl.multiple_of(end + t * tr, tr), 0)
            cp.start()
            cp.wait()
            return carry

        lax.fori_loop(0, (n_rows - end) // tr, fill, 0)


def _expert_in(xs_words, w_in, offs, counts, *, tr, tn=512):
    p, kw = xs_words.shape
    k = 2 * kw
    n_exp = w_in.shape[0]
    half = w_in.shape[2] // 2
    grid_spec = pltpu.PrefetchScalarGridSpec(
        num_scalar_prefetch=2,
        grid=(half // tn, n_exp),
        in_specs=[pl.BlockSpec(memory_space=pl.ANY), pl.BlockSpec(memory_space=pl.ANY)],
        out_specs=pl.BlockSpec(memory_space=pl.ANY),
        scratch_shapes=[pltpu.VMEM((2, 2, k, tn), w_in.dtype),
                        pltpu.VMEM((k, tn), BF16), pltpu.VMEM((k, tn), BF16),
                        pltpu.VMEM((2, tr, kw), xs_words.dtype), pltpu.VMEM((2, tr, tn), BF16),
                        pltpu.SemaphoreType.DMA((2, 2)), pltpu.SemaphoreType.DMA((2,)),
                        pltpu.SemaphoreType.DMA((2,))])
    return pl.pallas_call(
        functools.partial(_expert_rows_body, swiglu=True, tr=tr, tn=tn, n_rows=p),
        grid_spec=grid_spec,
        out_shape=jax.ShapeDtypeStruct((p, half), BF16),
        compiler_params=_cparams("arbitrary", "arbitrary"),
        name="expert_in",
    )(offs, counts, w_in, xs_words)


def _expert_out(gs, w_out, offs, counts, *, tr, tn=1024):
    p, k = gs.shape
    n_exp = w_out.shape[0]
    n = w_out.shape[2]
    grid_spec = pltpu.PrefetchScalarGridSpec(
        num_scalar_prefetch=2,
        grid=(n // tn, n_exp),
        in_specs=[pl.BlockSpec(memory_space=pl.ANY), pl.BlockSpec(memory_space=pl.ANY)],
        out_specs=pl.BlockSpec(memory_space=pl.ANY),
        scratch_shapes=[pltpu.VMEM((2, 1, k, tn), w_out.dtype),
                        pltpu.VMEM((k, tn), BF16),
                        pltpu.VMEM((2, tr, k), gs.dtype), pltpu.VMEM((2, tr, tn), F32),
                        pltpu.SemaphoreType.DMA((2, 1)), pltpu.SemaphoreType.DMA((2,)),
                        pltpu.SemaphoreType.DMA((2,))])
    return pl.pallas_call(
        functools.partial(_expert_rows_body, swiglu=False, tr=tr, tn=tn, n_rows=p),
        grid_spec=grid_spec,
        out_shape=jax.ShapeDtypeStruct((p, n), F32),
        compiler_params=_cparams("arbitrary", "arbitrary"),
        name="expert_out",
    )(offs, counts, w_out, gs)


def _combine_copies(s1_ref, s2_ref, ys_ref, buf, sem, blk, slot, r, tb):
    t = blk * tb + r
    return (pltpu.make_async_copy(ys_ref.at[pl.ds(s1_ref[t], 1)], buf.at[slot, 0, pl.ds(r, 1)], sem.at[slot, 0]),
            pltpu.make_async_copy(ys_ref.at[pl.ds(s2_ref[t], 1)], buf.at[slot, 1, pl.ds(r, 1)], sem.at[slot, 1]))


def _combine_body(s1_ref, s2_ref, x_ref, info_ref, ys_ref, gate_ref, g_ref, b_ref, o_ref, buf, sem,
                  *, tb, alpha):
    i = pl.program_id(0)
    slot = lax.rem(i, 2)

    def gather(blk, slt):
        def issue(r, carry):
            for prio, cp in enumerate(_combine_copies(s1_ref, s2_ref, ys_ref, buf, sem, blk, slt, r, tb)):
                cp.start(priority=prio)
            return carry
        lax.fori_loop(0, tb, issue, 0, unroll=DMA_ISSUE_UNROLL)

    @pl.when(i == 0)
    def _():
        gather(i, slot)

    @pl.when(i + 1 < pl.num_programs(0))
    def _():
        gather(i + 1, 1 - slot)

    def drain(r, carry):
        for cp in _combine_copies(s1_ref, s2_ref, ys_ref, buf, sem, i, slot, r, tb):
            cp.wait()
        return carry

    lax.fori_loop(0, tb, drain, 0, unroll=DMA_ISSUE_UNROLL)
    info = info_ref[...]
    y = info[:, 2:3] * buf[slot, 0] + info[:, 3:4] * buf[slot, 1]
    r = alpha * x_ref[...] + (1.0 + gate_ref[...]) * y
    o_ref[...] = _layer_norm_rows(r, g_ref[...], b_ref[...])


def _combine_postnorm(x, info, ys, slot1, slot2, gate, g, b, *, alpha, tb=128):
    s, d = x.shape
    vec = pl.BlockSpec((1, d), lambda i, a, c: (0, 0))
    grid_spec = pltpu.PrefetchScalarGridSpec(
        num_scalar_prefetch=2,
        grid=(s // tb,),
        in_specs=[pl.BlockSpec((tb, d), lambda i, a, c: (i, 0)),
                  pl.BlockSpec((tb, LANES), lambda i, a, c: (i, 0)),
                  pl.BlockSpec(memory_space=pl.ANY),
                  vec, vec, vec],
        out_specs=pl.BlockSpec((tb, d), lambda i, a, c: (i, 0)),
        scratch_shapes=[pltpu.VMEM((2, 2, tb, d), F32), pltpu.SemaphoreType.DMA((2, 2))])
    return pl.pallas_call(
        functools.partial(_combine_body, tb=tb, alpha=alpha),
        grid_spec=grid_spec,
        out_shape=jax.ShapeDtypeStruct((s, d), F32),
        compiler_params=_cparams("arbitrary"),
        name="moe_combine_postnorm",
    )(slot1, slot2, x, info, ys, gate.reshape(1, d), g.reshape(1, d), b.reshape(1, d))


def _moe_plan(info, counts, *, tr):
    e1 = info[:, 0].astype(I32)
    e2 = info[:, 1].astype(I32)
    r1 = info[:, 4].astype(I32)
    r2 = info[:, 5].astype(I32)
    cnt = counts[0, :N_EXPERTS].astype(I32)
    padded = (cnt + tr - 1) // tr * tr
    row_end = jnp.cumsum(padded)
    row_off = row_end - padded
    slot1 = row_off[e1] + r1
    slot2 = row_off[e2] + r2
    offs = jnp.concatenate([row_off, row_end[-1:]]).astype(I32)
    return slot1, slot2, offs, cnt


def kernel(x, c, ada_w, ada_b, ln_g, ln_b, even_w_in, lam_q1, lam_k1, lam_q2, lam_k2, subln_g, pool_w,
           pool_scale, even_w_out, ffn_w_in, ffn_w_out, odd_w_in, lb_raw, gnorm_g, odd_w_out, router_w,
           exp_w_in, exp_w_out):
    _, seq, d = x.shape
    depth = ada_w.shape[0]
    alpha = (2 * depth) ** 0.25
    a_width = d // 2
    a_heads = a_width // A_VDIM
    x2 = x.reshape(seq, d)

    lb_all = jnp.cumsum(jax.nn.softmax(lb_raw.astype(F32), axis=0), axis=0)
    lb_all = lb_all - lb_all[0]
    mod = _ada_mod(c, ada_w, ada_b)

    def mod_parts(l):
        return [mod[l, i * d:(i + 1) * d] for i in range(6)]

    sh1, sc1, g1, sh2, sc2, g2 = mod_parts(0)
    h = _modulate(x2, sc1, sh1)
    qkvu = _proj([h], even_w_in[0], col_off=0, n_cols=3 * a_width + (d - a_width), out_dtype=BF16,
                 epilogue="qscale", q_cols=a_width, q_scale=A_QKDIM ** -0.5 * math.log2(math.e),
                 name="even_in_proj")
    lam_init = 0.8 - 0.6 * math.exp(-0.3 * 0)
    v_t = _attn_value_rows(qkvu[:, 2 * a_width:3 * a_width], a_heads)
    o_a = _diff_attention(qkvu, v_t, lam_q1[0], lam_k1[0], lam_q2[0], lam_k2[0], subln_g[0],
                          n_heads=a_heads, lam_init=lam_init)
    o_b = _multiscale_pool(qkvu, pool_w[0].astype(BF16), pool_scale[0], col_off=3 * a_width)
    y = _proj([o_a, o_b], even_w_out[0], col_off=0, n_cols=d, out_dtype=BF16, name="even_out_proj")
    x2, h = _postnorm(x2, y, g1, ln_g[0, 0], ln_b[0, 0], sc2, sh2, alpha=alpha)

    ff = ffn_w_out.shape[1]
    gact = _swiglu_in(h, ffn_w_in[0])
    y = _mm_acc(gact, ffn_w_out[0], tk=ff // 2, out_dtype=BF16)
    sh1, sc1, g1n, sh2n, sc2n, g2n = mod_parts(1)
    x2, h = _postnorm(x2, y, g2, ln_g[0, 1], ln_b[0, 1], sc1, sh1, alpha=alpha)

    w_odd = odd_w_in[0]
    qs = _proj([h], w_odd, col_off=0, n_cols=d, out_dtype=BF16, epilogue="silu", name="odd_in_q")
    fg = _proj([h], w_odd, col_off=d, n_cols=d, out_dtype=F32, epilogue="forget", lb=lb_all[1], name="odd_in_f")
    vi = _proj([h], w_odd, col_off=2 * d, n_cols=d, out_dtype=BF16, name="odd_in_i")
    gs = _proj([h], w_odd, col_off=3 * d, n_cols=d, out_dtype=BF16, epilogue="silu", name="odd_in_g")
    o_c = _hgrn2(qs, fg, vi, gs, gnorm_g[0])
    y = _proj([o_c], odd_w_out[0], col_off=0, n_cols=d, out_dtype=BF16, name="odd_out_proj")
    x2, h_words = _postnorm(x2, y, g1n, ln_g[1, 0], ln_b[1, 0], sc2n, sh2n, alpha=alpha, pack=True)

    tr = 512
    n_rows = 2 * seq + N_EXPERTS * tr
    info, counts = _router(x2, sc2n, sh2n, router_w[0])
    slot1, slot2, offs, cnt = _moe_plan(info, counts, tr=tr)
    xs_words = _dispatch(h_words, slot1, slot2, n_rows)
    gexp = _expert_in(xs_words, exp_w_in[0], offs, cnt, tr=tr)
    ys = _expert_out(gexp, exp_w_out[0], offs, cnt, tr=tr)
    out = _combine_postnorm(x2, info, ys, slot1, slot2, g2n, ln_g[1, 1], ln_b[1, 1], alpha=alpha)
    return out.reshape(x.shape)
```

```python
import functools
import math

import numpy as np
import jax
import jax.numpy as jnp
from jax import lax
from jax.experimental import pallas as pl
from jax.experimental.pallas import tpu as pltpu

F32 = jnp.float32
BF16 = jnp.bfloat16
I32 = jnp.int32

A_VDIM = 128
A_QKDIM = 64
POOL_WINDOWS = (2, 4, 8, 16)
POOL_HALO = 16
C_KDIM = 128
C_VDIM = 128
HGRN_CHUNK = 64
N_EXPERTS = 8
LN_EPS = 1e-5
RMS_EPS = 1e-6
NEG_INF = -1e30

LANES = 128
VMEM_LIMIT_BYTES = 56 * 1024 * 1024
DMA_ISSUE_UNROLL = 8


def _cparams(*sem):
    return pltpu.CompilerParams(dimension_semantics=sem, vmem_limit_bytes=VMEM_LIMIT_BYTES)


def _silu(x):
    return x * (1.0 / (1.0 + jnp.exp(-x)))


def _dot(a, b):
    return jnp.dot(a, b, preferred_element_type=F32)


def _dot_t(a, b):
    return lax.dot_general(a, b, (((1,), (1,)), ((), ())), preferred_element_type=F32)


def _pack_bf16_pairs(h):
    half = h.shape[1] // 2
    bits = pltpu.bitcast(h.astype(BF16).astype(F32), I32)
    return jnp.bitwise_or(bits[:, :half], lax.shift_right_logical(bits[:, half:], 16))


def _unpack_bf16_pairs(w):
    hi = pltpu.bitcast(jnp.bitwise_and(w, -65536), F32).astype(BF16)
    lo = pltpu.bitcast(lax.shift_left(w, 16), F32).astype(BF16)
    return jnp.concatenate([hi, lo], axis=1)


def _ada_body(c_ref, w_ref, b_ref, o_ref):
    ca = _silu(c_ref[...]).astype(BF16)
    o_ref[...] = _dot(ca, w_ref[...].astype(BF16)) + b_ref[...]


def _ada_mod(c, ada_w, ada_b, *, tn=1024):
    depth, d, n = ada_w.shape
    c8 = jnp.broadcast_to(c.astype(F32), (8, d))
    out = pl.pallas_call(
        _ada_body,
        grid=(depth, n // tn),
        in_specs=[pl.BlockSpec((8, d), lambda l, j: (0, 0)),
                  pl.BlockSpec((None, d, tn), lambda l, j: (l, 0, j)),
                  pl.BlockSpec((None, 1, tn), lambda l, j: (l, 0, j))],
        out_specs=pl.BlockSpec((None, 8, tn), lambda l, j: (l, 0, j)),
        out_shape=jax.ShapeDtypeStruct((depth, 8, n), F32),
        compiler_params=_cparams("arbitrary", "arbitrary"),
        name="ada_mod",
    )(c8, ada_w, ada_b.reshape(depth, 1, n))
    return out[:, 0, :]


def _modulate_body(x_ref, sc_ref, sh_ref, o_ref):
    o_ref[...] = (x_ref[...] * (1.0 + sc_ref[...]) + sh_ref[...]).astype(o_ref.dtype)


def _modulate(x, scale, shift, *, tm=512):
    s, d = x.shape
    vec = pl.BlockSpec((1, d), lambda i: (0, 0))
    return pl.pallas_call(
        _modulate_body,
        grid=(s // tm,),
        in_specs=[pl.BlockSpec((tm, d), lambda i: (i, 0)), vec, vec],
        out_specs=pl.BlockSpec((tm, d), lambda i: (i, 0)),
        out_shape=jax.ShapeDtypeStruct((s, d), BF16),
        compiler_params=_cparams("arbitrary"),
        name="modulate",
    )(x, scale.reshape(1, d), shift.reshape(1, d))


def _layer_norm_rows(r, g, b):
    mu = jnp.mean(r, axis=-1, keepdims=True)
    rc = r - mu
    var = jnp.mean(rc * rc, axis=-1, keepdims=True)
    return rc * lax.rsqrt(var + LN_EPS) * g + b


def _postnorm_body(x_ref, y_ref, gate_ref, g_ref, b_ref, sc_ref, sh_ref, xo_ref, ho_ref, *, alpha, pack):
    r = alpha * x_ref[...] + (1.0 + gate_ref[...]) * y_ref[...].astype(F32)
    xn = _layer_norm_rows(r, g_ref[...], b_ref[...])
    xo_ref[...] = xn
    hn = xn * (1.0 + sc_ref[...]) + sh_ref[...]
    ho_ref[...] = _pack_bf16_pairs(hn) if pack else hn.astype(ho_ref.dtype)


def _postnorm(x, y, gate, g, b, nscale, nshift, *, alpha, pack=False, tm=256):
    s, d = x.shape
    row = pl.BlockSpec((tm, d), lambda i: (i, 0))
    vec = pl.BlockSpec((1, d), lambda i: (0, 0))
    hd, hdt = (d // 2, I32) if pack else (d, BF16)
    return pl.pallas_call(
        functools.partial(_postnorm_body, alpha=alpha, pack=pack),
        grid=(s // tm,),
        in_specs=[row, row, vec, vec, vec, vec, vec],
        out_specs=[row, pl.BlockSpec((tm, hd), lambda i: (i, 0))],
        out_shape=[jax.ShapeDtypeStruct((s, d), F32), jax.ShapeDtypeStruct((s, hd), hdt)],
        compiler_params=_cparams("arbitrary"),
        name="postnorm",
    )(x, y, gate.reshape(1, d), g.reshape(1, d), b.reshape(1, d), nscale.reshape(1, d), nshift.reshape(1, d))


def _proj_body(*refs, n_a, epilogue, tn, q_cols, q_scale):
    a_refs, w_ref, extra = refs[:n_a], refs[n_a], refs[n_a + 1:-2]
    o_ref, wb_s = refs[-2], refs[-1]

    @pl.when(pl.program_id(1) == 0)
    def _():
        wb_s[...] = w_ref[...].astype(BF16)

    acc, off = None, 0
    for a_ref in a_refs:
        kp = a_ref.shape[1]
        part = _dot(a_ref[...], wb_s[off:off + kp, :])
        acc = part if acc is None else acc + part
        off += kp
    if epilogue == "silu":
        acc = _silu(acc)
    elif epilogue == "forget":
        lb = extra[0][...]
        acc = lb + (1.0 - lb) * (1.0 / (1.0 + jnp.exp(-acc)))
    elif epilogue == "qscale":
        acc = acc * jnp.where(pl.program_id(0) * tn < q_cols, q_scale, 1.0)
    o_ref[...] = acc.astype(o_ref.dtype)


def _proj(a_parts, w, *, col_off, n_cols, out_dtype, epilogue="id", lb=None, q_cols=0, q_scale=1.0,
          tm=1024, tn=512, name="proj"):
    m = a_parts[0].shape[0]
    k = w.shape[0]
    joff = col_off // tn
    in_specs = [pl.BlockSpec((tm, a.shape[1]), lambda j, i: (i, 0)) for a in a_parts]
    in_specs.append(pl.BlockSpec((k, tn), lambda j, i: (0, j + joff)))
    args = list(a_parts) + [w]
    if epilogue == "forget":
        in_specs.append(pl.BlockSpec((1, tn), lambda j, i: (0, j)))
        args.append(lb.reshape(1, n_cols))
    return pl.pallas_call(
        functools.partial(_proj_body, n_a=len(a_parts), epilogue=epilogue, tn=tn, q_cols=q_cols, q_scale=q_scale),
        grid=(n_cols // tn, m // tm),
        in_specs=in_specs,
        out_specs=pl.BlockSpec((tm, tn), lambda j, i: (i, j)),
        out_shape=jax.ShapeDtypeStruct((m, n_cols), out_dtype),
        scratch_shapes=[pltpu.VMEM((k, tn), BF16)],
        compiler_params=_cparams("arbitrary", "arbitrary"),
        name=name,
    )(*args)


def _swiglu_in_body(a_ref, wa_ref, wb_ref, o_ref, wa_s, wb_s):
    @pl.when(pl.program_id(1) == 0)
    def _():
        wa_s[...] = wa_ref[...].astype(BF16)
        wb_s[...] = wb_ref[...].astype(BF16)

    a = a_ref[...]
    o_ref[...] = (_silu(_dot(a, wa_s[...])) * _dot(a, wb_s[...])).astype(o_ref.dtype)


def _swiglu_in(a, w, *, tm=1024, tn=256):
    m, k = a.shape
    half = w.shape[1] // 2
    hoff = half // tn
    return pl.pallas_call(
        _swiglu_in_body,
        grid=(half // tn, m // tm),
        in_specs=[pl.BlockSpec((tm, k), lambda j, i: (i, 0)),
                  pl.BlockSpec((k, tn), lambda j, i: (0, j)),
                  pl.BlockSpec((k, tn), lambda j, i: (0, j + hoff))],
        out_specs=pl.BlockSpec((tm, tn), lambda j, i: (i, j)),
        out_shape=jax.ShapeDtypeStruct((m, half), BF16),
        scratch_shapes=[pltpu.VMEM((k, tn), BF16), pltpu.VMEM((k, tn), BF16)],
        compiler_params=_cparams("arbitrary", "arbitrary"),
        name="swiglu_in",
    )(a, w, w)


def _mm_acc_body(a_ref, w_ref, o_ref, acc_ref, *, nk):
    kk = pl.program_id(2)
    part = _dot(a_ref[...], w_ref[...].astype(BF16))

    @pl.when(kk == 0)
    def _():
        acc_ref[...] = part

    @pl.when(kk > 0)
    def _():
        acc_ref[...] += part

    @pl.when(kk == nk - 1)
    def _():
        o_ref[...] = acc_ref[...].astype(o_ref.dtype)


def _mm_acc(a, w, *, tm=1024, tn=512, tk, out_dtype=F32):
    m, k = a.shape
    n = w.shape[1]
    nk = k // tk
    return pl.pallas_call(
        functools.partial(_mm_acc_body, nk=nk),
        grid=(m // tm, n // tn, nk),
        in_specs=[pl.BlockSpec((tm, tk), lambda i, j, kk: (i, kk)),
                  pl.BlockSpec((tk, tn), lambda i, j, kk: (kk, j))],
        out_specs=pl.BlockSpec((tm, tn), lambda i, j, kk: (i, j)),
        out_shape=jax.ShapeDtypeStruct((m, n), out_dtype),
        scratch_shapes=[pltpu.VMEM((tm, tn), F32)],
        compiler_params=_cparams("arbitrary", "arbitrary", "arbitrary"),
        name="mm_acc",
    )(a, w)


ATTN_ONES_ROWS = 16
ATTN_SLAB = 256


def _attn_body(qi_tab, kv_tab, slopes, q_ref, k_ref, vt_ref, pos_ref, lq1_ref, lk1_ref, lq2_ref, lk2_ref, g_ref,
               o_ref, q1_s, q2_s, m1_s, a1_s, m2_s, a2_s, jmi_s, *, tq, tk, lam_init):
    h = pl.program_id(0)
    t = pl.program_id(1)
    qi = qi_tab[t]
    kv = kv_tab[t]
    slope = slopes[h]
    n_sub = tq // tk
    dv = A_VDIM

    @pl.when(t == 0)
    def _():
        jmi_s[...] = lax.broadcasted_iota(I32, (tk, tq), 0) - lax.broadcasted_iota(I32, (tk, tq), 1)

    @pl.when(kv == 0)
    def _():
        q = q_ref[...]
        lane = lax.broadcasted_iota(I32, q.shape, 1)
        zero = jnp.zeros_like(q)
        sv = jnp.full(q.shape, slope, F32)
        s_hi = sv.astype(BF16).astype(F32)
        s_lo = sv - s_hi
        coef = jnp.where(lane == 0, s_hi * 16.0, jnp.where(lane == 1, s_lo * 16.0,
                         jnp.where(lane == 2, s_hi, jnp.where(lane == 3, s_lo, 0.0)))).astype(BF16)
        q1_s[...] = jnp.concatenate([jnp.where(lane < A_QKDIM, q, zero), coef], axis=1)
        q2_s[...] = jnp.concatenate([jnp.where(lane >= A_QKDIM, q, zero), coef], axis=1)
        for m_s, a_s in ((m1_s, a1_s), (m2_s, a2_s)):
            m_s[...] = jnp.full(m_s.shape, -jnp.inf, F32)
            a_s[...] = jnp.zeros(a_s.shape, F32)

    shift = kv * tk - qi * tq

    def step(key_off):
        k_aug = jnp.concatenate([k_ref[...], pos_ref[...]], axis=1)
        vt = vt_ref[...]
        c = slope * shift.astype(F32)
        chains = []
        for c0 in range(0, tq, ATTN_SLAB):
            if key_off is not None and c0 + ATTN_SLAB - 1 < key_off:
                continue
            masked = key_off is not None and c0 < key_off + tk - 1
            for q_s, m_s, a_s in ((q1_s, m1_s, a1_s), (q2_s, m2_s, a2_s)):
                chains.append((q_s, m_s, a_s, slice(c0, c0 + ATTN_SLAB), masked))
        scores = [_dot_t(k_aug, q_s[cols, :]) for q_s, _, _, cols, _ in chains]
        for s, (_, m_s, a_s, cols, masked) in zip(scores, chains):
            if masked:
                s = jnp.where(jmi_s[:, cols] <= -key_off, s, NEG_INF)
            m_prev = m_s[:, cols]
            m_new = jnp.maximum(m_prev, jnp.max(s, axis=0, keepdims=True) + c)
            p = jnp.exp2(s - (m_new - c))
            a_s[:, cols] = jnp.exp2(m_prev - m_new) * a_s[:, cols] + _dot(vt, p.astype(BF16))
            m_s[:, cols] = m_new

    @pl.when(kv < qi * n_sub)
    def _():
        step(None)

    for r in range(n_sub):
        @pl.when(kv == qi * n_sub + r)
        def _():
            step(r * tk)

    @pl.when(kv == (qi + 1) * n_sub - 1)
    def _():
        lam = (jnp.exp(jnp.sum(lq1_ref[...] * lk1_ref[...], axis=-1, keepdims=True))
               - jnp.exp(jnp.sum(lq2_ref[...] * lk2_ref[...], axis=-1, keepdims=True)) + lam_init)
        a1 = a1_s[...]
        a2 = a2_s[...]
        o = a1[:dv] / a1[dv:dv + 1] - lam * (a2[:dv] / a2[dv:dv + 1])
        ms = jnp.mean(o * o, axis=0, keepdims=True)
        on = o * lax.rsqrt(ms + RMS_EPS) * g_ref[...] * (1.0 - lam_init)
        o_ref[...] = on.T.astype(o_ref.dtype)


def _diff_attention(qkvu, v_t, lam_q1, lam_k1, lam_q2, lam_k2, subln_g, *, n_heads, lam_init, tq=2048, tk=512):
    s = qkvu.shape[0]
    n_sub = tq // tk
    pairs = [(i, j) for i in range(s // tq) for j in range((i + 1) * n_sub)]
    qi_tab = jnp.asarray([p[0] for p in pairs], I32)
    kv_tab = jnp.asarray([p[1] for p in pairs], I32)
    slopes = jnp.asarray([math.log2(math.e) * 2.0 ** (-8.0 * (i + 1) / n_heads) for i in range(n_heads)], F32)
    hd = A_VDIM
    hv = hd + ATTN_ONES_ROWS
    j = np.arange(tk)
    pos = np.zeros((tk, hd), np.float32)
    pos[:, 0] = pos[:, 1] = j // 16
    pos[:, 2] = pos[:, 3] = j % 16
    small = pl.BlockSpec((1, A_QKDIM), lambda h, t, qt, kt, sl: (0, 0))
    grid_spec = pltpu.PrefetchScalarGridSpec(
        num_scalar_prefetch=3,
        grid=(n_heads, len(pairs)),
        in_specs=[pl.BlockSpec((tq, hd), lambda h, t, qt, kt, sl: (qt[t], h)),
                  pl.BlockSpec((tk, hd), lambda h, t, qt, kt, sl: (kt[t], n_heads + h)),
                  pl.BlockSpec((hv, tk), lambda h, t, qt, kt, sl: (h, kt[t])),
                  pl.BlockSpec((tk, hd), lambda h, t, qt, kt, sl: (0, 0)),
                  small, small, small, small,
                  pl.BlockSpec((hd, 1), lambda h, t, qt, kt, sl: (0, 0))],
        out_specs=pl.BlockSpec((tq, hd), lambda h, t, qt, kt, sl: (qt[t], h)),
        scratch_shapes=[pltpu.VMEM((tq, 2 * hd), BF16), pltpu.VMEM((tq, 2 * hd), BF16),
                        pltpu.VMEM((1, tq), F32), pltpu.VMEM((hv, tq), F32),
                        pltpu.VMEM((1, tq), F32), pltpu.VMEM((hv, tq), F32),
                        pltpu.VMEM((tk, tq), I32)])
    return pl.pallas_call(
        functools.partial(_attn_body, tq=tq, tk=tk, lam_init=lam_init),
        grid_spec=grid_spec,
        out_shape=jax.ShapeDtypeStruct((s, n_heads * hd), BF16),
        compiler_params=_cparams("arbitrary", "arbitrary"),
        name="diff_attention",
    )(qi_tab, kv_tab, slopes, qkvu, qkvu, v_t, jnp.asarray(pos, BF16),
      lam_q1.reshape(1, -1), lam_k1.reshape(1, -1), lam_q2.reshape(1, -1), lam_k2.reshape(1, -1),
      subln_g.reshape(-1, 1))


def _attn_value_rows(v, n_heads):
    s = v.shape[0]
    vt = v.T.reshape(n_heads, A_VDIM, s)
    ones = jnp.ones((n_heads, ATTN_ONES_ROWS, s), v.dtype)
    return jnp.concatenate([vt, ones], axis=1).reshape(n_heads * (A_VDIM + ATTN_ONES_ROWS), s)


def _pool_body(ucur_ref, uprev_ref, w_ref, sc_ref, o_ref, *, t_blk):
    g = pl.program_id(0)
    i = pl.program_id(1)
    win = jnp.left_shift(2, g)
    row = lax.broadcasted_iota(I32, (t_blk, t_blk), 0)
    col = lax.broadcasted_iota(I32, (t_blk, t_blk), 1)
    d = row - col
    band = jnp.where(jnp.logical_and(d >= 0, d < win), 1.0, 0.0).astype(BF16)
    rowp = lax.broadcasted_iota(I32, (t_blk, POOL_HALO), 0)
    colp = lax.broadcasted_iota(I32, (t_blk, POOL_HALO), 1)
    dp = rowp + POOL_HALO - colp
    bandp = jnp.where(jnp.logical_and(dp < win, i > 0), 1.0, 0.0).astype(BF16)
    u = ucur_ref[...]
    usum = _dot(band, u) + _dot(bandp, uprev_ref[...])
    tpos = i * t_blk + lax.broadcasted_iota(I32, (t_blk, 1), 0)
    cnt = jnp.minimum(tpos + 1, win).astype(F32)
    dev = usum / cnt - u.astype(F32)
    y = _dot(dev.astype(BF16), w_ref[...]) * sc_ref[...]
    o_ref[...] = y.astype(o_ref.dtype)


def _multiscale_pool(qkvu, pool_w, pool_scale, *, col_off, t_blk=256):
    s = qkvu.shape[0]
    ng, gd, _ = pool_w.shape
    goff = col_off // gd
    hb = t_blk // POOL_HALO
    return pl.pallas_call(
        functools.partial(_pool_body, t_blk=t_blk),
        grid=(ng, s // t_blk),
        in_specs=[pl.BlockSpec((t_blk, gd), lambda g, i: (i, goff + g)),
                  pl.BlockSpec((POOL_HALO, gd), lambda g, i: (jnp.maximum(i * hb - 1, 0), goff + g)),
                  pl.BlockSpec((None, gd, gd), lambda g, i: (g, 0, 0)),
                  pl.BlockSpec((1, gd), lambda g, i: (0, g))],
        out_specs=pl.BlockSpec((t_blk, gd), lambda g, i: (i, g)),
        out_shape=jax.ShapeDtypeStruct((s, ng * gd), BF16),
        compiler_params=_cparams("arbitrary", "arbitrary"),
        name="multiscale_pool",
    )(qkvu, qkvu, pool_w, pool_scale.reshape(1, -1))


def _hgrn_tables():
    c = HGRN_CHUNK
    idx = np.arange(c)
    mats, masks = [], []
    h = c // 2
    while h >= 1:
        upper = (idx % (2 * h)) >= h
        e = idx - (idx % (2 * h)) + h - 1
        u = idx[None, :]
        pq = (upper[:, None] & (u > e[:, None]) & (u <= idx[:, None]))
        pk = ((~upper)[:, None] & (u > idx[:, None]) & (u <= e[:, None]))
        mats += [pq, pk]
        masks.append(upper[:, None] & (~upper)[None, :] & ((idx[:, None] // (2 * h)) == (idx[None, :] // (2 * h))))
        h //= 2
    masks.append(idx[:, None] == idx[None, :])
    mats.append(idx[None, :] <= idx[:, None])
    mats.append(idx[None, :] > idx[:, None])
    return (np.concatenate(mats, axis=0).astype(np.float32),
            np.stack(masks, axis=0).astype(np.float32))


def _hgrn_body(q_ref, f_ref, v_ref, gs_ref, p_ref, mask_ref, gn_ref, o_ref, st_ref, *, t_blk, n_grp):
    c = HGRN_CHUNK
    n_lvl = mask_ref.shape[0] - 1
    heads = range(n_grp)

    @pl.when(pl.program_id(1) == 0)
    def _():
        st_ref[...] = jnp.zeros(st_ref.shape, F32)

    def col(x, hg):
        return x[:, hg * C_KDIM:(hg + 1) * C_KDIM]

    def stack(parts):
        return jnp.concatenate(parts, axis=0)

    def block_diag(parts):
        zero = jnp.zeros_like(parts[0])
        return stack([jnp.concatenate([parts[hg] if j == hg else zero for j in heads], axis=1) for hg in heads])

    def chunk(ci, carry):
        rows = pl.ds(pl.multiple_of(ci * c, c), c)
        f = f_ref[rows, :]
        lf = jnp.log(f)
        kk = 1.0 - f
        hi = lf.astype(BF16)
        r1 = lf - hi.astype(F32)
        mid = r1.astype(BF16)
        lo = (r1 - mid.astype(F32)).astype(BF16)
        ed = jnp.exp(_dot(p_ref[...], stack([hi, mid, lo])))
        e = [col(ed, hg) for hg in heads]
        q = q_ref[rows, :].astype(F32)
        qh = [col(q, hg) for hg in heads]
        kh = [col(kk, hg) for hg in heads]
        attn = mask_ref[n_lvl] * _dot_t(stack([x.astype(BF16) for x in qh]), stack([x.astype(BF16) for x in kh]))
        for lv in range(n_lvl):
            qs = stack([(qh[hg] * e[hg][2 * lv * c:(2 * lv + 1) * c]).astype(BF16) for hg in heads])
            ks = stack([(kh[hg] * e[hg][(2 * lv + 1) * c:(2 * lv + 2) * c]).astype(BF16) for hg in heads])
            attn = attn + mask_ref[lv] * _dot_t(qs, ks)
        eb = [e[hg][2 * n_lvl * c:(2 * n_lvl + 1) * c] for hg in heads]
        el = [e[hg][(2 * n_lvl + 1) * c:(2 * n_lvl + 2) * c] for hg in heads]
        v = v_ref[rows, :]
        v_st = stack([col(v, hg) for hg in heads])
        st = st_ref[...]
        q_bd = block_diag([(qh[hg] * eb[hg]).astype(BF16) for hg in heads])
        o = _dot(attn.astype(BF16), v_st) + _dot_t(q_bd, st.astype(BF16))
        k_bd = block_diag([(kh[hg] * el[hg]).astype(BF16) for hg in heads])
        decay = jnp.concatenate([eb[hg][c - 1:c, :] for hg in heads], axis=1)
        st_ref[...] = st * decay + lax.dot_general(v_st, k_bd, (((0,), (0,)), ((), ())),
                                                   preferred_element_type=F32)
        ms = jnp.mean(o * o, axis=-1, keepdims=True)
        on = o * lax.rsqrt(ms + RMS_EPS) * gn_ref[...]
        gs = gs_ref[rows, :].astype(F32)
        for hg in heads:
            o_ref[rows, hg * C_VDIM:(hg + 1) * C_VDIM] = (on[hg * c:(hg + 1) * c] * col(gs, hg)).astype(o_ref.dtype)
        return carry

    lax.fori_loop(0, t_blk // c, chunk, 0)


def _hgrn2(qs, forget, v, gs, gnorm_g, *, t_blk=512, n_grp=4):
    s, d = qs.shape
    n_heads = d // C_KDIM
    p_np, mask_np = _hgrn_tables()
    p_mat = jnp.asarray(np.concatenate([p_np] * 3, axis=1), BF16)
    masks = jnp.asarray(np.stack([np.kron(np.eye(n_grp, dtype=np.float32), m) for m in mask_np]), F32)
    blk = lambda: pl.BlockSpec((t_blk, n_grp * C_KDIM), lambda h, i: (i, h))
    return pl.pallas_call(
        functools.partial(_hgrn_body, t_blk=t_blk, n_grp=n_grp),
        grid=(n_heads // n_grp, s // t_blk),
        in_specs=[blk(), blk(), blk(), blk(),
                  pl.BlockSpec(p_mat.shape, lambda h, i: (0, 0)),
                  pl.BlockSpec(masks.shape, lambda h, i: (0, 0, 0)),
                  pl.BlockSpec((1, C_VDIM), lambda h, i: (0, 0))],
        out_specs=blk(),
        out_shape=jax.ShapeDtypeStruct((s, d), BF16),
        scratch_shapes=[pltpu.VMEM((C_VDIM, n_grp * C_KDIM), F32)],
        compiler_params=_cparams("arbitrary", "arbitrary"),
        name="hgrn2",
    )(qs, forget, v, gs, p_mat, masks, gnorm_g.reshape(1, -1))


def _router_body(x_ref, sc_ref, sh_ref, whi_ref, wlo_ref, info_ref, cnt_ref, carry_s, *, tm):
    @pl.when(pl.program_id(0) == 0)
    def _():
        carry_s[...] = jnp.zeros(carry_s.shape, F32)

    hmod = x_ref[...] * (1.0 + sc_ref[...]) + sh_ref[...]
    hh = hmod.astype(BF16)
    hl = (hmod - hh.astype(F32)).astype(BF16)
    whi = whi_ref[...]
    logits = _dot(hh, whi) + _dot(hl, whi) + _dot(hh, wlo_ref[...])
    lane = lax.broadcasted_iota(I32, (tm, LANES), 1)
    logits = jnp.where(lane < N_EXPERTS, logits, -jnp.inf)
    m1 = jnp.max(logits, axis=-1, keepdims=True)
    e1 = jnp.min(jnp.where(logits == m1, lane, LANES), axis=-1, keepdims=True)
    rest = jnp.where(lane == e1, -jnp.inf, logits)
    m2 = jnp.max(rest, axis=-1, keepdims=True)
    e2 = jnp.min(jnp.where(rest == m2, lane, LANES), axis=-1, keepdims=True)
    ex = jnp.exp(m2 - m1)
    g1 = 1.0 / (1.0 + ex)
    g2 = ex / (1.0 + ex)
    onehot = jnp.where(jnp.logical_or(lane == e1, lane == e2), 1.0, 0.0)
    row = lax.broadcasted_iota(I32, (tm, tm), 0)
    col = lax.broadcasted_iota(I32, (tm, tm), 1)
    before = jnp.where(row > col, 1.0, 0.0).astype(BF16)
    cum = _dot(before, onehot.astype(BF16)) + carry_s[...]
    r1 = jnp.sum(jnp.where(lane == e1, cum, 0.0), axis=-1, keepdims=True)
    r2 = jnp.sum(jnp.where(lane == e2, cum, 0.0), axis=-1, keepdims=True)
    carry_s[...] = carry_s[...] + jnp.sum(onehot, axis=0, keepdims=True)
    cnt_ref[...] = carry_s[...]
    info = jnp.where(lane == 0, e1.astype(F32), 0.0)
    info = jnp.where(lane == 1, e2.astype(F32), info)
    info = jnp.where(lane == 2, g1, info)
    info = jnp.where(lane == 3, g2, info)
    info = jnp.where(lane == 4, r1, info)
    info = jnp.where(lane == 5, r2, info)
    info_ref[...] = info


def _router(x, scale, shift, router_w, *, tm=256):
    s, d = x.shape
    wpad = jnp.zeros((d, LANES), F32).at[:, :N_EXPERTS].set(router_w)
    whi = wpad.astype(BF16)
    wlo = (wpad - whi.astype(F32)).astype(BF16)
    vec = pl.BlockSpec((1, d), lambda i: (0, 0))
    wspec = pl.BlockSpec((d, LANES), lambda i: (0, 0))
    return pl.pallas_call(
        functools.partial(_router_body, tm=tm),
        grid=(s // tm,),
        in_specs=[pl.BlockSpec((tm, d), lambda i: (i, 0)), vec, vec, wspec, wspec],
        out_specs=[pl.BlockSpec((tm, LANES), lambda i: (i, 0)),
                   pl.BlockSpec((1, LANES), lambda i: (0, 0))],
        out_shape=[jax.ShapeDtypeStruct((s, LANES), F32), jax.ShapeDtypeStruct((1, LANES), F32)],
        scratch_shapes=[pltpu.VMEM((1, LANES), F32)],
        compiler_params=_cparams("arbitrary"),
        name="router",
    )(x, scale.reshape(1, d), shift.reshape(1, d), whi, wlo)


def _dispatch_copies(s1_ref, s2_ref, h_ref, out_ref, sem, base, r):
    t = base + r
    src = h_ref.at[pl.ds(r, 1)]
    return (pltpu.make_async_copy(src, out_ref.at[pl.ds(s1_ref[t], 1)], sem.at[0]),
            pltpu.make_async_copy(src, out_ref.at[pl.ds(s2_ref[t], 1)], sem.at[1]))


def _dispatch_body(s1_ref, s2_ref, h_ref, init_ref, out_ref, sem, *, tb):
    del init_ref
    base = pl.program_id(0) * tb

    def issue(r, carry):
        for prio, cp in enumerate(_dispatch_copies(s1_ref, s2_ref, h_ref, out_ref, sem, base, r)):
            cp.start(priority=prio)
        return carry

    def drain(r, carry):
        for cp in _dispatch_copies(s1_ref, s2_ref, h_ref, out_ref, sem, base, r):
            cp.wait()
        return carry

    lax.fori_loop(0, tb, issue, 0, unroll=DMA_ISSUE_UNROLL)
    lax.fori_loop(0, tb, drain, 0, unroll=DMA_ISSUE_UNROLL)


def _dispatch(h_words, slot1, slot2, n_rows, *, tb=256):
    s, dw = h_words.shape
    init = jnp.zeros((n_rows, dw), h_words.dtype)
    grid_spec = pltpu.PrefetchScalarGridSpec(
        num_scalar_prefetch=2,
        grid=(s // tb,),
        in_specs=[pl.BlockSpec((tb, dw), lambda i, a, b: (i, 0)),
                  pl.BlockSpec(memory_space=pl.ANY)],
        out_specs=pl.BlockSpec(memory_space=pl.ANY),
        scratch_shapes=[pltpu.SemaphoreType.DMA((2,))])
    return pl.pallas_call(
        functools.partial(_dispatch_body, tb=tb),
        grid_spec=grid_spec,
        out_shape=jax.ShapeDtypeStruct((n_rows, dw), h_words.dtype),
        input_output_aliases={3: 0},
        compiler_params=_cparams("arbitrary"),
        name="moe_dispatch",
    )(slot1, slot2, h_words, init)


def _expert_tile_changed(te_ref, i):
    return jnp.logical_or(i == 0, te_ref[i] != te_ref[jnp.maximum(i - 1, 0)])


def _expert_in_body(te_ref, nu_ref, x_ref, wa_ref, wb_ref, o_ref, wa_s, wb_s):
    i = pl.program_id(1)

    @pl.when(_expert_tile_changed(te_ref, i))
    def _():
        wa_s[...] = wa_ref[...].astype(BF16)
        wb_s[...] = wb_ref[...].astype(BF16)

    @pl.when(i < nu_ref[0])
    def _():
        x = _unpack_bf16_pairs(x_ref[...])
        o_ref[...] = (_silu(_dot(x, wa_s[...])) * _dot(x, wb_s[...])).astype(o_ref.dtype)

    @pl.when(i >= nu_ref[0])
    def _():
        o_ref[...] = jnp.zeros(o_ref.shape, o_ref.dtype)


def _expert_out_body(te_ref, nu_ref, x_ref, w_ref, o_ref, w_s):
    i = pl.program_id(1)

    @pl.when(_expert_tile_changed(te_ref, i))
    def _():
        w_s[...] = w_ref[...].astype(BF16)

    @pl.when(i < nu_ref[0])
    def _():
        o_ref[...] = _dot(x_ref[...], w_s[...]).astype(o_ref.dtype)

    @pl.when(i >= nu_ref[0])
    def _():
        o_ref[...] = jnp.zeros(o_ref.shape, o_ref.dtype)


def _expert_in(xs_words, w_in, tile_expert, n_used, *, tm, tn=512):
    p, kw = xs_words.shape
    k = 2 * kw
    half = w_in.shape[2] // 2
    hoff = half // tn
    row = lambda j, i, te, nu: (jnp.minimum(i, nu[0] - 1), 0)
    grid_spec = pltpu.PrefetchScalarGridSpec(
        num_scalar_prefetch=2,
        grid=(half // tn, p // tm),
        in_specs=[pl.BlockSpec((tm, kw), row),
                  pl.BlockSpec((None, k, tn), lambda j, i, te, nu: (te[i], 0, j)),
                  pl.BlockSpec((None, k, tn), lambda j, i, te, nu: (te[i], 0, j + hoff))],
        out_specs=pl.BlockSpec((tm, tn), lambda j, i, te, nu: (i, j)),
        scratch_shapes=[pltpu.VMEM((k, tn), BF16), pltpu.VMEM((k, tn), BF16)])
    return pl.pallas_call(
        _expert_in_body,
        grid_spec=grid_spec,
        out_shape=jax.ShapeDtypeStruct((p, half), BF16),
        compiler_params=_cparams("arbitrary", "arbitrary"),
        name="expert_in",
    )(tile_expert, n_used, xs_words, w_in, w_in)


def _expert_out(gs, w_out, tile_expert, n_used, *, tm, tn=1024):
    p, k = gs.shape
    n = w_out.shape[2]
    row = lambda j, i, te, nu: (jnp.minimum(i, nu[0] - 1), 0)
    grid_spec = pltpu.PrefetchScalarGridSpec(
        num_scalar_prefetch=2,
        grid=(n // tn, p // tm),
        in_specs=[pl.BlockSpec((tm, k), row),
                  pl.BlockSpec((None, k, tn), lambda j, i, te, nu: (te[i], 0, j))],
        out_specs=pl.BlockSpec((tm, tn), lambda j, i, te, nu: (i, j)),
        scratch_shapes=[pltpu.VMEM((k, tn), BF16)])
    return pl.pallas_call(
        _expert_out_body,
        grid_spec=grid_spec,
        out_shape=jax.ShapeDtypeStruct((p, n), F32),
        compiler_params=_cparams("arbitrary", "arbitrary"),
        name="expert_out",
    )(tile_expert, n_used, gs, w_out)


def _combine_copies(s1_ref, s2_ref, ys_ref, buf, sem, blk, slot, r, tb):
    t = blk * tb + r
    return (pltpu.make_async_copy(ys_ref.at[pl.ds(s1_ref[t], 1)], buf.at[slot, 0, pl.ds(r, 1)], sem.at[slot, 0]),
            pltpu.make_async_copy(ys_ref.at[pl.ds(s2_ref[t], 1)], buf.at[slot, 1, pl.ds(r, 1)], sem.at[slot, 1]))


def _combine_body(s1_ref, s2_ref, x_ref, info_ref, ys_ref, gate_ref, g_ref, b_ref, o_ref, buf, sem,
                  *, tb, alpha):
    i = pl.program_id(0)
    slot = lax.rem(i, 2)

    def gather(blk, slt):
        def issue(r, carry):
            for prio, cp in enumerate(_combine_copies(s1_ref, s2_ref, ys_ref, buf, sem, blk, slt, r, tb)):
                cp.start(priority=prio)
            return carry
        lax.fori_loop(0, tb, issue, 0, unroll=DMA_ISSUE_UNROLL)

    @pl.when(i == 0)
    def _():
        gather(i, slot)

    @pl.when(i + 1 < pl.num_programs(0))
    def _():
        gather(i + 1, 1 - slot)

    def drain(r, carry):
        for cp in _combine_copies(s1_ref, s2_ref, ys_ref, buf, sem, i, slot, r, tb):
            cp.wait()
        return carry

    lax.fori_loop(0, tb, drain, 0, unroll=DMA_ISSUE_UNROLL)
    info = info_ref[...]
    y = info[:, 2:3] * buf[slot, 0] + info[:, 3:4] * buf[slot, 1]
    r = alpha * x_ref[...] + (1.0 + gate_ref[...]) * y
    o_ref[...] = _layer_norm_rows(r, g_ref[...], b_ref[...])


def _combine_postnorm(x, info, ys, slot1, slot2, gate, g, b, *, alpha, tb=128):
    s, d = x.shape
    vec = pl.BlockSpec((1, d), lambda i, a, c: (0, 0))
    grid_spec = pltpu.PrefetchScalarGridSpec(
        num_scalar_prefetch=2,
        grid=(s // tb,),
        in_specs=[pl.BlockSpec((tb, d), lambda i, a, c: (i, 0)),
                  pl.BlockSpec((tb, LANES), lambda i, a, c: (i, 0)),
                  pl.BlockSpec(memory_space=pl.ANY),
                  vec, vec, vec],
        out_specs=pl.BlockSpec((tb, d), lambda i, a, c: (i, 0)),
        scratch_shapes=[pltpu.VMEM((2, 2, tb, d), F32), pltpu.SemaphoreType.DMA((2, 2))])
    return pl.pallas_call(
        functools.partial(_combine_body, tb=tb, alpha=alpha),
        grid_spec=grid_spec,
        out_shape=jax.ShapeDtypeStruct((s, d), F32),
        compiler_params=_cparams("arbitrary"),
        name="moe_combine_postnorm",
    )(slot1, slot2, x, info, ys, gate.reshape(1, d), g.reshape(1, d), b.reshape(1, d))


def _moe_plan(info, counts, *, tm, n_tiles):
    e1 = info[:, 0].astype(I32)
    e2 = info[:, 1].astype(I32)
    r1 = info[:, 4].astype(I32)
    r2 = info[:, 5].astype(I32)
    cnt = counts[0, :N_EXPERTS].astype(I32)
    tiles = (cnt + tm - 1) // tm
    tile_end = jnp.cumsum(tiles)
    row_off = (tile_end - tiles) * tm
    slot1 = row_off[e1] + r1
    slot2 = row_off[e2] + r2
    tile_expert = jnp.minimum(
        jnp.sum(jnp.arange(n_tiles, dtype=I32)[:, None] >= tile_end[None, :], axis=1),
        N_EXPERTS - 1).astype(I32)
    n_used = tile_end[-1:].astype(I32)
    return slot1, slot2, tile_expert, n_used


def kernel(x, c, ada_w, ada_b, ln_g, ln_b, even_w_in, lam_q1, lam_k1, lam_q2, lam_k2, subln_g, pool_w,
           pool_scale, even_w_out, ffn_w_in, ffn_w_out, odd_w_in, lb_raw, gnorm_g, odd_w_out, router_w,
           exp_w_in, exp_w_out):
    _, seq, d = x.shape
    depth = ada_w.shape[0]
    alpha = (2 * depth) ** 0.25
    a_width = d // 2
    a_heads = a_width // A_VDIM
    x2 = x.reshape(seq, d)

    lb_all = jnp.cumsum(jax.nn.softmax(lb_raw.astype(F32), axis=0), axis=0)
    lb_all = lb_all - lb_all[0]
    mod = _ada_mod(c, ada_w, ada_b)

    def mod_parts(l):
        return [mod[l, i * d:(i + 1) * d] for i in range(6)]

    sh1, sc1, g1, sh2, sc2, g2 = mod_parts(0)
    h = _modulate(x2, sc1, sh1)
    qkvu = _proj([h], even_w_in[0], col_off=0, n_cols=3 * a_width + (d - a_width), out_dtype=BF16,
                 epilogue="qscale", q_cols=a_width, q_scale=A_QKDIM ** -0.5 * math.log2(math.e),
                 name="even_in_proj")
    lam_init = 0.8 - 0.6 * math.exp(-0.3 * 0)
    v_t = _attn_value_rows(qkvu[:, 2 * a_width:3 * a_width], a_heads)
    o_a = _diff_attention(qkvu, v_t, lam_q1[0], lam_k1[0], lam_q2[0], lam_k2[0], subln_g[0],
                          n_heads=a_heads, lam_init=lam_init)
    o_b = _multiscale_pool(qkvu, pool_w[0].astype(BF16), pool_scale[0], col_off=3 * a_width)
    y = _proj([o_a, o_b], even_w_out[0], col_off=0, n_cols=d, out_dtype=BF16, name="even_out_proj")
    x2, h = _postnorm(x2, y, g1, ln_g[0, 0], ln_b[0, 0], sc2, sh2, alpha=alpha)

    ff = ffn_w_out.shape[1]
    gact = _swiglu_in(h, ffn_w_in[0])
    y = _mm_acc(gact, ffn_w_out[0], tk=ff // 2, out_dtype=BF16)
    sh1, sc1, g1n, sh2n, sc2n, g2n = mod_parts(1)
    x2, h = _postnorm(x2, y, g2, ln_g[0, 1], ln_b[0, 1], sc1, sh1, alpha=alpha)

    w_odd = odd_w_in[0]
    qs = _proj([h], w_odd, col_off=0, n_cols=d, out_dtype=BF16, epilogue="silu", name="odd_in_q")
    fg = _proj([h], w_odd, col_off=d, n_cols=d, out_dtype=F32, epilogue="forget", lb=lb_all[1], name="odd_in_f")
    vi = _proj([h], w_odd, col_off=2 * d, n_cols=d, out_dtype=BF16, name="odd_in_i")
    gs = _proj([h], w_odd, col_off=3 * d, n_cols=d, out_dtype=BF16, epilogue="silu", name="odd_in_g")
    o_c = _hgrn2(qs, fg, vi, gs, gnorm_g[0])
    y = _proj([o_c], odd_w_out[0], col_off=0, n_cols=d, out_dtype=BF16, name="odd_out_proj")
    x2, h_words = _postnorm(x2, y, g1n, ln_g[1, 0], ln_b[1, 0], sc2n, sh2n, alpha=alpha, pack=True)

    tm_e = 512
    n_tiles = (2 * seq) // tm_e + N_EXPERTS
    info, counts = _router(x2, sc2n, sh2n, router_w[0])
    slot1, slot2, tile_expert, n_used = _moe_plan(info, counts, tm=tm_e, n_tiles=n_tiles)
    xs_words = _dispatch(h_words, slot1, slot2, n_tiles * tm_e)
    gexp = _expert_in(xs_words, exp_w_in[0], tile_expert, n_used, tm=tm_e)
    ys = _expert_out(gexp, exp_w_out[0], tile_expert, n_used, tm=tm_e)
    out = _combine_postnorm(x2, info, ys, slot1, slot2, g2n, ln_g[1, 1], ln_b[1, 1], alpha=alpha)
    return out.reshape(x.shape)
```

```python
import functools
import math

import numpy as np
import jax
import jax.numpy as jnp
from jax import lax
from jax.experimental import pallas as pl
from jax.experimental.pallas import tpu as pltpu

F32 = jnp.float32
BF16 = jnp.bfloat16
I32 = jnp.int32

A_VDIM = 128
A_QKDIM = 64
POOL_WINDOWS = (2, 4, 8, 16)
POOL_HALO = 16
C_KDIM = 128
C_VDIM = 128
HGRN_CHUNK = 64
N_EXPERTS = 8
LN_EPS = 1e-5
RMS_EPS = 1e-6
NEG_INF = -1e30

LANES = 128
VMEM_LIMIT_BYTES = 56 * 1024 * 1024
DMA_ISSUE_UNROLL = 8


def _cparams(*sem):
    return pltpu.CompilerParams(dimension_semantics=sem, vmem_limit_bytes=VMEM_LIMIT_BYTES)


def _silu(x):
    return x * (1.0 / (1.0 + jnp.exp(-x)))


def _dot(a, b):
    return jnp.dot(a, b, preferred_element_type=F32)


def _dot_t(a, b):
    return lax.dot_general(a, b, (((1,), (1,)), ((), ())), preferred_element_type=F32)


def _pack_bf16_pairs(h):
    half = h.shape[1] // 2
    bits = pltpu.bitcast(h.astype(BF16).astype(F32), I32)
    return jnp.bitwise_or(bits[:, :half], lax.shift_right_logical(bits[:, half:], 16))


def _unpack_bf16_pairs(w):
    hi = pltpu.bitcast(jnp.bitwise_and(w, -65536), F32).astype(BF16)
    lo = pltpu.bitcast(lax.shift_left(w, 16), F32).astype(BF16)
    return jnp.concatenate([hi, lo], axis=1)


def _ada_body(c_ref, w_ref, b_ref, o_ref):
    ca = _silu(c_ref[...]).astype(BF16)
    o_ref[...] = _dot(ca, w_ref[...].astype(BF16)) + b_ref[...]


def _ada_mod(c, ada_w, ada_b, *, tn=1024):
    depth, d, n = ada_w.shape
    c8 = jnp.broadcast_to(c.astype(F32), (8, d))
    out = pl.pallas_call(
        _ada_body,
        grid=(depth, n // tn),
        in_specs=[pl.BlockSpec((8, d), lambda l, j: (0, 0)),
                  pl.BlockSpec((None, d, tn), lambda l, j: (l, 0, j)),
                  pl.BlockSpec((None, 1, tn), lambda l, j: (l, 0, j))],
        out_specs=pl.BlockSpec((None, 8, tn), lambda l, j: (l, 0, j)),
        out_shape=jax.ShapeDtypeStruct((depth, 8, n), F32),
        compiler_params=_cparams("arbitrary", "arbitrary"),
        name="ada_mod",
    )(c8, ada_w, ada_b.reshape(depth, 1, n))
    return out[:, 0, :]


def _modulate_body(x_ref, sc_ref, sh_ref, o_ref):
    o_ref[...] = (x_ref[...] * (1.0 + sc_ref[...]) + sh_ref[...]).astype(o_ref.dtype)


def _modulate(x, scale, shift, *, tm=512):
    s, d = x.shape
    vec = pl.BlockSpec((1, d), lambda i: (0, 0))
    return pl.pallas_call(
        _modulate_body,
        grid=(s // tm,),
        in_specs=[pl.BlockSpec((tm, d), lambda i: (i, 0)), vec, vec],
        out_specs=pl.BlockSpec((tm, d), lambda i: (i, 0)),
        out_shape=jax.ShapeDtypeStruct((s, d), BF16),
        compiler_params=_cparams("arbitrary"),
        name="modulate",
    )(x, scale.reshape(1, d), shift.reshape(1, d))


def _layer_norm_rows(r, g, b):
    mu = jnp.mean(r, axis=-1, keepdims=True)
    rc = r - mu
    var = jnp.mean(rc * rc, axis=-1, keepdims=True)
    return rc * lax.rsqrt(var + LN_EPS) * g + b


def _postnorm_body(x_ref, y_ref, gate_ref, g_ref, b_ref, sc_ref, sh_ref, xo_ref, ho_ref, *, alpha, pack):
    r = alpha * x_ref[...] + (1.0 + gate_ref[...]) * y_ref[...].astype(F32)
    xn = _layer_norm_rows(r, g_ref[...], b_ref[...])
    xo_ref[...] = xn
    hn = xn * (1.0 + sc_ref[...]) + sh_ref[...]
    ho_ref[...] = _pack_bf16_pairs(hn) if pack else hn.astype(ho_ref.dtype)


def _postnorm(x, y, gate, g, b, nscale, nshift, *, alpha, pack=False, tm=256):
    s, d = x.shape
    row = pl.BlockSpec((tm, d), lambda i: (i, 0))
    vec = pl.BlockSpec((1, d), lambda i: (0, 0))
    hd, hdt = (d // 2, I32) if pack else (d, BF16)
    return pl.pallas_call(
        functools.partial(_postnorm_body, alpha=alpha, pack=pack),
        grid=(s // tm,),
        in_specs=[row, row, vec, vec, vec, vec, vec],
        out_specs=[row, pl.BlockSpec((tm, hd), lambda i: (i, 0))],
        out_shape=[jax.ShapeDtypeStruct((s, d), F32), jax.ShapeDtypeStruct((s, hd), hdt)],
        compiler_params=_cparams("arbitrary"),
        name="postnorm",
    )(x, y, gate.reshape(1, d), g.reshape(1, d), b.reshape(1, d), nscale.reshape(1, d), nshift.reshape(1, d))


def _proj_body(*refs, n_a, epilogue, tn, q_cols, q_scale):
    a_refs, w_ref, extra = refs[:n_a], refs[n_a], refs[n_a + 1:-2]
    o_ref, wb_s = refs[-2], refs[-1]

    @pl.when(pl.program_id(1) == 0)
    def _():
        wb_s[...] = w_ref[...].astype(BF16)

    acc, off = None, 0
    for a_ref in a_refs:
        kp = a_ref.shape[1]
        part = _dot(a_ref[...], wb_s[off:off + kp, :])
        acc = part if acc is None else acc + part
        off += kp
    if epilogue == "silu":
        acc = _silu(acc)
    elif epilogue == "forget":
        lb = extra[0][...]
        acc = lb + (1.0 - lb) * (1.0 / (1.0 + jnp.exp(-acc)))
    elif epilogue == "qscale":
        acc = acc * jnp.where(pl.program_id(0) * tn < q_cols, q_scale, 1.0)
    o_ref[...] = acc.astype(o_ref.dtype)


def _proj(a_parts, w, *, col_off, n_cols, out_dtype, epilogue="id", lb=None, q_cols=0, q_scale=1.0,
          tm=1024, tn=512, name="proj"):
    m = a_parts[0].shape[0]
    k = w.shape[0]
    joff = col_off // tn
    in_specs = [pl.BlockSpec((tm, a.shape[1]), lambda j, i: (i, 0)) for a in a_parts]
    in_specs.append(pl.BlockSpec((k, tn), lambda j, i: (0, j + joff)))
    args = list(a_parts) + [w]
    if epilogue == "forget":
        in_specs.append(pl.BlockSpec((1, tn), lambda j, i: (0, j)))
        args.append(lb.reshape(1, n_cols))
    return pl.pallas_call(
        functools.partial(_proj_body, n_a=len(a_parts), epilogue=epilogue, tn=tn, q_cols=q_cols, q_scale=q_scale),
        grid=(n_cols // tn, m // tm),
        in_specs=in_specs,
        out_specs=pl.BlockSpec((tm, tn), lambda j, i: (i, j)),
        out_shape=jax.ShapeDtypeStruct((m, n_cols), out_dtype),
        scratch_shapes=[pltpu.VMEM((k, tn), BF16)],
        compiler_params=_cparams("arbitrary", "arbitrary"),
        name=name,
    )(*args)


def _swiglu_in_body(a_ref, wa_ref, wb_ref, o_ref, wa_s, wb_s):
    @pl.when(pl.program_id(1) == 0)
    def _():
        wa_s[...] = wa_ref[...].astype(BF16)
        wb_s[...] = wb_ref[...].astype(BF16)

    a = a_ref[...]
    o_ref[...] = (_silu(_dot(a, wa_s[...])) * _dot(a, wb_s[...])).astype(o_ref.dtype)


def _swiglu_in(a, w, *, tm=1024, tn=256):
    m, k = a.shape
    half = w.shape[1] // 2
    hoff = half // tn
    return pl.pallas_call(
        _swiglu_in_body,
        grid=(half // tn, m // tm),
        in_specs=[pl.BlockSpec((tm, k), lambda j, i: (i, 0)),
                  pl.BlockSpec((k, tn), lambda j, i: (0, j)),
                  pl.BlockSpec((k, tn), lambda j, i: (0, j + hoff))],
        out_specs=pl.BlockSpec((tm, tn), lambda j, i: (i, j)),
        out_shape=jax.ShapeDtypeStruct((m, half), BF16),
        scratch_shapes=[pltpu.VMEM((k, tn), BF16), pltpu.VMEM((k, tn), BF16)],
        compiler_params=_cparams("arbitrary", "arbitrary"),
        name="swiglu_in",
    )(a, w, w)


def _mm_acc_body(a_ref, w_ref, o_ref, acc_ref, *, nk):
    kk = pl.program_id(2)
    part = _dot(a_ref[...], w_ref[...].astype(BF16))

    @pl.when(kk == 0)
    def _():
        acc_ref[...] = part

    @pl.when(kk > 0)
    def _():
        acc_ref[...] += part

    @pl.when(kk == nk - 1)
    def _():
        o_ref[...] = acc_ref[...].astype(o_ref.dtype)


def _mm_acc(a, w, *, tm=1024, tn=512, tk, out_dtype=F32):
    m, k = a.shape
    n = w.shape[1]
    nk = k // tk
    return pl.pallas_call(
        functools.partial(_mm_acc_body, nk=nk),
        grid=(m // tm, n // tn, nk),
        in_specs=[pl.BlockSpec((tm, tk), lambda i, j, kk: (i, kk)),
                  pl.BlockSpec((tk, tn), lambda i, j, kk: (kk, j))],
        out_specs=pl.BlockSpec((tm, tn), lambda i, j, kk: (i, j)),
        out_shape=jax.ShapeDtypeStruct((m, n), out_dtype),
        scratch_shapes=[pltpu.VMEM((tm, tn), F32)],
        compiler_params=_cparams("arbitrary", "arbitrary", "arbitrary"),
        name="mm_acc",
    )(a, w)


ATTN_ONES_ROWS = 16
ATTN_SLAB = 512


def _attn_body(qi_tab, kv_tab, slopes, q_ref, k_ref, vt_ref, pos_ref, lq1_ref, lk1_ref, lq2_ref, lk2_ref, g_ref,
               o_ref, q1_s, q2_s, m1_s, a1_s, m2_s, a2_s, jmi_s, *, tq, tk, lam_init):
    h = pl.program_id(0)
    t = pl.program_id(1)
    qi = qi_tab[t]
    kv = kv_tab[t]
    slope = slopes[h]
    n_sub = tq // tk
    dv = A_VDIM

    @pl.when(t == 0)
    def _():
        jmi_s[...] = lax.broadcasted_iota(I32, (tk, tq), 0) - lax.broadcasted_iota(I32, (tk, tq), 1)

    @pl.when(kv == 0)
    def _():
        q = q_ref[...]
        lane = lax.broadcasted_iota(I32, q.shape, 1)
        zero = jnp.zeros_like(q)
        sv = jnp.full(q.shape, slope, F32)
        s_hi = sv.astype(BF16).astype(F32)
        s_lo = sv - s_hi
        coef = jnp.where(lane == 0, s_hi * 16.0, jnp.where(lane == 1, s_lo * 16.0,
                         jnp.where(lane == 2, s_hi, jnp.where(lane == 3, s_lo, 0.0)))).astype(BF16)
        q1_s[...] = jnp.concatenate([jnp.where(lane < A_QKDIM, q, zero), coef], axis=1)
        q2_s[...] = jnp.concatenate([jnp.where(lane >= A_QKDIM, q, zero), coef], axis=1)
        for m_s, a_s in ((m1_s, a1_s), (m2_s, a2_s)):
            m_s[...] = jnp.full(m_s.shape, -jnp.inf, F32)
            a_s[...] = jnp.zeros(a_s.shape, F32)

    shift = kv * tk - qi * tq

    def step(key_off):
        k_aug = jnp.concatenate([k_ref[...], pos_ref[...]], axis=1)
        vt = vt_ref[...]
        c = slope * shift.astype(F32)
        chains = []
        for c0 in range(0, tq, ATTN_SLAB):
            if key_off is not None and c0 + ATTN_SLAB - 1 < key_off:
                continue
            masked = key_off is not None and c0 < key_off + tk - 1
            for q_s, m_s, a_s in ((q1_s, m1_s, a1_s), (q2_s, m2_s, a2_s)):
                chains.append((q_s, m_s, a_s, slice(c0, c0 + ATTN_SLAB), masked))
        scores = [_dot_t(k_aug, q_s[cols, :]) for q_s, _, _, cols, _ in chains]
        for s, (_, m_s, a_s, cols, masked) in zip(scores, chains):
            if masked:
                s = jnp.where(jmi_s[:, cols] <= -key_off, s, NEG_INF)
            m_prev = m_s[:, cols]
            m_new = jnp.maximum(m_prev, jnp.max(s, axis=0, keepdims=True) + c)
            p = jnp.exp2(s - (m_new - c))
            a_s[:, cols] = jnp.exp2(m_prev - m_new) * a_s[:, cols] + _dot(vt, p.astype(BF16))
            m_s[:, cols] = m_new

    @pl.when(kv < qi * n_sub)
    def _():
        step(None)

    for r in range(n_sub):
        @pl.when(kv == qi * n_sub + r)
        def _():
            step(r * tk)

    @pl.when(kv == (qi + 1) * n_sub - 1)
    def _():
        lam = (jnp.exp(jnp.sum(lq1_ref[...] * lk1_ref[...], axis=-1, keepdims=True))
               - jnp.exp(jnp.sum(lq2_ref[...] * lk2_ref[...], axis=-1, keepdims=True)) + lam_init)
        a1 = a1_s[...]
        a2 = a2_s[...]
        o = a1[:dv] / a1[dv:dv + 1] - lam * (a2[:dv] / a2[dv:dv + 1])
        ms = jnp.mean(o * o, axis=0, keepdims=True)
        on = o * lax.rsqrt(ms + RMS_EPS) * g_ref[...] * (1.0 - lam_init)
        o_ref[...] = on.T.astype(o_ref.dtype)


def _diff_attention(qkvu, v_t, lam_q1, lam_k1, lam_q2, lam_k2, subln_g, *, n_heads, lam_init, tq=2048, tk=512):
    s = qkvu.shape[0]
    n_sub = tq // tk
    pairs = [(i, j) for i in range(s // tq) for j in range((i + 1) * n_sub)]
    qi_tab = jnp.asarray([p[0] for p in pairs], I32)
    kv_tab = jnp.asarray([p[1] for p in pairs], I32)
    slopes = jnp.asarray([math.log2(math.e) * 2.0 ** (-8.0 * (i + 1) / n_heads) for i in range(n_heads)], F32)
    hd = A_VDIM
    hv = hd + ATTN_ONES_ROWS
    j = np.arange(tk)
    pos = np.zeros((tk, hd), np.float32)
    pos[:, 0] = pos[:, 1] = j // 16
    pos[:, 2] = pos[:, 3] = j % 16
    small = pl.BlockSpec((1, A_QKDIM), lambda h, t, qt, kt, sl: (0, 0))
    grid_spec = pltpu.PrefetchScalarGridSpec(
        num_scalar_prefetch=3,
        grid=(n_heads, len(pairs)),
        in_specs=[pl.BlockSpec((tq, hd), lambda h, t, qt, kt, sl: (qt[t], h)),
                  pl.BlockSpec((tk, hd), lambda h, t, qt, kt, sl: (kt[t], n_heads + h)),
                  pl.BlockSpec((hv, tk), lambda h, t, qt, kt, sl: (h, kt[t])),
                  pl.BlockSpec((tk, hd), lambda h, t, qt, kt, sl: (0, 0)),
                  small, small, small, small,
                  pl.BlockSpec((hd, 1), lambda h, t, qt, kt, sl: (0, 0))],
        out_specs=pl.BlockSpec((tq, hd), lambda h, t, qt, kt, sl: (qt[t], h)),
        scratch_shapes=[pltpu.VMEM((tq, 2 * hd), BF16), pltpu.VMEM((tq, 2 * hd), BF16),
                        pltpu.VMEM((1, tq), F32), pltpu.VMEM((hv, tq), F32),
                        pltpu.VMEM((1, tq), F32), pltpu.VMEM((hv, tq), F32),
                        pltpu.VMEM((tk, tq), I32)])
    return pl.pallas_call(
        functools.partial(_attn_body, tq=tq, tk=tk, lam_init=lam_init),
        grid_spec=grid_spec,
        out_shape=jax.ShapeDtypeStruct((s, n_heads * hd), BF16),
        compiler_params=_cparams("arbitrary", "arbitrary"),
        name="diff_attention",
    )(qi_tab, kv_tab, slopes, qkvu, qkvu, v_t, jnp.asarray(pos, BF16),
      lam_q1.reshape(1, -1), lam_k1.reshape(1, -1), lam_q2.reshape(1, -1), lam_k2.reshape(1, -1),
      subln_g.reshape(-1, 1))


def _attn_value_rows(v, n_heads):
    s = v.shape[0]
    vt = v.T.reshape(n_heads, A_VDIM, s)
    ones = jnp.ones((n_heads, ATTN_ONES_ROWS, s), v.dtype)
    return jnp.concatenate([vt, ones], axis=1).reshape(n_heads * (A_VDIM + ATTN_ONES_ROWS), s)


def _pool_body(ucur_ref, uprev_ref, w_ref, sc_ref, o_ref, *, t_blk):
    g = pl.program_id(0)
    i = pl.program_id(1)
    win = jnp.left_shift(2, g)
    row = lax.broadcasted_iota(I32, (t_blk, t_blk), 0)
    col = lax.broadcasted_iota(I32, (t_blk, t_blk), 1)
    d = row - col
    band = jnp.where(jnp.logical_and(d >= 0, d < win), 1.0, 0.0).astype(BF16)
    rowp = lax.broadcasted_iota(I32, (t_blk, POOL_HALO), 0)
    colp = lax.broadcasted_iota(I32, (t_blk, POOL_HALO), 1)
    dp = rowp + POOL_HALO - colp
    bandp = jnp.where(jnp.logical_and(dp < win, i > 0), 1.0, 0.0).astype(BF16)
    u = ucur_ref[...]
    usum = _dot(band, u) + _dot(bandp, uprev_ref[...])
    tpos = i * t_blk + lax.broadcasted_iota(I32, (t_blk, 1), 0)
    cnt = jnp.minimum(tpos + 1, win).astype(F32)
    dev = usum / cnt - u.astype(F32)
    y = _dot(dev.astype(BF16), w_ref[...]) * sc_ref[...]
    o_ref[...] = y.astype(o_ref.dtype)


def _multiscale_pool(qkvu, pool_w, pool_scale, *, col_off, t_blk=256):
    s = qkvu.shape[0]
    ng, gd, _ = pool_w.shape
    goff = col_off // gd
    hb = t_blk // POOL_HALO
    return pl.pallas_call(
        functools.partial(_pool_body, t_blk=t_blk),
        grid=(ng, s // t_blk),
        in_specs=[pl.BlockSpec((t_blk, gd), lambda g, i: (i, goff + g)),
                  pl.BlockSpec((POOL_HALO, gd), lambda g, i: (jnp.maximum(i * hb - 1, 0), goff + g)),
                  pl.BlockSpec((None, gd, gd), lambda g, i: (g, 0, 0)),
                  pl.BlockSpec((1, gd), lambda g, i: (0, g))],
        out_specs=pl.BlockSpec((t_blk, gd), lambda g, i: (i, g)),
        out_shape=jax.ShapeDtypeStruct((s, ng * gd), BF16),
        compiler_params=_cparams("arbitrary", "arbitrary"),
        name="multiscale_pool",
    )(qkvu, qkvu, pool_w, pool_scale.reshape(1, -1))


def _hgrn_tables():
    c = HGRN_CHUNK
    idx = np.arange(c)
    mats, masks = [], []
    h = c // 2
    while h >= 1:
        upper = (idx % (2 * h)) >= h
        e = idx - (idx % (2 * h)) + h - 1
        u = idx[None, :]
        pq = (upper[:, None] & (u > e[:, None]) & (u <= idx[:, None]))
        pk = ((~upper)[:, None] & (u > idx[:, None]) & (u <= e[:, None]))
        mats.append(pq | pk)
        masks.append(upper[:, None] & (~upper)[None, :] & ((idx[:, None] // (2 * h)) == (idx[None, :] // (2 * h))))
        h //= 2
    masks.append(idx[:, None] == idx[None, :])
    mats.append(idx[None, :] <= idx[:, None])
    mats.append(idx[None, :] > idx[:, None])
    return (np.concatenate(mats, axis=0).astype(np.float32),
            np.stack(masks, axis=0).astype(np.float32))


def _hgrn_body(q_ref, f_ref, v_ref, gs_ref, p_ref, mask_ref, gn_ref, o_ref, st_ref, *, t_blk, n_grp):
    c = HGRN_CHUNK
    n_lvl = mask_ref.shape[0] - 1
    heads = range(n_grp)

    @pl.when(pl.program_id(1) == 0)
    def _():
        st_ref[...] = jnp.zeros(st_ref.shape, F32)

    def col(x, hg):
        return x[:, hg * C_KDIM:(hg + 1) * C_KDIM]

    def stack(parts):
        return jnp.concatenate(parts, axis=0)

    def block_diag(parts):
        zero = jnp.zeros_like(parts[0])
        return stack([jnp.concatenate([parts[hg] if j == hg else zero for j in heads], axis=1) for hg in heads])

    def chunk(ci, carry):
        rows = pl.ds(pl.multiple_of(ci * c, c), c)
        f = f_ref[rows, :]
        lf = jnp.log(f)
        kk = 1.0 - f
        hi = lf.astype(BF16)
        r1 = lf - hi.astype(F32)
        mid = r1.astype(BF16)
        lo = (r1 - mid.astype(F32)).astype(BF16)
        ed = jnp.exp(_dot(p_ref[...], stack([hi, mid, lo])))
        e = [col(ed, hg) for hg in heads]
        q = q_ref[rows, :].astype(F32)
        qh = [col(q, hg) for hg in heads]
        kh = [col(kk, hg) for hg in heads]
        attn = mask_ref[n_lvl] * _dot_t(stack([x.astype(BF16) for x in qh]), stack([x.astype(BF16) for x in kh]))
        for lv in range(n_lvl):
            qs = stack([(qh[hg] * e[hg][lv * c:(lv + 1) * c]).astype(BF16) for hg in heads])
            ks = stack([(kh[hg] * e[hg][lv * c:(lv + 1) * c]).astype(BF16) for hg in heads])
            attn = attn + mask_ref[lv] * _dot_t(qs, ks)
        eb = [e[hg][n_lvl * c:(n_lvl + 1) * c] for hg in heads]
        el = [e[hg][(n_lvl + 1) * c:(n_lvl + 2) * c] for hg in heads]
        v = v_ref[rows, :]
        v_st = stack([col(v, hg) for hg in heads])
        st = st_ref[...]
        q_bd = block_diag([(qh[hg] * eb[hg]).astype(BF16) for hg in heads])
        o = _dot(attn.astype(BF16), v_st) + _dot_t(q_bd, st.astype(BF16))
        k_bd = block_diag([(kh[hg] * el[hg]).astype(BF16) for hg in heads])
        decay = jnp.concatenate([eb[hg][c - 1:c, :] for hg in heads], axis=1)
        st_ref[...] = st * decay + lax.dot_general(v_st, k_bd, (((0,), (0,)), ((), ())),
                                                   preferred_element_type=F32)
        ms = jnp.mean(o * o, axis=-1, keepdims=True)
        on = o * lax.rsqrt(ms + RMS_EPS) * gn_ref[...]
        gs = gs_ref[rows, :].astype(F32)
        for hg in heads:
            o_ref[rows, hg * C_VDIM:(hg + 1) * C_VDIM] = (on[hg * c:(hg + 1) * c] * col(gs, hg)).astype(o_ref.dtype)
        return carry

    lax.fori_loop(0, t_blk // c, chunk, 0, unroll=8)


def _hgrn2(qs, forget, v, gs, gnorm_g, *, t_blk=512, n_grp=4):
    s, d = qs.shape
    n_heads = d // C_KDIM
    p_np, mask_np = _hgrn_tables()
    p_mat = jnp.asarray(np.concatenate([p_np] * 3, axis=1), BF16)
    masks = jnp.asarray(np.stack([np.kron(np.eye(n_grp, dtype=np.float32), m) for m in mask_np]), F32)
    blk = lambda: pl.BlockSpec((t_blk, n_grp * C_KDIM), lambda h, i: (i, h))
    return pl.pallas_call(
        functools.partial(_hgrn_body, t_blk=t_blk, n_grp=n_grp),
        grid=(n_heads // n_grp, s // t_blk),
        in_specs=[blk(), blk(), blk(), blk(),
                  pl.BlockSpec(p_mat.shape, lambda h, i: (0, 0)),
                  pl.BlockSpec(masks.shape, lambda h, i: (0, 0, 0)),
                  pl.BlockSpec((1, C_VDIM), lambda h, i: (0, 0))],
        out_specs=blk(),
        out_shape=jax.ShapeDtypeStruct((s, d), BF16),
        scratch_shapes=[pltpu.VMEM((C_VDIM, n_grp * C_KDIM), F32)],
        compiler_params=_cparams("arbitrary", "arbitrary"),
        name="hgrn2",
    )(qs, forget, v, gs, p_mat, masks, gnorm_g.reshape(1, -1))


def _router_body(x_ref, sc_ref, sh_ref, whi_ref, wlo_ref, info_ref, cnt_ref, carry_s, *, tm):
    @pl.when(pl.program_id(0) == 0)
    def _():
        carry_s[...] = jnp.zeros(carry_s.shape, F32)

    hmod = x_ref[...] * (1.0 + sc_ref[...]) + sh_ref[...]
    hh = hmod.astype(BF16)
    hl = (hmod - hh.astype(F32)).astype(BF16)
    whi = whi_ref[...]
    logits = _dot(hh, whi) + _dot(hl, whi) + _dot(hh, wlo_ref[...])
    lane = lax.broadcasted_iota(I32, (tm, LANES), 1)
    logits = jnp.where(lane < N_EXPERTS, logits, -jnp.inf)
    m1 = jnp.max(logits, axis=-1, keepdims=True)
    e1 = jnp.min(jnp.where(logits == m1, lane, LANES), axis=-1, keepdims=True)
    rest = jnp.where(lane == e1, -jnp.inf, logits)
    m2 = jnp.max(rest, axis=-1, keepdims=True)
    e2 = jnp.min(jnp.where(rest == m2, lane, LANES), axis=-1, keepdims=True)
    ex = jnp.exp(m2 - m1)
    g1 = 1.0 / (1.0 + ex)
    g2 = ex / (1.0 + ex)
    onehot = jnp.where(jnp.logical_or(lane == e1, lane == e2), 1.0, 0.0)
    row = lax.broadcasted_iota(I32, (tm, tm), 0)
    col = lax.broadcasted_iota(I32, (tm, tm), 1)
    before = jnp.where(row > col, 1.0, 0.0).astype(BF16)
    cum = _dot(before, onehot.astype(BF16)) + carry_s[...]
    r1 = jnp.sum(jnp.where(lane == e1, cum, 0.0), axis=-1, keepdims=True)
    r2 = jnp.sum(jnp.where(lane == e2, cum, 0.0), axis=-1, keepdims=True)
    carry_s[...] = carry_s[...] + jnp.sum(onehot, axis=0, keepdims=True)
    cnt_ref[...] = carry_s[...]
    info = jnp.where(lane == 0, e1.astype(F32), 0.0)
    info = jnp.where(lane == 1, e2.astype(F32), info)
    info = jnp.where(lane == 2, g1, info)
    info = jnp.where(lane == 3, g2, info)
    info = jnp.where(lane == 4, r1, info)
    info = jnp.where(lane == 5, r2, info)
    info_ref[...] = info


def _router(x, scale, shift, router_w, *, tm=256):
    s, d = x.shape
    wpad = jnp.zeros((d, LANES), F32).at[:, :N_EXPERTS].set(router_w)
    whi = wpad.astype(BF16)
    wlo = (wpad - whi.astype(F32)).astype(BF16)
    vec = pl.BlockSpec((1, d), lambda i: (0, 0))
    wspec = pl.BlockSpec((d, LANES), lambda i: (0, 0))
    return pl.pallas_call(
        functools.partial(_router_body, tm=tm),
        grid=(s // tm,),
        in_specs=[pl.BlockSpec((tm, d), lambda i: (i, 0)), vec, vec, wspec, wspec],
        out_specs=[pl.BlockSpec((tm, LANES), lambda i: (i, 0)),
                   pl.BlockSpec((1, LANES), lambda i: (0, 0))],
        out_shape=[jax.ShapeDtypeStruct((s, LANES), F32), jax.ShapeDtypeStruct((1, LANES), F32)],
        scratch_shapes=[pltpu.VMEM((1, LANES), F32)],
        compiler_params=_cparams("arbitrary"),
        name="router",
    )(x, scale.reshape(1, d), shift.reshape(1, d), whi, wlo)


def _dispatch_copies(s1_ref, s2_ref, h_ref, out_ref, sem, base, r):
    t = base + r
    src = h_ref.at[pl.ds(r, 1)]
    return (pltpu.make_async_copy(src, out_ref.at[pl.ds(s1_ref[t], 1)], sem.at[0]),
            pltpu.make_async_copy(src, out_ref.at[pl.ds(s2_ref[t], 1)], sem.at[1]))


def _dispatch_body(s1_ref, s2_ref, h_ref, init_ref, out_ref, sem, *, tb):
    del init_ref
    base = pl.program_id(0) * tb

    def issue(r, carry):
        for prio, cp in enumerate(_dispatch_copies(s1_ref, s2_ref, h_ref, out_ref, sem, base, r)):
            cp.start(priority=prio)
        return carry

    def drain(r, carry):
        for cp in _dispatch_copies(s1_ref, s2_ref, h_ref, out_ref, sem, base, r):
            cp.wait()
        return carry

    lax.fori_loop(0, tb, issue, 0, unroll=DMA_ISSUE_UNROLL)
    lax.fori_loop(0, tb, drain, 0, unroll=DMA_ISSUE_UNROLL)


def _dispatch(h_words, slot1, slot2, n_rows, *, tb=256):
    s, dw = h_words.shape
    init = jnp.zeros((n_rows, dw), h_words.dtype)
    grid_spec = pltpu.PrefetchScalarGridSpec(
        num_scalar_prefetch=2,
        grid=(s // tb,),
        in_specs=[pl.BlockSpec((tb, dw), lambda i, a, b: (i, 0)),
                  pl.BlockSpec(memory_space=pl.ANY)],
        out_specs=pl.BlockSpec(memory_space=pl.ANY),
        scratch_shapes=[pltpu.SemaphoreType.DMA((2,))])
    return pl.pallas_call(
        functools.partial(_dispatch_body, tb=tb),
        grid_spec=grid_spec,
        out_shape=jax.ShapeDtypeStruct((n_rows, dw), h_words.dtype),
        input_output_aliases={3: 0},
        compiler_params=_cparams("arbitrary"),
        name="moe_dispatch",
    )(slot1, slot2, h_words, init)


def _expert_tile_changed(te_ref, i):
    return jnp.logical_or(i == 0, te_ref[i] != te_ref[jnp.maximum(i - 1, 0)])


def _expert_in_body(te_ref, nu_ref, x_ref, wa_ref, wb_ref, o_ref, wa_s, wb_s):
    i = pl.program_id(1)

    @pl.when(_expert_tile_changed(te_ref, i))
    def _():
        wa_s[...] = wa_ref[...].astype(BF16)
        wb_s[...] = wb_ref[...].astype(BF16)

    @pl.when(i < nu_ref[0])
    def _():
        x = _unpack_bf16_pairs(x_ref[...])
        o_ref[...] = (_silu(_dot(x, wa_s[...])) * _dot(x, wb_s[...])).astype(o_ref.dtype)

    @pl.when(i >= nu_ref[0])
    def _():
        o_ref[...] = jnp.zeros(o_ref.shape, o_ref.dtype)


def _expert_out_body(te_ref, nu_ref, x_ref, w_ref, o_ref, w_s):
    i = pl.program_id(1)

    @pl.when(_expert_tile_changed(te_ref, i))
    def _():
        w_s[...] = w_ref[...].astype(BF16)

    @pl.when(i < nu_ref[0])
    def _():
        o_ref[...] = _dot(x_ref[...], w_s[...]).astype(o_ref.dtype)

    @pl.when(i >= nu_ref[0])
    def _():
        o_ref[...] = jnp.zeros(o_ref.shape, o_ref.dtype)


def _expert_in(xs_words, w_in, tile_expert, n_used, *, tm, tn=512):
    p, kw = xs_words.shape
    k = 2 * kw
    half = w_in.shape[2] // 2
    hoff = half // tn
    row = lambda j, i, te, nu: (jnp.minimum(i, nu[0] - 1), 0)
    grid_spec = pltpu.PrefetchScalarGridSpec(
        num_scalar_prefetch=2,
        grid=(half // tn, p // tm),
        in_specs=[pl.BlockSpec((tm, kw), row),
                  pl.BlockSpec((None, k, tn), lambda j, i, te, nu: (te[i], 0, j)),
                  pl.BlockSpec((None, k, tn), lambda j, i, te, nu: (te[i], 0, j + hoff))],
        out_specs=pl.BlockSpec((tm, tn), lambda j, i, te, nu: (i, j)),
        scratch_shapes=[pltpu.VMEM((k, tn), BF16), pltpu.VMEM((k, tn), BF16)])
    return pl.pallas_call(
        _expert_in_body,
        grid_spec=grid_spec,
        out_shape=jax.ShapeDtypeStruct((p, half), BF16),
        compiler_params=_cparams("arbitrary", "arbitrary"),
        name="expert_in",
    )(tile_expert, n_used, xs_words, w_in, w_in)


def _expert_out(gs, w_out, tile_expert, n_used, *, tm, tn=1024):
    p, k = gs.shape
    n = w_out.shape[2]
    row = lambda j, i, te, nu: (jnp.minimum(i, nu[0] - 1), 0)
    grid_spec = pltpu.PrefetchScalarGridSpec(
        num_scalar_prefetch=2,
        grid=(n // tn, p // tm),
        in_specs=[pl.BlockSpec((tm, k), row),
                  pl.BlockSpec((None, k, tn), lambda j, i, te, nu: (te[i], 0, j))],
        out_specs=pl.BlockSpec((tm, tn), lambda j, i, te, nu: (i, j)),
        scratch_shapes=[pltpu.VMEM((k, tn), BF16)])
    return pl.pallas_call(
        _expert_out_body,
        grid_spec=grid_spec,
        out_shape=jax.ShapeDtypeStruct((p, n), F32),
        compiler_params=_cparams("arbitrary", "arbitrary"),
        name="expert_out",
    )(tile_expert, n_used, gs, w_out)


def _combine_copies(s1_ref, s2_ref, ys_ref, buf, sem, blk, slot, r, tb):
    t = blk * tb + r
    return (pltpu.make_async_copy(ys_ref.at[pl.ds(s1_ref[t], 1)], buf.at[slot, 0, pl.ds(r, 1)], sem.at[slot, 0]),
            pltpu.make_async_copy(ys_ref.at[pl.ds(s2_ref[t], 1)], buf.at[slot, 1, pl.ds(r, 1)], sem.at[slot, 1]))


def _combine_body(s1_ref, s2_ref, x_ref, info_ref, ys_ref, gate_ref, g_ref, b_ref, o_ref, buf, sem,
                  *, tb, alpha):
    i = pl.program_id(0)
    slot = lax.rem(i, 2)

    def gather(blk, slt):
        def issue(r, carry):
            for prio, cp in enumerate(_combine_copies(s1_ref, s2_ref, ys_ref, buf, sem, blk, slt, r, tb)):
                cp.start(priority=prio)
            return carry
        lax.fori_loop(0, tb, issue, 0, unroll=DMA_ISSUE_UNROLL)

    @pl.when(i == 0)
    def _():
        gather(i, slot)

    @pl.when(i + 1 < pl.num_programs(0))
    def _():
        gather(i + 1, 1 - slot)

    def drain(r, carry):
        for cp in _combine_copies(s1_ref, s2_ref, ys_ref, buf, sem, i, slot, r, tb):
            cp.wait()
        return carry

    lax.fori_loop(0, tb, drain, 0, unroll=DMA_ISSUE_UNROLL)
    info = info_ref[...]
    y = info[:, 2:3] * buf[slot, 0] + info[:, 3:4] * buf[slot, 1]
    r = alpha * x_ref[...] + (1.0 + gate_ref[...]) * y
    o_ref[...] = _layer_norm_rows(r, g_ref[...], b_ref[...])


def _combine_postnorm(x, info, ys, slot1, slot2, gate, g, b, *, alpha, tb=128):
    s, d = x.shape
    vec = pl.BlockSpec((1, d), lambda i, a, c: (0, 0))
    grid_spec = pltpu.PrefetchScalarGridSpec(
        num_scalar_prefetch=2,
        grid=(s // tb,),
        in_specs=[pl.BlockSpec((tb, d), lambda i, a, c: (i, 0)),
                  pl.BlockSpec((tb, LANES), lambda i, a, c: (i, 0)),
                  pl.BlockSpec(memory_space=pl.ANY),
                  vec, vec, vec],
        out_specs=pl.BlockSpec((tb, d), lambda i, a, c: (i, 0)),
        scratch_shapes=[pltpu.VMEM((2, 2, tb, d), F32), pltpu.SemaphoreType.DMA((2, 2))])
    return pl.pallas_call(
        functools.partial(_combine_body, tb=tb, alpha=alpha),
        grid_spec=grid_spec,
        out_shape=jax.ShapeDtypeStruct((s, d), F32),
        compiler_params=_cparams("arbitrary"),
        name="moe_combine_postnorm",
    )(slot1, slot2, x, info, ys, gate.reshape(1, d), g.reshape(1, d), b.reshape(1, d))


def _moe_plan(info, counts, *, tm, n_tiles):
    e1 = info[:, 0].astype(I32)
    e2 = info[:, 1].astype(I32)
    r1 = info[:, 4].astype(I32)
    r2 = info[:, 5].astype(I32)
    cnt = counts[0, :N_EXPERTS].astype(I32)
    tiles = (cnt + tm - 1) // tm
    tile_end = jnp.cumsum(tiles)
    row_off = (tile_end - tiles) * tm
    slot1 = row_off[e1] + r1
    slot2 = row_off[e2] + r2
    tile_expert = jnp.minimum(
        jnp.sum(jnp.arange(n_tiles, dtype=I32)[:, None] >= tile_end[None, :], axis=1),
        N_EXPERTS - 1).astype(I32)
    n_used = tile_end[-1:].astype(I32)
    return slot1, slot2, tile_expert, n_used


def kernel(x, c, ada_w, ada_b, ln_g, ln_b, even_w_in, lam_q1, lam_k1, lam_q2, lam_k2, subln_g, pool_w,
           pool_scale, even_w_out, ffn_w_in, ffn_w_out, odd_w_in, lb_raw, gnorm_g, odd_w_out, router_w,
           exp_w_in, exp_w_out):
    _, seq, d = x.shape
    depth = ada_w.shape[0]
    alpha = (2 * depth) ** 0.25
    a_width = d // 2
    a_heads = a_width // A_VDIM
    x2 = x.reshape(seq, d)

    lb_all = jnp.cumsum(jax.nn.softmax(lb_raw.astype(F32), axis=0), axis=0)
    lb_all = lb_all - lb_all[0]
    mod = _ada_mod(c, ada_w, ada_b)

    def mod_parts(l):
        return [mod[l, i * d:(i + 1) * d] for i in range(6)]

    sh1, sc1, g1, sh2, sc2, g2 = mod_parts(0)
    h = _modulate(x2, sc1, sh1)
    qkvu = _proj([h], even_w_in[0], col_off=0, n_cols=3 * a_width + (d - a_width), out_dtype=BF16,
                 epilogue="qscale", q_cols=a_width, q_scale=A_QKDIM ** -0.5 * math.log2(math.e),
                 name="even_in_proj")
    lam_init = 0.8 - 0.6 * math.exp(-0.3 * 0)
    v_t = _attn_value_rows(qkvu[:, 2 * a_width:3 * a_width], a_heads)
    o_a = _diff_attention(qkvu, v_t, lam_q1[0], lam_k1[0], lam_q2[0], lam_k2[0], subln_g[0],
                          n_heads=a_heads, lam_init=lam_init)
    o_b = _multiscale_pool(qkvu, pool_w[0].astype(BF16), pool_scale[0], col_off=3 * a_width)
    y = _proj([o_a, o_b], even_w_out[0], col_off=0, n_cols=d, out_dtype=BF16, name="even_out_proj")
    x2, h = _postnorm(x2, y, g1, ln_g[0, 0], ln_b[0, 0], sc2, sh2, alpha=alpha)

    ff = ffn_w_out.shape[1]
    gact = _swiglu_in(h, ffn_w_in[0])
    y = _mm_acc(gact, ffn_w_out[0], tk=ff // 2, out_dtype=BF16)
    sh1, sc1, g1n, sh2n, sc2n, g2n = mod_parts(1)
    x2, h = _postnorm(x2, y, g2, ln_g[0, 1], ln_b[0, 1], sc1, sh1, alpha=alpha)

    w_odd = odd_w_in[0]
    qs = _proj([h], w_odd, col_off=0, n_cols=d, out_dtype=BF16, epilogue="silu", name="odd_in_q")
    fg = _proj([h], w_odd, col_off=d, n_cols=d, out_dtype=F32, epilogue="forget", lb=lb_all[1], name="odd_in_f")
    vi = _proj([h], w_odd, col_off=2 * d, n_cols=d, out_dtype=BF16, name="odd_in_i")
    gs = _proj([h], w_odd, col_off=3 * d, n_cols=d, out_dtype=BF16, epilogue="silu", name="odd_in_g")
    o_c = _hgrn2(qs, fg, vi, gs, gnorm_g[0])
    y = _proj([o_c], odd_w_out[0], col_off=0, n_cols=d, out_dtype=BF16, name="odd_out_proj")
    x2, h_words = _postnorm(x2, y, g1n, ln_g[1, 0], ln_b[1, 0], sc2n, sh2n, alpha=alpha, pack=True)

    tm_e = 512
    n_tiles = (2 * seq) // tm_e + N_EXPERTS
    info, counts = _router(x2, sc2n, sh2n, router_w[0])
    slot1, slot2, tile_expert, n_used = _moe_plan(info, counts, tm=tm_e, n_tiles=n_tiles)
    xs_words = _dispatch(h_words, slot1, slot2, n_tiles * tm_e)
    gexp = _expert_in(xs_words, exp_w_in[0], tile_expert, n_used, tm=tm_e)
    ys = _expert_out(gexp, exp_w_out[0], tile_expert, n_used, tm=tm_e)
    out = _combine_postnorm(x2, info, ys, slot1, slot2, g2n, ln_g[1, 1], ln_b[1, 1], alpha=alpha)
    return out.reshape(x.shape)
```

```python
import functools
import math

import numpy as np
import jax
import jax.numpy as jnp
from jax import lax
from jax.experimental import pallas as pl
from jax.experimental.pallas import tpu as pltpu

F32 = jnp.float32
BF16 = jnp.bfloat16
I32 = jnp.int32

A_VDIM = 128
A_QKDIM = 64
POOL_WINDOWS = (2, 4, 8, 16)
POOL_HALO = 16
C_KDIM = 128
C_VDIM = 128
HGRN_CHUNK = 64
N_EXPERTS = 8
LN_EPS = 1e-5
RMS_EPS = 1e-6
NEG_INF = -1e30

LANES = 128
VMEM_LIMIT_BYTES = 56 * 1024 * 1024
DMA_ISSUE_UNROLL = 8


def _cparams(*sem):
    return pltpu.CompilerParams(dimension_semantics=sem, vmem_limit_bytes=VMEM_LIMIT_BYTES)


def _silu(x):
    return x * (1.0 / (1.0 + jnp.exp(-x)))


def _dot(a, b):
    return jnp.dot(a, b, preferred_element_type=F32)


def _dot_t(a, b):
    return lax.dot_general(a, b, (((1,), (1,)), ((), ())), preferred_element_type=F32)


def _pack_bf16_pairs(h):
    half = h.shape[1] // 2
    bits = pltpu.bitcast(h.astype(BF16).astype(F32), I32)
    return jnp.bitwise_or(bits[:, :half], lax.shift_right_logical(bits[:, half:], 16))


def _unpack_bf16_pairs(w):
    hi = pltpu.bitcast(jnp.bitwise_and(w, -65536), F32).astype(BF16)
    lo = pltpu.bitcast(lax.shift_left(w, 16), F32).astype(BF16)
    return jnp.concatenate([hi, lo], axis=1)


def _ada_body(c_ref, w_ref, b_ref, o_ref):
    ca = _silu(c_ref[...]).astype(BF16)
    o_ref[...] = _dot(ca, w_ref[...].astype(BF16)) + b_ref[...]


def _ada_mod(c, ada_w, ada_b, *, tn=1024):
    depth, d, n = ada_w.shape
    c8 = jnp.broadcast_to(c.astype(F32), (8, d))
    out = pl.pallas_call(
        _ada_body,
        grid=(depth, n // tn),
        in_specs=[pl.BlockSpec((8, d), lambda l, j: (0, 0)),
                  pl.BlockSpec((None, d, tn), lambda l, j: (l, 0, j)),
                  pl.BlockSpec((None, 1, tn), lambda l, j: (l, 0, j))],
        out_specs=pl.BlockSpec((None, 8, tn), lambda l, j: (l, 0, j)),
        out_shape=jax.ShapeDtypeStruct((depth, 8, n), F32),
        compiler_params=_cparams("arbitrary", "arbitrary"),
        name="ada_mod",
    )(c8, ada_w, ada_b.reshape(depth, 1, n))
    return out[:, 0, :]


def _modulate_body(x_ref, sc_ref, sh_ref, o_ref):
    o_ref[...] = (x_ref[...] * (1.0 + sc_ref[...]) + sh_ref[...]).astype(o_ref.dtype)


def _modulate(x, scale, shift, *, tm=512):
    s, d = x.shape
    vec = pl.BlockSpec((1, d), lambda i: (0, 0))
    return pl.pallas_call(
        _modulate_body,
        grid=(s // tm,),
        in_specs=[pl.BlockSpec((tm, d), lambda i: (i, 0)), vec, vec],
        out_specs=pl.BlockSpec((tm, d), lambda i: (i, 0)),
        out_shape=jax.ShapeDtypeStruct((s, d), BF16),
        compiler_params=_cparams("arbitrary"),
        name="modulate",
    )(x, scale.reshape(1, d), shift.reshape(1, d))


def _layer_norm_rows(r, g, b):
    mu = jnp.mean(r, axis=-1, keepdims=True)
    rc = r - mu
    var = jnp.mean(rc * rc, axis=-1, keepdims=True)
    return rc * lax.rsqrt(var + LN_EPS) * g + b


def _postnorm_body(x_ref, y_ref, gate_ref, g_ref, b_ref, sc_ref, sh_ref, xo_ref, ho_ref, *, alpha, pack):
    r = alpha * x_ref[...] + (1.0 + gate_ref[...]) * y_ref[...].astype(F32)
    xn = _layer_norm_rows(r, g_ref[...], b_ref[...])
    xo_ref[...] = xn
    hn = xn * (1.0 + sc_ref[...]) + sh_ref[...]
    ho_ref[...] = _pack_bf16_pairs(hn) if pack else hn.astype(ho_ref.dtype)


def _postnorm(x, y, gate, g, b, nscale, nshift, *, alpha, pack=False, tm=256):
    s, d = x.shape
    row = pl.BlockSpec((tm, d), lambda i: (i, 0))
    vec = pl.BlockSpec((1, d), lambda i: (0, 0))
    hd, hdt = (d // 2, I32) if pack else (d, BF16)
    return pl.pallas_call(
        functools.partial(_postnorm_body, alpha=alpha, pack=pack),
        grid=(s // tm,),
        in_specs=[row, row, vec, vec, vec, vec, vec],
        out_specs=[row, pl.BlockSpec((tm, hd), lambda i: (i, 0))],
        out_shape=[jax.ShapeDtypeStruct((s, d), F32), jax.ShapeDtypeStruct((s, hd), hdt)],
        compiler_params=_cparams("arbitrary"),
        name="postnorm",
    )(x, y, gate.reshape(1, d), g.reshape(1, d), b.reshape(1, d), nscale.reshape(1, d), nshift.reshape(1, d))


def _proj_body(*refs, n_a, epilogue, tn, q_cols, q_scale):
    a_refs, w_ref, extra = refs[:n_a], refs[n_a], refs[n_a + 1:-2]
    o_ref, wb_s = refs[-2], refs[-1]

    @pl.when(pl.program_id(1) == 0)
    def _():
        wb_s[...] = w_ref[...].astype(BF16)

    acc, off = None, 0
    for a_ref in a_refs:
        kp = a_ref.shape[1]
        part = _dot(a_ref[...], wb_s[off:off + kp, :])
        acc = part if acc is None else acc + part
        off += kp
    if epilogue == "silu":
        acc = _silu(acc)
    elif epilogue == "forget":
        lb = extra[0][...]
        acc = lb + (1.0 - lb) * (1.0 / (1.0 + jnp.exp(-acc)))
    elif epilogue == "qscale":
        acc = acc * jnp.where(pl.program_id(0) * tn < q_cols, q_scale, 1.0)
    o_ref[...] = acc.astype(o_ref.dtype)


def _proj(a_parts, w, *, col_off, n_cols, out_dtype, epilogue="id", lb=None, q_cols=0, q_scale=1.0,
          tm=1024, tn=512, name="proj"):
    m = a_parts[0].shape[0]
    k = w.shape[0]
    joff = col_off // tn
    in_specs = [pl.BlockSpec((tm, a.shape[1]), lambda j, i: (i, 0)) for a in a_parts]
    in_specs.append(pl.BlockSpec((k, tn), lambda j, i: (0, j + joff)))
    args = list(a_parts) + [w]
    if epilogue == "forget":
        in_specs.append(pl.BlockSpec((1, tn), lambda j, i: (0, j)))
        args.append(lb.reshape(1, n_cols))
    return pl.pallas_call(
        functools.partial(_proj_body, n_a=len(a_parts), epilogue=epilogue, tn=tn, q_cols=q_cols, q_scale=q_scale),
        grid=(n_cols // tn, m // tm),
        in_specs=in_specs,
        out_specs=pl.BlockSpec((tm, tn), lambda j, i: (i, j)),
        out_shape=jax.ShapeDtypeStruct((m, n_cols), out_dtype),
        scratch_shapes=[pltpu.VMEM((k, tn), BF16)],
        compiler_params=_cparams("arbitrary", "arbitrary"),
        name=name,
    )(*args)


def _swiglu_in_body(a_ref, wa_ref, wb_ref, o_ref, wa_s, wb_s):
    @pl.when(pl.program_id(1) == 0)
    def _():
        wa_s[...] = wa_ref[...].astype(BF16)
        wb_s[...] = wb_ref[...].astype(BF16)

    a = a_ref[...]
    o_ref[...] = (_silu(_dot(a, wa_s[...])) * _dot(a, wb_s[...])).astype(o_ref.dtype)


def _swiglu_in(a, w, *, tm=1024, tn=256):
    m, k = a.shape
    half = w.shape[1] // 2
    hoff = half // tn
    return pl.pallas_call(
        _swiglu_in_body,
        grid=(half // tn, m // tm),
        in_specs=[pl.BlockSpec((tm, k), lambda j, i: (i, 0)),
                  pl.BlockSpec((k, tn), lambda j, i: (0, j)),
                  pl.BlockSpec((k, tn), lambda j, i: (0, j + hoff))],
        out_specs=pl.BlockSpec((tm, tn), lambda j, i: (i, j)),
        out_shape=jax.ShapeDtypeStruct((m, half), BF16),
        scratch_shapes=[pltpu.VMEM((k, tn), BF16), pltpu.VMEM((k, tn), BF16)],
        compiler_params=_cparams("arbitrary", "arbitrary"),
        name="swiglu_in",
    )(a, w, w)


def _mm_acc_body(a_ref, w_ref, o_ref, acc_ref, *, nk):
    kk = pl.program_id(2)
    part = _dot(a_ref[...], w_ref[...].astype(BF16))

    @pl.when(kk == 0)
    def _():
        acc_ref[...] = part

    @pl.when(kk > 0)
    def _():
        acc_ref[...] += part

    @pl.when(kk == nk - 1)
    def _():
        o_ref[...] = acc_ref[...].astype(o_ref.dtype)


def _mm_acc(a, w, *, tm=1024, tn=512, tk, out_dtype=F32):
    m, k = a.shape
    n = w.shape[1]
    nk = k // tk
    return pl.pallas_call(
        functools.partial(_mm_acc_body, nk=nk),
        grid=(m // tm, n // tn, nk),
        in_specs=[pl.BlockSpec((tm, tk), lambda i, j, kk: (i, kk)),
                  pl.BlockSpec((tk, tn), lambda i, j, kk: (kk, j))],
        out_specs=pl.BlockSpec((tm, tn), lambda i, j, kk: (i, j)),
        out_shape=jax.ShapeDtypeStruct((m, n), out_dtype),
        scratch_shapes=[pltpu.VMEM((tm, tn), F32)],
        compiler_params=_cparams("arbitrary", "arbitrary", "arbitrary"),
        name="mm_acc",
    )(a, w)


ATTN_ONES_ROWS = 16
ATTN_SLAB = 512
ATTN_KEY_ROWS = 512


def _attn_body(qi_tab, kv_tab, slopes, q_ref, k_ref, vt_ref, pos_ref, lq1_ref, lk1_ref, lq2_ref, lk2_ref, g_ref,
               o_ref, q1_s, q2_s, m1_s, a1_s, m2_s, a2_s, jmi_s, *, tq, tk, lam_init):
    h = pl.program_id(0)
    t = pl.program_id(1)
    qi = qi_tab[t]
    kv = kv_tab[t]
    slope = slopes[h]
    n_sub = tq // tk
    dv = A_VDIM
    ks = min(ATTN_KEY_ROWS, tk)
    slab = min(ATTN_SLAB, tq)

    @pl.when(t == 0)
    def _():
        jmi_s[...] = lax.broadcasted_iota(I32, (ks, tq), 0) - lax.broadcasted_iota(I32, (ks, tq), 1)

    @pl.when(kv == 0)
    def _():
        q = q_ref[...]
        lane = lax.broadcasted_iota(I32, q.shape, 1)
        zero = jnp.zeros_like(q)
        sv = jnp.full(q.shape, slope, F32)
        s_hi = sv.astype(BF16).astype(F32)
        s_lo = sv - s_hi
        coef = jnp.where(lane == 0, s_hi * 16.0, jnp.where(lane == 1, s_lo * 16.0,
                         jnp.where(lane == 2, s_hi, jnp.where(lane == 3, s_lo, 0.0)))).astype(BF16)
        q1_s[...] = jnp.concatenate([jnp.where(lane < A_QKDIM, q, zero), coef], axis=1)
        q2_s[...] = jnp.concatenate([jnp.where(lane >= A_QKDIM, q, zero), coef], axis=1)
        for m_s, a_s in ((m1_s, a1_s), (m2_s, a2_s)):
            m_s[...] = jnp.full(m_s.shape, -jnp.inf, F32)
            a_s[...] = jnp.zeros(a_s.shape, F32)

    shift = kv * tk - qi * tq

    def step(tile_off):
        for so in range(0, tk, ks):
            substep(None if tile_off is None else tile_off + so, so)

    def substep(key_off, so):
        k_aug = jnp.concatenate([k_ref[so:so + ks, :], pos_ref[...]], axis=1)
        vt = vt_ref[:, so:so + ks]
        c = slope * (shift + so).astype(F32)
        chains = []
        for c0 in range(0, tq, slab):
            if key_off is not None and c0 + slab - 1 < key_off:
                continue
            masked = key_off is not None and c0 < key_off + ks - 1
            for q_s, m_s, a_s in ((q1_s, m1_s, a1_s), (q2_s, m2_s, a2_s)):
                chains.append((q_s, m_s, a_s, slice(c0, c0 + slab), masked))
        scores = [_dot_t(k_aug, q_s[cols, :]) for q_s, _, _, cols, _ in chains]
        for s, (_, m_s, a_s, cols, masked) in zip(scores, chains):
            if masked:
                s = jnp.where(jmi_s[:, cols] <= -key_off, s, NEG_INF)
            m_prev = m_s[:, cols]
            m_new = jnp.maximum(m_prev, jnp.max(s, axis=0, keepdims=True) + c)
            p = jnp.exp2(s - (m_new - c))
            a_s[:, cols] = jnp.exp2(m_prev - m_new) * a_s[:, cols] + _dot(vt, p.astype(BF16))
            m_s[:, cols] = m_new

    @pl.when(kv < qi * n_sub)
    def _():
        step(None)

    for r in range(n_sub):
        @pl.when(kv == qi * n_sub + r)
        def _():
            step(r * tk)

    @pl.when(kv == (qi + 1) * n_sub - 1)
    def _():
        lam = (jnp.exp(jnp.sum(lq1_ref[...] * lk1_ref[...], axis=-1, keepdims=True))
               - jnp.exp(jnp.sum(lq2_ref[...] * lk2_ref[...], axis=-1, keepdims=True)) + lam_init)
        a1 = a1_s[...]
        a2 = a2_s[...]
        o = a1[:dv] / a1[dv:dv + 1] - lam * (a2[:dv] / a2[dv:dv + 1])
        ms = jnp.mean(o * o, axis=0, keepdims=True)
        on = o * lax.rsqrt(ms + RMS_EPS) * g_ref[...] * (1.0 - lam_init)
        o_ref[...] = on.T.astype(o_ref.dtype)


def _diff_attention(qkvu, v_t, lam_q1, lam_k1, lam_q2, lam_k2, subln_g, *, n_heads, lam_init, tq=2048, tk=1024):
    s = qkvu.shape[0]
    n_sub = tq // tk
    pairs = [(i, j) for i in range(s // tq) for j in range((i + 1) * n_sub)]
    qi_tab = jnp.asarray([p[0] for p in pairs], I32)
    kv_tab = jnp.asarray([p[1] for p in pairs], I32)
    slopes = jnp.asarray([math.log2(math.e) * 2.0 ** (-8.0 * (i + 1) / n_heads) for i in range(n_heads)], F32)
    hd = A_VDIM
    hv = hd + ATTN_ONES_ROWS
    ks = min(ATTN_KEY_ROWS, tk)
    j = np.arange(ks)
    pos = np.zeros((ks, hd), np.float32)
    pos[:, 0] = pos[:, 1] = j // 16
    pos[:, 2] = pos[:, 3] = j % 16
    small = pl.BlockSpec((1, A_QKDIM), lambda h, t, qt, kt, sl: (0, 0))
    grid_spec = pltpu.PrefetchScalarGridSpec(
        num_scalar_prefetch=3,
        grid=(n_heads, len(pairs)),
        in_specs=[pl.BlockSpec((tq, hd), lambda h, t, qt, kt, sl: (qt[t], h)),
                  pl.BlockSpec((tk, hd), lambda h, t, qt, kt, sl: (kt[t], n_heads + h)),
                  pl.BlockSpec((hv, tk), lambda h, t, qt, kt, sl: (h, kt[t])),
                  pl.BlockSpec((ks, hd), lambda h, t, qt, kt, sl: (0, 0)),
                  small, small, small, small,
                  pl.BlockSpec((hd, 1), lambda h, t, qt, kt, sl: (0, 0))],
        out_specs=pl.BlockSpec((tq, hd), lambda h, t, qt, kt, sl: (qt[t], h)),
        scratch_shapes=[pltpu.VMEM((tq, 2 * hd), BF16), pltpu.VMEM((tq, 2 * hd), BF16),
                        pltpu.VMEM((1, tq), F32), pltpu.VMEM((hv, tq), F32),
                        pltpu.VMEM((1, tq), F32), pltpu.VMEM((hv, tq), F32),
                        pltpu.VMEM((ks, tq), I32)])
    return pl.pallas_call(
        functools.partial(_attn_body, tq=tq, tk=tk, lam_init=lam_init),
        grid_spec=grid_spec,
        out_shape=jax.ShapeDtypeStruct((s, n_heads * hd), BF16),
        compiler_params=_cparams("arbitrary", "arbitrary"),
        name="diff_attention",
    )(qi_tab, kv_tab, slopes, qkvu, qkvu, v_t, jnp.asarray(pos, BF16),
      lam_q1.reshape(1, -1), lam_k1.reshape(1, -1), lam_q2.reshape(1, -1), lam_k2.reshape(1, -1),
      subln_g.reshape(-1, 1))


def _attn_value_rows(v, n_heads):
    s = v.shape[0]
    vt = v.T.reshape(n_heads, A_VDIM, s)
    ones = jnp.ones((n_heads, ATTN_ONES_ROWS, s), v.dtype)
    return jnp.concatenate([vt, ones], axis=1).reshape(n_heads * (A_VDIM + ATTN_ONES_ROWS), s)


def _pool_body(ucur_ref, uprev_ref, w_ref, sc_ref, o_ref, *, t_blk):
    g = pl.program_id(0)
    i = pl.program_id(1)
    win = jnp.left_shift(2, g)
    row = lax.broadcasted_iota(I32, (t_blk, t_blk), 0)
    col = lax.broadcasted_iota(I32, (t_blk, t_blk), 1)
    d = row - col
    band = jnp.where(jnp.logical_and(d >= 0, d < win), 1.0, 0.0).astype(BF16)
    rowp = lax.broadcasted_iota(I32, (t_blk, POOL_HALO), 0)
    colp = lax.broadcasted_iota(I32, (t_blk, POOL_HALO), 1)
    dp = rowp + POOL_HALO - colp
    bandp = jnp.where(jnp.logical_and(dp < win, i > 0), 1.0, 0.0).astype(BF16)
    u = ucur_ref[...]
    usum = _dot(band, u) + _dot(bandp, uprev_ref[...])
    tpos = i * t_blk + lax.broadcasted_iota(I32, (t_blk, 1), 0)
    cnt = jnp.minimum(tpos + 1, win).astype(F32)
    dev = usum / cnt - u.astype(F32)
    y = _dot(dev.astype(BF16), w_ref[...]) * sc_ref[...]
    o_ref[...] = y.astype(o_ref.dtype)


def _multiscale_pool(qkvu, pool_w, pool_scale, *, col_off, t_blk=256):
    s = qkvu.shape[0]
    ng, gd, _ = pool_w.shape
    goff = col_off // gd
    hb = t_blk // POOL_HALO
    return pl.pallas_call(
        functools.partial(_pool_body, t_blk=t_blk),
        grid=(ng, s // t_blk),
        in_specs=[pl.BlockSpec((t_blk, gd), lambda g, i: (i, goff + g)),
                  pl.BlockSpec((POOL_HALO, gd), lambda g, i: (jnp.maximum(i * hb - 1, 0), goff + g)),
                  pl.BlockSpec((None, gd, gd), lambda g, i: (g, 0, 0)),
                  pl.BlockSpec((1, gd), lambda g, i: (0, g))],
        out_specs=pl.BlockSpec((t_blk, gd), lambda g, i: (i, g)),
        out_shape=jax.ShapeDtypeStruct((s, ng * gd), BF16),
        compiler_params=_cparams("arbitrary", "arbitrary"),
        name="multiscale_pool",
    )(qkvu, qkvu, pool_w, pool_scale.reshape(1, -1))


def _hgrn_tables():
    c = HGRN_CHUNK
    idx = np.arange(c)
    mats, masks = [], []
    h = c // 2
    while h >= 1:
        upper = (idx % (2 * h)) >= h
        e = idx - (idx % (2 * h)) + h - 1
        u = idx[None, :]
        pq = (upper[:, None] & (u > e[:, None]) & (u <= idx[:, None]))
        pk = ((~upper)[:, None] & (u > idx[:, None]) & (u <= e[:, None]))
        mats.append(pq | pk)
        masks.append(upper[:, None] & (~upper)[None, :] & ((idx[:, None] // (2 * h)) == (idx[None, :] // (2 * h))))
        h //= 2
    masks.append(idx[:, None] == idx[None, :])
    mats.append(idx[None, :] <= idx[:, None])
    mats.append(idx[None, :] > idx[:, None])
    return (np.concatenate(mats, axis=0).astype(np.float32),
            np.stack(masks, axis=0).astype(np.float32))


def _hgrn_body(q_ref, f_ref, v_ref, gs_ref, p_ref, mask_ref, gn_ref, o_ref, st_ref, *, t_blk, n_grp):
    c = HGRN_CHUNK
    n_lvl = mask_ref.shape[0] - 1
    heads = range(n_grp)

    @pl.when(pl.program_id(1) == 0)
    def _():
        st_ref[...] = jnp.zeros(st_ref.shape, F32)

    def col(x, hg):
        return x[:, hg * C_KDIM:(hg + 1) * C_KDIM]

    def stack(parts):
        return jnp.concatenate(parts, axis=0)

    def block_diag(parts):
        zero = jnp.zeros_like(parts[0])
        return stack([jnp.concatenate([parts[hg] if j == hg else zero for j in heads], axis=1) for hg in heads])

    def chunk(ci, carry):
        rows = pl.ds(pl.multiple_of(ci * c, c), c)
        f = f_ref[rows, :]
        lf = jnp.log(f)
        kk = 1.0 - f
        hi = lf.astype(BF16)
        r1 = lf - hi.astype(F32)
        mid = r1.astype(BF16)
        lo = (r1 - mid.astype(F32)).astype(BF16)
        ed = jnp.exp(_dot(p_ref[...], stack([hi, mid, lo])))
        e = [col(ed, hg) for hg in heads]
        q = q_ref[rows, :].astype(F32)
        qh = [col(q, hg) for hg in heads]
        kh = [col(kk, hg) for hg in heads]
        attn = mask_ref[n_lvl] * _dot_t(stack([x.astype(BF16) for x in qh]), stack([x.astype(BF16) for x in kh]))
        for lv in range(n_lvl):
            qs = stack([(qh[hg] * e[hg][lv * c:(lv + 1) * c]).astype(BF16) for hg in heads])
            ks = stack([(kh[hg] * e[hg][lv * c:(lv + 1) * c]).astype(BF16) for hg in heads])
            attn = attn + mask_ref[lv] * _dot_t(qs, ks)
        eb = [e[hg][n_lvl * c:(n_lvl + 1) * c] for hg in heads]
        el = [e[hg][(n_lvl + 1) * c:(n_lvl + 2) * c] for hg in heads]
        v = v_ref[rows, :]
        v_st = stack([col(v, hg) for hg in heads])
        st = st_ref[...]
        q_bd = block_diag([(qh[hg] * eb[hg]).astype(BF16) for hg in heads])
        o = _dot(attn.astype(BF16), v_st) + _dot_t(q_bd, st.astype(BF16))
        k_bd = block_diag([(kh[hg] * el[hg]).astype(BF16) for hg in heads])
        decay = jnp.concatenate([eb[hg][c - 1:c, :] for hg in heads], axis=1)
        st_ref[...] = st * decay + lax.dot_general(v_st, k_bd, (((0,), (0,)), ((), ())),
                                                   preferred_element_type=F32)
        ms = jnp.mean(o * o, axis=-1, keepdims=True)
        on = o * lax.rsqrt(ms + RMS_EPS) * gn_ref[...]
        gs = gs_ref[rows, :].astype(F32)
        for hg in heads:
            o_ref[rows, hg * C_VDIM:(hg + 1) * C_VDIM] = (on[hg * c:(hg + 1) * c] * col(gs, hg)).astype(o_ref.dtype)
        return carry

    lax.fori_loop(0, t_blk // c, chunk, 0, unroll=8)


def _hgrn2(qs, forget, v, gs, gnorm_g, *, t_blk=512, n_grp=4):
    s, d = qs.shape
    n_heads = d // C_KDIM
    p_np, mask_np = _hgrn_tables()
    p_mat = jnp.asarray(np.concatenate([p_np] * 3, axis=1), BF16)
    masks = jnp.asarray(np.stack([np.kron(np.eye(n_grp, dtype=np.float32), m) for m in mask_np]), F32)
    blk = lambda: pl.BlockSpec((t_blk, n_grp * C_KDIM), lambda h, i: (i, h))
    return pl.pallas_call(
        functools.partial(_hgrn_body, t_blk=t_blk, n_grp=n_grp),
        grid=(n_heads // n_grp, s // t_blk),
        in_specs=[blk(), blk(), blk(), blk(),
                  pl.BlockSpec(p_mat.shape, lambda h, i: (0, 0)),
                  pl.BlockSpec(masks.shape, lambda h, i: (0, 0, 0)),
                  pl.BlockSpec((1, C_VDIM), lambda h, i: (0, 0))],
        out_specs=blk(),
        out_shape=jax.ShapeDtypeStruct((s, d), BF16),
        scratch_shapes=[pltpu.VMEM((C_VDIM, n_grp * C_KDIM), F32)],
        compiler_params=_cparams("arbitrary", "arbitrary"),
        name="hgrn2",
    )(qs, forget, v, gs, p_mat, masks, gnorm_g.reshape(1, -1))


def _router_body(x_ref, sc_ref, sh_ref, whi_ref, wlo_ref, info_ref, cnt_ref, carry_s, *, tm):
    @pl.when(pl.program_id(0) == 0)
    def _():
        carry_s[...] = jnp.zeros(carry_s.shape, F32)

    hmod = x_ref[...] * (1.0 + sc_ref[...]) + sh_ref[...]
    hh = hmod.astype(BF16)
    hl = (hmod - hh.astype(F32)).astype(BF16)
    whi = whi_ref[...]
    logits = _dot(hh, whi) + _dot(hl, whi) + _dot(hh, wlo_ref[...])
    lane = lax.broadcasted_iota(I32, (tm, LANES), 1)
    logits = jnp.where(lane < N_EXPERTS, logits, -jnp.inf)
    m1 = jnp.max(logits, axis=-1, keepdims=True)
    e1 = jnp.min(jnp.where(logits == m1, lane, LANES), axis=-1, keepdims=True)
    rest = jnp.where(lane == e1, -jnp.inf, logits)
    m2 = jnp.max(rest, axis=-1, keepdims=True)
    e2 = jnp.min(jnp.where(rest == m2, lane, LANES), axis=-1, keepdims=True)
    ex = jnp.exp(m2 - m1)
    g1 = 1.0 / (1.0 + ex)
    g2 = ex / (1.0 + ex)
    onehot = jnp.where(jnp.logical_or(lane == e1, lane == e2), 1.0, 0.0)
    row = lax.broadcasted_iota(I32, (tm, tm), 0)
    col = lax.broadcasted_iota(I32, (tm, tm), 1)
    before = jnp.where(row > col, 1.0, 0.0).astype(BF16)
    cum = _dot(before, onehot.astype(BF16)) + carry_s[...]
    r1 = jnp.sum(jnp.where(lane == e1, cum, 0.0), axis=-1, keepdims=True)
    r2 = jnp.sum(jnp.where(lane == e2, cum, 0.0), axis=-1, keepdims=True)
    carry_s[...] = carry_s[...] + jnp.sum(onehot, axis=0, keepdims=True)
    cnt_ref[...] = carry_s[...]
    info = jnp.where(lane == 0, e1.astype(F32), 0.0)
    info = jnp.where(lane == 1, e2.astype(F32), info)
    info = jnp.where(lane == 2, g1, info)
    info = jnp.where(lane == 3, g2, info)
    info = jnp.where(lane == 4, r1, info)
    info = jnp.where(lane == 5, r2, info)
    info_ref[...] = info


def _router(x, scale, shift, router_w, *, tm=256):
    s, d = x.shape
    wpad = jnp.zeros((d, LANES), F32).at[:, :N_EXPERTS].set(router_w)
    whi = wpad.astype(BF16)
    wlo = (wpad - whi.astype(F32)).astype(BF16)
    vec = pl.BlockSpec((1, d), lambda i: (0, 0))
    wspec = pl.BlockSpec((d, LANES), lambda i: (0, 0))
    return pl.pallas_call(
        functools.partial(_router_body, tm=tm),
        grid=(s // tm,),
        in_specs=[pl.BlockSpec((tm, d), lambda i: (i, 0)), vec, vec, wspec, wspec],
        out_specs=[pl.BlockSpec((tm, LANES), lambda i: (i, 0)),
                   pl.BlockSpec((1, LANES), lambda i: (0, 0))],
        out_shape=[jax.ShapeDtypeStruct((s, LANES), F32), jax.ShapeDtypeStruct((1, LANES), F32)],
        scratch_shapes=[pltpu.VMEM((1, LANES), F32)],
        compiler_params=_cparams("arbitrary"),
        name="router",
    )(x, scale.reshape(1, d), shift.reshape(1, d), whi, wlo)


def _dispatch_copies(s1_ref, s2_ref, h_ref, out_ref, sem, base, r):
    t = base + r
    src = h_ref.at[pl.ds(r, 1)]
    return (pltpu.make_async_copy(src, out_ref.at[pl.ds(s1_ref[t], 1)], sem.at[0]),
            pltpu.make_async_copy(src, out_ref.at[pl.ds(s2_ref[t], 1)], sem.at[1]))


def _dispatch_body(s1_ref, s2_ref, h_ref, init_ref, out_ref, sem, *, tb):
    del init_ref
    base = pl.program_id(0) * tb

    def issue(r, carry):
        for prio, cp in enumerate(_dispatch_copies(s1_ref, s2_ref, h_ref, out_ref, sem, base, r)):
            cp.start(priority=prio)
        return carry

    def drain(r, carry):
        for cp in _dispatch_copies(s1_ref, s2_ref, h_ref, out_ref, sem, base, r):
            cp.wait()
        return carry

    lax.fori_loop(0, tb, issue, 0, unroll=DMA_ISSUE_UNROLL)
    lax.fori_loop(0, tb, drain, 0, unroll=DMA_ISSUE_UNROLL)


def _dispatch(h_words, slot1, slot2, n_rows, *, tb=256):
    s, dw = h_words.shape
    init = jnp.zeros((n_rows, dw), h_words.dtype)
    grid_spec = pltpu.PrefetchScalarGridSpec(
        num_scalar_prefetch=2,
        grid=(s // tb,),
        in_specs=[pl.BlockSpec((tb, dw), lambda i, a, b: (i, 0)),
                  pl.BlockSpec(memory_space=pl.ANY)],
        out_specs=pl.BlockSpec(memory_space=pl.ANY),
        scratch_shapes=[pltpu.SemaphoreType.DMA((2,))])
    return pl.pallas_call(
        functools.partial(_dispatch_body, tb=tb),
        grid_spec=grid_spec,
        out_shape=jax.ShapeDtypeStruct((n_rows, dw), h_words.dtype),
        input_output_aliases={3: 0},
        compiler_params=_cparams("arbitrary"),
        name="moe_dispatch",
    )(slot1, slot2, h_words, init)


def _expert_tile_changed(te_ref, i):
    return jnp.logical_or(i == 0, te_ref[i] != te_ref[jnp.maximum(i - 1, 0)])


def _expert_in_body(te_ref, nu_ref, x_ref, wa_ref, wb_ref, o_ref, wa_s, wb_s):
    i = pl.program_id(1)

    @pl.when(_expert_tile_changed(te_ref, i))
    def _():
        wa_s[...] = wa_ref[...].astype(BF16)
        wb_s[...] = wb_ref[...].astype(BF16)

    @pl.when(i < nu_ref[0])
    def _():
        x = _unpack_bf16_pairs(x_ref[...])
        o_ref[...] = (_silu(_dot(x, wa_s[...])) * _dot(x, wb_s[...])).astype(o_ref.dtype)

    @pl.when(i >= nu_ref[0])
    def _():
        o_ref[...] = jnp.zeros(o_ref.shape, o_ref.dtype)


def _expert_out_body(te_ref, nu_ref, x_ref, w_ref, o_ref, w_s):
    i = pl.program_id(1)

    @pl.when(_expert_tile_changed(te_ref, i))
    def _():
        w_s[...] = w_ref[...].astype(BF16)

    @pl.when(i < nu_ref[0])
    def _():
        o_ref[...] = _dot(x_ref[...], w_s[...]).astype(o_ref.dtype)

    @pl.when(i >= nu_ref[0])
    def _():
        o_ref[...] = jnp.zeros(o_ref.shape, o_ref.dtype)


def _expert_in(xs_words, w_in, tile_expert, n_used, *, tm, tn=512):
    p, kw = xs_words.shape
    k = 2 * kw
    half = w_in.shape[2] // 2
    hoff = half // tn
    row = lambda j, i, te, nu: (jnp.minimum(i, nu[0] - 1), 0)
    grid_spec = pltpu.PrefetchScalarGridSpec(
        num_scalar_prefetch=2,
        grid=(half // tn, p // tm),
        in_specs=[pl.BlockSpec((tm, kw), row),
                  pl.BlockSpec((None, k, tn), lambda j, i, te, nu: (te[i], 0, j)),
                  pl.BlockSpec((None, k, tn), lambda j, i, te, nu: (te[i], 0, j + hoff))],
        out_specs=pl.BlockSpec((tm, tn), lambda j, i, te, nu: (i, j)),
        scratch_shapes=[pltpu.VMEM((k, tn), BF16), pltpu.VMEM((k, tn), BF16)])
    return pl.pallas_call(
        _expert_in_body,
        grid_spec=grid_spec,
        out_shape=jax.ShapeDtypeStruct((p, half), BF16),
        compiler_params=_cparams("arbitrary", "arbitrary"),
        name="expert_in",
    )(tile_expert, n_used, xs_words, w_in, w_in)


def _expert_out(gs, w_out, tile_expert, n_used, *, tm, tn=1024):
    p, k = gs.shape
    n = w_out.shape[2]
    row = lambda j, i, te, nu: (jnp.minimum(i, nu[0] - 1), 0)
    grid_spec = pltpu.PrefetchScalarGridSpec(
        num_scalar_prefetch=2,
        grid=(n // tn, p // tm),
        in_specs=[pl.BlockSpec((tm, k), row),
                  pl.BlockSpec((None, k, tn), lambda j, i, te, nu: (te[i], 0, j))],
        out_specs=pl.BlockSpec((tm, tn), lambda j, i, te, nu: (i, j)),
        scratch_shapes=[pltpu.VMEM((k, tn), BF16)])
    return pl.pallas_call(
        _expert_out_body,
        grid_spec=grid_spec,
        out_shape=jax.ShapeDtypeStruct((p, n), F32),
        compiler_params=_cparams("arbitrary", "arbitrary"),
        name="expert_out",
    )(tile_expert, n_used, gs, w_out)


def _combine_copies(s1_ref, s2_ref, ys_ref, buf, sem, blk, slot, r, tb):
    t = blk * tb + r
    return (pltpu.make_async_copy(ys_ref.at[pl.ds(s1_ref[t], 1)], buf.at[slot, 0, pl.ds(r, 1)], sem.at[slot, 0]),
            pltpu.make_async_copy(ys_ref.at[pl.ds(s2_ref[t], 1)], buf.at[slot, 1, pl.ds(r, 1)], sem.at[slot, 1]))


def _combine_body(s1_ref, s2_ref, x_ref, info_ref, ys_ref, gate_ref, g_ref, b_ref, o_ref, buf, sem,
                  *, tb, alpha):
    i = pl.program_id(0)
    slot = lax.rem(i, 2)

    def gather(blk, slt):
        def issue(r, carry):
            for prio, cp in enumerate(_combine_copies(s1_ref, s2_ref, ys_ref, buf, sem, blk, slt, r, tb)):
                cp.start(priority=prio)
            return carry
        lax.fori_loop(0, tb, issue, 0, unroll=DMA_ISSUE_UNROLL)

    @pl.when(i == 0)
    def _():
        gather(i, slot)

    @pl.when(i + 1 < pl.num_programs(0))
    def _():
        gather(i + 1, 1 - slot)

    def drain(r, carry):
        for cp in _combine_copies(s1_ref, s2_ref, ys_ref, buf, sem, i, slot, r, tb):
            cp.wait()
        return carry

    lax.fori_loop(0, tb, drain, 0, unroll=DMA_ISSUE_UNROLL)
    info = info_ref[...]
    y = info[:, 2:3] * buf[slot, 0] + info[:, 3:4] * buf[slot, 1]
    r = alpha * x_ref[...] + (1.0 + gate_ref[...]) * y
    o_ref[...] = _layer_norm_rows(r, g_ref[...], b_ref[...])


def _combine_postnorm(x, info, ys, slot1, slot2, gate, g, b, *, alpha, tb=128):
    s, d = x.shape
    vec = pl.BlockSpec((1, d), lambda i, a, c: (0, 0))
    grid_spec = pltpu.PrefetchScalarGridSpec(
        num_scalar_prefetch=2,
        grid=(s // tb,),
        in_specs=[pl.BlockSpec((tb, d), lambda i, a, c: (i, 0)),
                  pl.BlockSpec((tb, LANES), lambda i, a, c: (i, 0)),
                  pl.BlockSpec(memory_space=pl.ANY),
                  vec, vec, vec],
        out_specs=pl.BlockSpec((tb, d), lambda i, a, c: (i, 0)),
        scratch_shapes=[pltpu.VMEM((2, 2, tb, d), F32), pltpu.SemaphoreType.DMA((2, 2))])
    return pl.pallas_call(
        functools.partial(_combine_body, tb=tb, alpha=alpha),
        grid_spec=grid_spec,
        out_shape=jax.ShapeDtypeStruct((s, d), F32),
        compiler_params=_cparams("arbitrary"),
        name="moe_combine_postnorm",
    )(slot1, slot2, x, info, ys, gate.reshape(1, d), g.reshape(1, d), b.reshape(1, d))


def _moe_plan(info, counts, *, tm, n_tiles):
    e1 = info[:, 0].astype(I32)
    e2 = info[:, 1].astype(I32)
    r1 = info[:, 4].astype(I32)
    r2 = info[:, 5].astype(I32)
    cnt = counts[0, :N_EXPERTS].astype(I32)
    tiles = (cnt + tm - 1) // tm
    tile_end = jnp.cumsum(tiles)
    row_off = (tile_end - tiles) * tm
    slot1 = row_off[e1] + r1
    slot2 = row_off[e2] + r2
    tile_expert = jnp.minimum(
        jnp.sum(jnp.arange(n_tiles, dtype=I32)[:, None] >= tile_end[None, :], axis=1),
        N_EXPERTS - 1).astype(I32)
    n_used = tile_end[-1:].astype(I32)
    return slot1, slot2, tile_expert, n_used


def kernel(x, c, ada_w, ada_b, ln_g, ln_b, even_w_in, lam_q1, lam_k1, lam_q2, lam_k2, subln_g, pool_w,
           pool_scale, even_w_out, ffn_w_in, ffn_w_out, odd_w_in, lb_raw, gnorm_g, odd_w_out, router_w,
           exp_w_in, exp_w_out):
    _, seq, d = x.shape
    depth = ada_w.shape[0]
    alpha = (2 * depth) ** 0.25
    a_width = d // 2
    a_heads = a_width // A_VDIM
    x2 = x.reshape(seq, d)

    lb_all = jnp.cumsum(jax.nn.softmax(lb_raw.astype(F32), axis=0), axis=0)
    lb_all = lb_all - lb_all[0]
    mod = _ada_mod(c, ada_w, ada_b)

    def mod_parts(l):
        return [mod[l, i * d:(i + 1) * d] for i in range(6)]

    sh1, sc1, g1, sh2, sc2, g2 = mod_parts(0)
    h = _modulate(x2, sc1, sh1)
    qkvu = _proj([h], even_w_in[0], col_off=0, n_cols=3 * a_width + (d - a_width), out_dtype=BF16,
                 epilogue="qscale", q_cols=a_width, q_scale=A_QKDIM ** -0.5 * math.log2(math.e),
                 name="even_in_proj")
    lam_init = 0.8 - 0.6 * math.exp(-0.3 * 0)
    v_t = _attn_value_rows(qkvu[:, 2 * a_width:3 * a_width], a_heads)
    o_a = _diff_attention(qkvu, v_t, lam_q1[0], lam_k1[0], lam_q2[0], lam_k2[0], subln_g[0],
                          n_heads=a_heads, lam_init=lam_init)
    o_b = _multiscale_pool(qkvu, pool_w[0].astype(BF16), pool_scale[0], col_off=3 * a_width)
    y = _proj([o_a, o_b], even_w_out[0], col_off=0, n_cols=d, out_dtype=BF16, name="even_out_proj")
    x2, h = _postnorm(x2, y, g1, ln_g[0, 0], ln_b[0, 0], sc2, sh2, alpha=alpha)

    ff = ffn_w_out.shape[1]
    gact = _swiglu_in(h, ffn_w_in[0])
    y = _mm_acc(gact, ffn_w_out[0], tk=ff // 2, out_dtype=BF16)
    sh1, sc1, g1n, sh2n, sc2n, g2n = mod_parts(1)
    x2, h = _postnorm(x2, y, g2, ln_g[0, 1], ln_b[0, 1], sc1, sh1, alpha=alpha)

    w_odd = odd_w_in[0]
    qs = _proj([h], w_odd, col_off=0, n_cols=d, out_dtype=BF16, epilogue="silu", name="odd_in_q")
    fg = _proj([h], w_odd, col_off=d, n_cols=d, out_dtype=F32, epilogue="forget", lb=lb_all[1], name="odd_in_f")
    vi = _proj([h], w_odd, col_off=2 * d, n_cols=d, out_dtype=BF16, name="odd_in_i")
    gs = _proj([h], w_odd, col_off=3 * d, n_cols=d, out_dtype=BF16, epilogue="silu", name="odd_in_g")
    o_c = _hgrn2(qs, fg, vi, gs, gnorm_g[0])
    y = _proj([o_c], odd_w_out[0], col_off=0, n_cols=d, out_dtype=BF16, name="odd_out_proj")
    x2, h_words = _postnorm(x2, y, g1n, ln_g[1, 0], ln_b[1, 0], sc2n, sh2n, alpha=alpha, pack=True)

    tm_e = 512
    n_tiles = (2 * seq) // tm_e + N_EXPERTS
    info, counts = _router(x2, sc2n, sh2n, router_w[0])
    slot1, slot2, tile_expert, n_used = _moe_plan(info, counts, tm=tm_e, n_tiles=n_tiles)
    xs_words = _dispatch(h_words, slot1, slot2, n_tiles * tm_e)
    gexp = _expert_in(xs_words, exp_w_in[0], tile_expert, n_used, tm=tm_e)
    ys = _expert_out(gexp, exp_w_out[0], tile_expert, n_used, tm=tm_e)
    out = _combine_postnorm(x2, info, ys, slot1, slot2, g2n, ln_g[1, 1], ln_b[1, 1], alpha=alpha)
    return out.reshape(x.shape)
```

```python
import functools
import math

import numpy as np
import jax
import jax.numpy as jnp
from jax import lax
from jax.experimental import pallas as pl
from jax.experimental.pallas import tpu as pltpu

F32 = jnp.float32
BF16 = jnp.bfloat16
I32 = jnp.int32

A_VDIM = 128
A_QKDIM = 64
POOL_WINDOWS = (2, 4, 8, 16)
POOL_HALO = 16
C_KDIM = 128
C_VDIM = 128
HGRN_CHUNK = 64
N_EXPERTS = 8
LN_EPS = 1e-5
RMS_EPS = 1e-6
NEG_INF = -1e30

LANES = 128
VMEM_LIMIT_BYTES = 56 * 1024 * 1024
DMA_ISSUE_UNROLL = 8


def _cparams(*sem):
    return pltpu.CompilerParams(dimension_semantics=sem, vmem_limit_bytes=VMEM_LIMIT_BYTES)


def _silu(x):
    return x * (1.0 / (1.0 + jnp.exp(-x)))


def _dot(a, b):
    return jnp.dot(a, b, preferred_element_type=F32)


def _dot_t(a, b):
    return lax.dot_general(a, b, (((1,), (1,)), ((), ())), preferred_element_type=F32)


def _pack_bf16_pairs(h):
    half = h.shape[1] // 2
    bits = pltpu.bitcast(h.astype(BF16).astype(F32), I32)
    return jnp.bitwise_or(bits[:, :half], lax.shift_right_logical(bits[:, half:], 16))


def _unpack_bf16_pairs(w):
    hi = pltpu.bitcast(jnp.bitwise_and(w, -65536), F32).astype(BF16)
    lo = pltpu.bitcast(lax.shift_left(w, 16), F32).astype(BF16)
    return jnp.concatenate([hi, lo], axis=1)


def _ada_body(c_ref, w_ref, b_ref, o_ref):
    ca = _silu(c_ref[...]).astype(BF16)
    o_ref[...] = _dot(ca, w_ref[...].astype(BF16)) + b_ref[...]


def _ada_mod(c, ada_w, ada_b, *, tn=1024):
    depth, d, n = ada_w.shape
    c8 = jnp.broadcast_to(c.astype(F32), (8, d))
    out = pl.pallas_call(
        _ada_body,
        grid=(depth, n // tn),
        in_specs=[pl.BlockSpec((8, d), lambda l, j: (0, 0)),
                  pl.BlockSpec((None, d, tn), lambda l, j: (l, 0, j)),
                  pl.BlockSpec((None, 1, tn), lambda l, j: (l, 0, j))],
        out_specs=pl.BlockSpec((None, 8, tn), lambda l, j: (l, 0, j)),
        out_shape=jax.ShapeDtypeStruct((depth, 8, n), F32),
        compiler_params=_cparams("arbitrary", "arbitrary"),
        name="ada_mod",
    )(c8, ada_w, ada_b.reshape(depth, 1, n))
    return out[:, 0, :]


def _modulate_body(x_ref, sc_ref, sh_ref, o_ref):
    o_ref[...] = (x_ref[...] * (1.0 + sc_ref[...]) + sh_ref[...]).astype(o_ref.dtype)


def _modulate(x, scale, shift, *, tm=512):
    s, d = x.shape
    vec = pl.BlockSpec((1, d), lambda i: (0, 0))
    return pl.pallas_call(
        _modulate_body,
        grid=(s // tm,),
        in_specs=[pl.BlockSpec((tm, d), lambda i: (i, 0)), vec, vec],
        out_specs=pl.BlockSpec((tm, d), lambda i: (i, 0)),
        out_shape=jax.ShapeDtypeStruct((s, d), BF16),
        compiler_params=_cparams("arbitrary"),
        name="modulate",
    )(x, scale.reshape(1, d), shift.reshape(1, d))


def _layer_norm_rows(r, g, b):
    mu = jnp.mean(r, axis=-1, keepdims=True)
    rc = r - mu
    var = jnp.mean(rc * rc, axis=-1, keepdims=True)
    return rc * lax.rsqrt(var + LN_EPS) * g + b


def _postnorm_body(x_ref, y_ref, gate_ref, g_ref, b_ref, sc_ref, sh_ref, xo_ref, ho_ref, *, alpha, pack):
    r = alpha * x_ref[...] + (1.0 + gate_ref[...]) * y_ref[...].astype(F32)
    xn = _layer_norm_rows(r, g_ref[...], b_ref[...])
    xo_ref[...] = xn
    hn = xn * (1.0 + sc_ref[...]) + sh_ref[...]
    ho_ref[...] = _pack_bf16_pairs(hn) if pack else hn.astype(ho_ref.dtype)


def _postnorm(x, y, gate, g, b, nscale, nshift, *, alpha, pack=False, tm=256):
    s, d = x.shape
    row = pl.BlockSpec((tm, d), lambda i: (i, 0))
    vec = pl.BlockSpec((1, d), lambda i: (0, 0))
    hd, hdt = (d // 2, I32) if pack else (d, BF16)
    return pl.pallas_call(
        functools.partial(_postnorm_body, alpha=alpha, pack=pack),
        grid=(s // tm,),
        in_specs=[row, row, vec, vec, vec, vec, vec],
        out_specs=[row, pl.BlockSpec((tm, hd), lambda i: (i, 0))],
        out_shape=[jax.ShapeDtypeStruct((s, d), F32), jax.ShapeDtypeStruct((s, hd), hdt)],
        compiler_params=_cparams("arbitrary"),
        name="postnorm",
    )(x, y, gate.reshape(1, d), g.reshape(1, d), b.reshape(1, d), nscale.reshape(1, d), nshift.reshape(1, d))


def _proj_body(*refs, n_a, epilogue, tn, q_cols, q_scale):
    a_refs, w_ref, extra = refs[:n_a], refs[n_a], refs[n_a + 1:-2]
    o_ref, wb_s = refs[-2], refs[-1]

    @pl.when(pl.program_id(1) == 0)
    def _():
        wb_s[...] = w_ref[...].astype(BF16)

    acc, off = None, 0
    for a_ref in a_refs:
        kp = a_ref.shape[1]
        part = _dot(a_ref[...], wb_s[off:off + kp, :])
        acc = part if acc is None else acc + part
        off += kp
    if epilogue == "silu":
        acc = _silu(acc)
    elif epilogue == "forget":
        lb = extra[0][...]
        acc = lb + (1.0 - lb) * (1.0 / (1.0 + jnp.exp(-acc)))
    elif epilogue == "qscale":
        acc = acc * jnp.where(pl.program_id(0) * tn < q_cols, q_scale, 1.0)
    o_ref[...] = acc.astype(o_ref.dtype)


def _proj(a_parts, w, *, col_off, n_cols, out_dtype, epilogue="id", lb=None, q_cols=0, q_scale=1.0,
          tm=1024, tn=512, name="proj"):
    m = a_parts[0].shape[0]
    k = w.shape[0]
    joff = col_off // tn
    in_specs = [pl.BlockSpec((tm, a.shape[1]), lambda j, i: (i, 0)) for a in a_parts]
    in_specs.append(pl.BlockSpec((k, tn), lambda j, i: (0, j + joff)))
    args = list(a_parts) + [w]
    if epilogue == "forget":
        in_specs.append(pl.BlockSpec((1, tn), lambda j, i: (0, j)))
        args.append(lb.reshape(1, n_cols))
    return pl.pallas_call(
        functools.partial(_proj_body, n_a=len(a_parts), epilogue=epilogue, tn=tn, q_cols=q_cols, q_scale=q_scale),
        grid=(n_cols // tn, m // tm),
        in_specs=in_specs,
        out_specs=pl.BlockSpec((tm, tn), lambda j, i: (i, j)),
        out_shape=jax.ShapeDtypeStruct((m, n_cols), out_dtype),
        scratch_shapes=[pltpu.VMEM((k, tn), BF16)],
        compiler_params=_cparams("arbitrary", "arbitrary"),
        name=name,
    )(*args)


def _swiglu_in_body(a_ref, wa_ref, wb_ref, o_ref, wa_s, wb_s):
    @pl.when(pl.program_id(1) == 0)
    def _():
        wa_s[...] = wa_ref[...].astype(BF16)
        wb_s[...] = wb_ref[...].astype(BF16)

    a = a_ref[...]
    o_ref[...] = (_silu(_dot(a, wa_s[...])) * _dot(a, wb_s[...])).astype(o_ref.dtype)


def _swiglu_in(a, w, *, tm=1024, tn=256):
    m, k = a.shape
    half = w.shape[1] // 2
    hoff = half // tn
    return pl.pallas_call(
        _swiglu_in_body,
        grid=(half // tn, m // tm),
        in_specs=[pl.BlockSpec((tm, k), lambda j, i: (i, 0)),
                  pl.BlockSpec((k, tn), lambda j, i: (0, j)),
                  pl.BlockSpec((k, tn), lambda j, i: (0, j + hoff))],
        out_specs=pl.BlockSpec((tm, tn), lambda j, i: (i, j)),
        out_shape=jax.ShapeDtypeStruct((m, half), BF16),
        scratch_shapes=[pltpu.VMEM((k, tn), BF16), pltpu.VMEM((k, tn), BF16)],
        compiler_params=_cparams("arbitrary", "arbitrary"),
        name="swiglu_in",
    )(a, w, w)


def _mm_acc_body(a_ref, w_ref, o_ref, acc_ref, *, nk):
    kk = pl.program_id(2)
    part = _dot(a_ref[...], w_ref[...].astype(BF16))

    @pl.when(kk == 0)
    def _():
        acc_ref[...] = part

    @pl.when(kk > 0)
    def _():
        acc_ref[...] += part

    @pl.when(kk == nk - 1)
    def _():
        o_ref[...] = acc_ref[...].astype(o_ref.dtype)


def _mm_acc(a, w, *, tm=1024, tn=512, tk, out_dtype=F32):
    m, k = a.shape
    n = w.shape[1]
    nk = k // tk
    return pl.pallas_call(
        functools.partial(_mm_acc_body, nk=nk),
        grid=(m // tm, n // tn, nk),
        in_specs=[pl.BlockSpec((tm, tk), lambda i, j, kk: (i, kk)),
                  pl.BlockSpec((tk, tn), lambda i, j, kk: (kk, j))],
        out_specs=pl.BlockSpec((tm, tn), lambda i, j, kk: (i, j)),
        out_shape=jax.ShapeDtypeStruct((m, n), out_dtype),
        scratch_shapes=[pltpu.VMEM((tm, tn), F32)],
        compiler_params=_cparams("arbitrary", "arbitrary", "arbitrary"),
        name="mm_acc",
    )(a, w)


ATTN_ONES_ROWS = 16
ATTN_SLAB = 512
ATTN_KEY_ROWS = 512


def _attn_body(qi_tab, kv_tab, slopes, q_ref, k_ref, vt_ref, pos_ref, lq1_ref, lk1_ref, lq2_ref, lk2_ref, g_ref,
               o_ref, q1_s, q2_s, m1_s, a1_s, m2_s, a2_s, jmi_s, *, tq, tk, lam_init):
    h = pl.program_id(0)
    t = pl.program_id(1)
    qi = qi_tab[t]
    kv = kv_tab[t]
    slope = slopes[h]
    n_sub = tq // tk
    dv = A_VDIM
    ks = min(ATTN_KEY_ROWS, tk)
    slab = min(ATTN_SLAB, tq)

    @pl.when(t == 0)
    def _():
        jmi_s[...] = lax.broadcasted_iota(I32, (ks, tq), 0) - lax.broadcasted_iota(I32, (ks, tq), 1)

    @pl.when(kv == 0)
    def _():
        q = q_ref[...]
        lane = lax.broadcasted_iota(I32, q.shape, 1)
        zero = jnp.zeros_like(q)
        sv = jnp.full(q.shape, slope, F32)
        s_hi = sv.astype(BF16).astype(F32)
        s_lo = sv - s_hi
        coef = jnp.where(lane == 0, s_hi * 16.0, jnp.where(lane == 1, s_lo * 16.0,
                         jnp.where(lane == 2, s_hi, jnp.where(lane == 3, s_lo, 0.0)))).astype(BF16)
        q1_s[...] = jnp.concatenate([jnp.where(lane < A_QKDIM, q, zero), coef], axis=1)
        q2_s[...] = jnp.concatenate([jnp.where(lane >= A_QKDIM, q, zero), coef], axis=1)
        for m_s, a_s in ((m1_s, a1_s), (m2_s, a2_s)):
            m_s[...] = jnp.full(m_s.shape, -jnp.inf, F32)
            a_s[...] = jnp.zeros(a_s.shape, F32)

    shift = kv * tk - qi * tq

    def step(tile_off):
        for so in range(0, tk, ks):
            substep(None if tile_off is None else tile_off + so, so)

    def substep(key_off, so):
        k_aug = jnp.concatenate([k_ref[so:so + ks, :], pos_ref[...]], axis=1)
        vt = vt_ref[:, so:so + ks]
        c = slope * (shift + so).astype(F32)
        chains = []
        for c0 in range(0, tq, slab):
            if key_off is not None and c0 + slab - 1 < key_off:
                continue
            masked = key_off is not None and c0 < key_off + ks - 1
            for q_s, m_s, a_s in ((q1_s, m1_s, a1_s), (q2_s, m2_s, a2_s)):
                chains.append((q_s, m_s, a_s, slice(c0, c0 + slab), masked))
        scores = [_dot_t(k_aug, q_s[cols, :]) for q_s, _, _, cols, _ in chains]
        for s, (_, m_s, a_s, cols, masked) in zip(scores, chains):
            if masked:
                s = jnp.where(jmi_s[:, cols] <= -key_off, s, NEG_INF)
            m_prev = m_s[:, cols]
            m_new = jnp.maximum(m_prev, jnp.max(s, axis=0, keepdims=True) + c)
            p = jnp.exp2(s - (m_new - c))
            a_s[:, cols] = jnp.exp2(m_prev - m_new) * a_s[:, cols] + _dot(vt, p.astype(BF16))
            m_s[:, cols] = m_new

    @pl.when(kv < qi * n_sub)
    def _():
        step(None)

    for r in range(n_sub):
        @pl.when(kv == qi * n_sub + r)
        def _():
            step(r * tk)

    @pl.when(kv == (qi + 1) * n_sub - 1)
    def _():
        lam = (jnp.exp(jnp.sum(lq1_ref[...] * lk1_ref[...], axis=-1, keepdims=True))
               - jnp.exp(jnp.sum(lq2_ref[...] * lk2_ref[...], axis=-1, keepdims=True)) + lam_init)
        a1 = a1_s[...]
        a2 = a2_s[...]
        o = a1[:dv] / a1[dv:dv + 1] - lam * (a2[:dv] / a2[dv:dv + 1])
        ms = jnp.mean(o * o, axis=0, keepdims=True)
        on = o * lax.rsqrt(ms + RMS_EPS) * g_ref[...] * (1.0 - lam_init)
        o_ref[...] = on.T.astype(o_ref.dtype)


def _diff_attention(qkvu, v_t, lam_q1, lam_k1, lam_q2, lam_k2, subln_g, *, n_heads, lam_init, tq=2048, tk=2048):
    s = qkvu.shape[0]
    n_sub = tq // tk
    pairs = [(i, j) for i in range(s // tq) for j in range((i + 1) * n_sub)]
    qi_tab = jnp.asarray([p[0] for p in pairs], I32)
    kv_tab = jnp.asarray([p[1] for p in pairs], I32)
    slopes = jnp.asarray([math.log2(math.e) * 2.0 ** (-8.0 * (i + 1) / n_heads) for i in range(n_heads)], F32)
    hd = A_VDIM
    hv = hd + ATTN_ONES_ROWS
    ks = min(ATTN_KEY_ROWS, tk)
    j = np.arange(ks)
    pos = np.zeros((ks, hd), np.float32)
    pos[:, 0] = pos[:, 1] = j // 16
    pos[:, 2] = pos[:, 3] = j % 16
    small = pl.BlockSpec((1, A_QKDIM), lambda h, t, qt, kt, sl: (0, 0))
    grid_spec = pltpu.PrefetchScalarGridSpec(
        num_scalar_prefetch=3,
        grid=(n_heads, len(pairs)),
        in_specs=[pl.BlockSpec((tq, hd), lambda h, t, qt, kt, sl: (qt[t], h)),
                  pl.BlockSpec((tk, hd), lambda h, t, qt, kt, sl: (kt[t], n_heads + h)),
                  pl.BlockSpec((hv, tk), lambda h, t, qt, kt, sl: (h, kt[t])),
                  pl.BlockSpec((ks, hd), lambda h, t, qt, kt, sl: (0, 0)),
                  small, small, small, small,
                  pl.BlockSpec((hd, 1), lambda h, t, qt, kt, sl: (0, 0))],
        out_specs=pl.BlockSpec((tq, hd), lambda h, t, qt, kt, sl: (qt[t], h)),
        scratch_shapes=[pltpu.VMEM((tq, 2 * hd), BF16), pltpu.VMEM((tq, 2 * hd), BF16),
                        pltpu.VMEM((1, tq), F32), pltpu.VMEM((hv, tq), F32),
                        pltpu.VMEM((1, tq), F32), pltpu.VMEM((hv, tq), F32),
                        pltpu.VMEM((ks, tq), I32)])
    return pl.pallas_call(
        functools.partial(_attn_body, tq=tq, tk=tk, lam_init=lam_init),
        grid_spec=grid_spec,
        out_shape=jax.ShapeDtypeStruct((s, n_heads * hd), BF16),
        compiler_params=_cparams("arbitrary", "arbitrary"),
        name="diff_attention",
    )(qi_tab, kv_tab, slopes, qkvu, qkvu, v_t, jnp.asarray(pos, BF16),
      lam_q1.reshape(1, -1), lam_k1.reshape(1, -1), lam_q2.reshape(1, -1), lam_k2.reshape(1, -1),
      subln_g.reshape(-1, 1))


def _attn_value_rows(v, n_heads):
    s = v.shape[0]
    vt = v.T.reshape(n_heads, A_VDIM, s)
    ones = jnp.ones((n_heads, ATTN_ONES_ROWS, s), v.dtype)
    return jnp.concatenate([vt, ones], axis=1).reshape(n_heads * (A_VDIM + ATTN_ONES_ROWS), s)


def _pool_body(ucur_ref, uprev_ref, w_ref, sc_ref, o_ref, *, t_blk):
    g = pl.program_id(0)
    i = pl.program_id(1)
    win = jnp.left_shift(2, g)
    row = lax.broadcasted_iota(I32, (t_blk, t_blk), 0)
    col = lax.broadcasted_iota(I32, (t_blk, t_blk), 1)
    d = row - col
    band = jnp.where(jnp.logical_and(d >= 0, d < win), 1.0, 0.0).astype(BF16)
    rowp = lax.broadcasted_iota(I32, (t_blk, POOL_HALO), 0)
    colp = lax.broadcasted_iota(I32, (t_blk, POOL_HALO), 1)
    dp = rowp + POOL_HALO - colp
    bandp = jnp.where(jnp.logical_and(dp < win, i > 0), 1.0, 0.0).astype(BF16)
    u = ucur_ref[...]
    usum = _dot(band, u) + _dot(bandp, uprev_ref[...])
    tpos = i * t_blk + lax.broadcasted_iota(I32, (t_blk, 1), 0)
    cnt = jnp.minimum(tpos + 1, win).astype(F32)
    dev = usum / cnt - u.astype(F32)
    y = _dot(dev.astype(BF16), w_ref[...]) * sc_ref[...]
    o_ref[...] = y.astype(o_ref.dtype)


def _multiscale_pool(qkvu, pool_w, pool_scale, *, col_off, t_blk=256):
    s = qkvu.shape[0]
    ng, gd, _ = pool_w.shape
    goff = col_off // gd
    hb = t_blk // POOL_HALO
    return pl.pallas_call(
        functools.partial(_pool_body, t_blk=t_blk),
        grid=(ng, s // t_blk),
        in_specs=[pl.BlockSpec((t_blk, gd), lambda g, i: (i, goff + g)),
                  pl.BlockSpec((POOL_HALO, gd), lambda g, i: (jnp.maximum(i * hb - 1, 0), goff + g)),
                  pl.BlockSpec((None, gd, gd), lambda g, i: (g, 0, 0)),
                  pl.BlockSpec((1, gd), lambda g, i: (0, g))],
        out_specs=pl.BlockSpec((t_blk, gd), lambda g, i: (i, g)),
        out_shape=jax.ShapeDtypeStruct((s, ng * gd), BF16),
        compiler_params=_cparams("arbitrary", "arbitrary"),
        name="multiscale_pool",
    )(qkvu, qkvu, pool_w, pool_scale.reshape(1, -1))


def _hgrn_tables():
    c = HGRN_CHUNK
    idx = np.arange(c)
    mats, masks = [], []
    h = c // 2
    while h >= 1:
        upper = (idx % (2 * h)) >= h
        e = idx - (idx % (2 * h)) + h - 1
        u = idx[None, :]
        pq = (upper[:, None] & (u > e[:, None]) & (u <= idx[:, None]))
        pk = ((~upper)[:, None] & (u > idx[:, None]) & (u <= e[:, None]))
        mats.append(pq | pk)
        masks.append(upper[:, None] & (~upper)[None, :] & ((idx[:, None] // (2 * h)) == (idx[None, :] // (2 * h))))
        h //= 2
    masks.append(idx[:, None] == idx[None, :])
    mats.append(idx[None, :] <= idx[:, None])
    mats.append(idx[None, :] > idx[:, None])
    return (np.concatenate(mats, axis=0).astype(np.float32),
            np.stack(masks, axis=0).astype(np.float32))


def _hgrn_body(q_ref, f_ref, v_ref, gs_ref, p_ref, mask_ref, gn_ref, o_ref, st_ref, *, t_blk, n_grp):
    c = HGRN_CHUNK
    n_lvl = mask_ref.shape[0] - 1
    heads = range(n_grp)

    @pl.when(pl.program_id(1) == 0)
    def _():
        st_ref[...] = jnp.zeros(st_ref.shape, F32)

    def col(x, hg):
        return x[:, hg * C_KDIM:(hg + 1) * C_KDIM]

    def stack(parts):
        return jnp.concatenate(parts, axis=0)

    def block_diag(parts):
        zero = jnp.zeros_like(parts[0])
        return stack([jnp.concatenate([parts[hg] if j == hg else zero for j in heads], axis=1) for hg in heads])

    def chunk(ci, carry):
        rows = pl.ds(pl.multiple_of(ci * c, c), c)
        f = f_ref[rows, :]
        lf = jnp.log(f)
        kk = 1.0 - f
        hi = lf.astype(BF16)
        r1 = lf - hi.astype(F32)
        mid = r1.astype(BF16)
        lo = (r1 - mid.astype(F32)).astype(BF16)
        ed = jnp.exp(_dot(p_ref[...], stack([hi, mid, lo])))
        e = [col(ed, hg) for hg in heads]
        q = q_ref[rows, :].astype(F32)
        qh = [col(q, hg) for hg in heads]
        kh = [col(kk, hg) for hg in heads]
        attn = mask_ref[n_lvl] * _dot_t(stack([x.astype(BF16) for x in qh]), stack([x.astype(BF16) for x in kh]))
        for lv in range(n_lvl):
            qs = stack([(qh[hg] * e[hg][lv * c:(lv + 1) * c]).astype(BF16) for hg in heads])
            ks = stack([(kh[hg] * e[hg][lv * c:(lv + 1) * c]).astype(BF16) for hg in heads])
            attn = attn + mask_ref[lv] * _dot_t(qs, ks)
        eb = [e[hg][n_lvl * c:(n_lvl + 1) * c] for hg in heads]
        el = [e[hg][(n_lvl + 1) * c:(n_lvl + 2) * c] for hg in heads]
        v = v_ref[rows, :]
        v_st = stack([col(v, hg) for hg in heads])
        st = st_ref[...]
        q_bd = block_diag([(qh[hg] * eb[hg]).astype(BF16) for hg in heads])
        o = _dot(attn.astype(BF16), v_st) + _dot_t(q_bd, st.astype(BF16))
        k_bd = block_diag([(kh[hg] * el[hg]).astype(BF16) for hg in heads])
        decay = jnp.concatenate([eb[hg][c - 1:c, :] for hg in heads], axis=1)
        st_ref[...] = st * decay + lax.dot_general(v_st, k_bd, (((0,), (0,)), ((), ())),
                                                   preferred_element_type=F32)
        ms = jnp.mean(o * o, axis=-1, keepdims=True)
        on = o * lax.rsqrt(ms + RMS_EPS) * gn_ref[...]
        gs = gs_ref[rows, :].astype(F32)
        for hg in heads:
            o_ref[rows, hg * C_VDIM:(hg + 1) * C_VDIM] = (on[hg * c:(hg + 1) * c] * col(gs, hg)).astype(o_ref.dtype)
        return carry

    lax.fori_loop(0, t_blk // c, chunk, 0, unroll=8)


def _hgrn2(qs, forget, v, gs, gnorm_g, *, t_blk=1024, n_grp=4):
    s, d = qs.shape
    n_heads = d // C_KDIM
    p_np, mask_np = _hgrn_tables()
    p_mat = jnp.asarray(np.concatenate([p_np] * 3, axis=1), BF16)
    masks = jnp.asarray(np.stack([np.kron(np.eye(n_grp, dtype=np.float32), m) for m in mask_np]), F32)
    blk = lambda: pl.BlockSpec((t_blk, n_grp * C_KDIM), lambda h, i: (i, h))
    return pl.pallas_call(
        functools.partial(_hgrn_body, t_blk=t_blk, n_grp=n_grp),
        grid=(n_heads // n_grp, s // t_blk),
        in_specs=[blk(), blk(), blk(), blk(),
                  pl.BlockSpec(p_mat.shape, lambda h, i: (0, 0)),
                  pl.BlockSpec(masks.shape, lambda h, i: (0, 0, 0)),
                  pl.BlockSpec((1, C_VDIM), lambda h, i: (0, 0))],
        out_specs=blk(),
        out_shape=jax.ShapeDtypeStruct((s, d), BF16),
        scratch_shapes=[pltpu.VMEM((C_VDIM, n_grp * C_KDIM), F32)],
        compiler_params=_cparams("arbitrary", "arbitrary"),
        name="hgrn2",
    )(qs, forget, v, gs, p_mat, masks, gnorm_g.reshape(1, -1))


def _router_body(x_ref, sc_ref, sh_ref, whi_ref, wlo_ref, info_ref, cnt_ref, carry_s, *, tm):
    @pl.when(pl.program_id(0) == 0)
    def _():
        carry_s[...] = jnp.zeros(carry_s.shape, F32)

    hmod = x_ref[...] * (1.0 + sc_ref[...]) + sh_ref[...]
    hh = hmod.astype(BF16)
    hl = (hmod - hh.astype(F32)).astype(BF16)
    whi = whi_ref[...]
    logits = _dot(hh, whi) + _dot(hl, whi) + _dot(hh, wlo_ref[...])
    lane = lax.broadcasted_iota(I32, (tm, LANES), 1)
    logits = jnp.where(lane < N_EXPERTS, logits, -jnp.inf)
    m1 = jnp.max(logits, axis=-1, keepdims=True)
    e1 = jnp.min(jnp.where(logits == m1, lane, LANES), axis=-1, keepdims=True)
    rest = jnp.where(lane == e1, -jnp.inf, logits)
    m2 = jnp.max(rest, axis=-1, keepdims=True)
    e2 = jnp.min(jnp.where(rest == m2, lane, LANES), axis=-1, keepdims=True)
    ex = jnp.exp(m2 - m1)
    g1 = 1.0 / (1.0 + ex)
    g2 = ex / (1.0 + ex)
    onehot = jnp.where(jnp.logical_or(lane == e1, lane == e2), 1.0, 0.0)
    row = lax.broadcasted_iota(I32, (tm, tm), 0)
    col = lax.broadcasted_iota(I32, (tm, tm), 1)
    before = jnp.where(row > col, 1.0, 0.0).astype(BF16)
    cum = _dot(before, onehot.astype(BF16)) + carry_s[...]
    r1 = jnp.sum(jnp.where(lane == e1, cum, 0.0), axis=-1, keepdims=True)
    r2 = jnp.sum(jnp.where(lane == e2, cum, 0.0), axis=-1, keepdims=True)
    carry_s[...] = carry_s[...] + jnp.sum(onehot, axis=0, keepdims=True)
    cnt_ref[...] = carry_s[...]
    info = jnp.where(lane == 0, e1.astype(F32), 0.0)
    info = jnp.where(lane == 1, e2.astype(F32), info)
    info = jnp.where(lane == 2, g1, info)
    info = jnp.where(lane == 3, g2, info)
    info = jnp.where(lane == 4, r1, info)
    info = jnp.where(lane == 5, r2, info)
    info_ref[...] = info


def _router(x, scale, shift, router_w, *, tm=256):
    s, d = x.shape
    wpad = jnp.zeros((d, LANES), F32).at[:, :N_EXPERTS].set(router_w)
    whi = wpad.astype(BF16)
    wlo = (wpad - whi.astype(F32)).astype(BF16)
    vec = pl.BlockSpec((1, d), lambda i: (0, 0))
    wspec = pl.BlockSpec((d, LANES), lambda i: (0, 0))
    return pl.pallas_call(
        functools.partial(_router_body, tm=tm),
        grid=(s // tm,),
        in_specs=[pl.BlockSpec((tm, d), lambda i: (i, 0)), vec, vec, wspec, wspec],
        out_specs=[pl.BlockSpec((tm, LANES), lambda i: (i, 0)),
                   pl.BlockSpec((1, LANES), lambda i: (0, 0))],
        out_shape=[jax.ShapeDtypeStruct((s, LANES), F32), jax.ShapeDtypeStruct((1, LANES), F32)],
        scratch_shapes=[pltpu.VMEM((1, LANES), F32)],
        compiler_params=_cparams("arbitrary"),
        name="router",
    )(x, scale.reshape(1, d), shift.reshape(1, d), whi, wlo)


def _dispatch_copies(s1_ref, s2_ref, h_ref, out_ref, sem, base, r):
    t = base + r
    src = h_ref.at[pl.ds(r, 1)]
    return (pltpu.make_async_copy(src, out_ref.at[pl.ds(s1_ref[t], 1)], sem.at[0]),
            pltpu.make_async_copy(src, out_ref.at[pl.ds(s2_ref[t], 1)], sem.at[1]))


def _dispatch_body(s1_ref, s2_ref, h_ref, init_ref, out_ref, sem, *, tb):
    del init_ref
    base = pl.program_id(0) * tb

    def issue(r, carry):
        for prio, cp in enumerate(_dispatch_copies(s1_ref, s2_ref, h_ref, out_ref, sem, base, r)):
            cp.start(priority=prio)
        return carry

    def drain(r, carry):
        for cp in _dispatch_copies(s1_ref, s2_ref, h_ref, out_ref, sem, base, r):
            cp.wait()
        return carry

    lax.fori_loop(0, tb, issue, 0, unroll=DMA_ISSUE_UNROLL)
    lax.fori_loop(0, tb, drain, 0, unroll=DMA_ISSUE_UNROLL)


def _dispatch(h_words, slot1, slot2, n_rows, *, tb=256):
    s, dw = h_words.shape
    init = jnp.zeros((n_rows, dw), h_words.dtype)
    grid_spec = pltpu.PrefetchScalarGridSpec(
        num_scalar_prefetch=2,
        grid=(s // tb,),
        in_specs=[pl.BlockSpec((tb, dw), lambda i, a, b: (i, 0)),
                  pl.BlockSpec(memory_space=pl.ANY)],
        out_specs=pl.BlockSpec(memory_space=pl.ANY),
        scratch_shapes=[pltpu.SemaphoreType.DMA((2,))])
    return pl.pallas_call(
        functools.partial(_dispatch_body, tb=tb),
        grid_spec=grid_spec,
        out_shape=jax.ShapeDtypeStruct((n_rows, dw), h_words.dtype),
        input_output_aliases={3: 0},
        compiler_params=_cparams("arbitrary"),
        name="moe_dispatch",
    )(slot1, slot2, h_words, init)


def _expert_tile_changed(te_ref, i):
    return jnp.logical_or(i == 0, te_ref[i] != te_ref[jnp.maximum(i - 1, 0)])


def _expert_in_body(te_ref, nu_ref, x_ref, wa_ref, wb_ref, o_ref, wa_s, wb_s):
    i = pl.program_id(1)

    @pl.when(_expert_tile_changed(te_ref, i))
    def _():
        wa_s[...] = wa_ref[...].astype(BF16)
        wb_s[...] = wb_ref[...].astype(BF16)

    @pl.when(i < nu_ref[0])
    def _():
        x = _unpack_bf16_pairs(x_ref[...])
        o_ref[...] = (_silu(_dot(x, wa_s[...])) * _dot(x, wb_s[...])).astype(o_ref.dtype)

    @pl.when(i >= nu_ref[0])
    def _():
        o_ref[...] = jnp.zeros(o_ref.shape, o_ref.dtype)


def _expert_out_body(te_ref, nu_ref, x_ref, w_ref, o_ref, w_s):
    i = pl.program_id(1)

    @pl.when(_expert_tile_changed(te_ref, i))
    def _():
        w_s[...] = w_ref[...].astype(BF16)

    @pl.when(i < nu_ref[0])
    def _():
        o_ref[...] = _dot(x_ref[...], w_s[...]).astype(o_ref.dtype)

    @pl.when(i >= nu_ref[0])
    def _():
        o_ref[...] = jnp.zeros(o_ref.shape, o_ref.dtype)


def _expert_in(xs_words, w_in, tile_expert, n_used, *, tm, tn=512):
    p, kw = xs_words.shape
    k = 2 * kw
    half = w_in.shape[2] // 2
    hoff = half // tn
    row = lambda j, i, te, nu: (jnp.minimum(i, nu[0] - 1), 0)
    grid_spec = pltpu.PrefetchScalarGridSpec(
        num_scalar_prefetch=2,
        grid=(half // tn, p // tm),
        in_specs=[pl.BlockSpec((tm, kw), row),
                  pl.BlockSpec((None, k, tn), lambda j, i, te, nu: (te[i], 0, j)),
                  pl.BlockSpec((None, k, tn), lambda j, i, te, nu: (te[i], 0, j + hoff))],
        out_specs=pl.BlockSpec((tm, tn), lambda j, i, te, nu: (i, j)),
        scratch_shapes=[pltpu.VMEM((k, tn), BF16), pltpu.VMEM((k, tn), BF16)])
    return pl.pallas_call(
        _expert_in_body,
        grid_spec=grid_spec,
        out_shape=jax.ShapeDtypeStruct((p, half), BF16),
        compiler_params=_cparams("arbitrary", "arbitrary"),
        name="expert_in",
    )(tile_expert, n_used, xs_words, w_in, w_in)


def _expert_out(gs, w_out, tile_expert, n_used, *, tm, tn=1024):
    p, k = gs.shape
    n = w_out.shape[2]
    row = lambda j, i, te, nu: (jnp.minimum(i, nu[0] - 1), 0)
    grid_spec = pltpu.PrefetchScalarGridSpec(
        num_scalar_prefetch=2,
        grid=(n // tn, p // tm),
        in_specs=[pl.BlockSpec((tm, k), row),
                  pl.BlockSpec((None, k, tn), lambda j, i, te, nu: (te[i], 0, j))],
        out_specs=pl.BlockSpec((tm, tn), lambda j, i, te, nu: (i, j)),
        scratch_shapes=[pltpu.VMEM((k, tn), BF16)])
    return pl.pallas_call(
        _expert_out_body,
        grid_spec=grid_spec,
        out_shape=jax.ShapeDtypeStruct((p, n), F32),
        compiler_params=_cparams("arbitrary", "arbitrary"),
        name="expert_out",
    )(tile_expert, n_used, gs, w_out)


def _combine_copies(s1_ref, s2_ref, ys_ref, buf, sem, blk, slot, r, tb):
    t = blk * tb + r
    return (pltpu.make_async_copy(ys_ref.at[pl.ds(s1_ref[t], 1)], buf.at[slot, 0, pl.ds(r, 1)], sem.at[slot, 0]),
            pltpu.make_async_copy(ys_ref.at[pl.ds(s2_ref[t], 1)], buf.at[slot, 1, pl.ds(r, 1)], sem.at[slot, 1]))


def _combine_body(s1_ref, s2_ref, x_ref, info_ref, ys_ref, gate_ref, g_ref, b_ref, o_ref, buf, sem,
                  *, tb, alpha):
    i = pl.program_id(0)
    slot = lax.rem(i, 2)

    def gather(blk, slt):
        def issue(r, carry):
            for prio, cp in enumerate(_combine_copies(s1_ref, s2_ref, ys_ref, buf, sem, blk, slt, r, tb)):
                cp.start(priority=prio)
            return carry
        lax.fori_loop(0, tb, issue, 0, unroll=DMA_ISSUE_UNROLL)

    @pl.when(i == 0)
    def _():
        gather(i, slot)

    @pl.when(i + 1 < pl.num_programs(0))
    def _():
        gather(i + 1, 1 - slot)

    def drain(r, carry):
        for cp in _combine_copies(s1_ref, s2_ref, ys_ref, buf, sem, i, slot, r, tb):
            cp.wait()
        return carry

    lax.fori_loop(0, tb, drain, 0, unroll=DMA_ISSUE_UNROLL)
    info = info_ref[...]
    y = info[:, 2:3] * buf[slot, 0] + info[:, 3:4] * buf[slot, 1]
    r = alpha * x_ref[...] + (1.0 + gate_ref[...]) * y
    o_ref[...] = _layer_norm_rows(r, g_ref[...], b_ref[...])


def _combine_postnorm(x, info, ys, slot1, slot2, gate, g, b, *, alpha, tb=128):
    s, d = x.shape
    vec = pl.BlockSpec((1, d), lambda i, a, c: (0, 0))
    grid_spec = pltpu.PrefetchScalarGridSpec(
        num_scalar_prefetch=2,
        grid=(s // tb,),
        in_specs=[pl.BlockSpec((tb, d), lambda i, a, c: (i, 0)),
                  pl.BlockSpec((tb, LANES), lambda i, a, c: (i, 0)),
                  pl.BlockSpec(memory_space=pl.ANY),
                  vec, vec, vec],
        out_specs=pl.BlockSpec((tb, d), lambda i, a, c: (i, 0)),
        scratch_shapes=[pltpu.VMEM((2, 2, tb, d), F32), pltpu.SemaphoreType.DMA((2, 2))])
    return pl.pallas_call(
        functools.partial(_combine_body, tb=tb, alpha=alpha),
        grid_spec=grid_spec,
        out_shape=jax.ShapeDtypeStruct((s, d), F32),
        compiler_params=_cparams("arbitrary"),
        name="moe_combine_postnorm",
    )(slot1, slot2, x, info, ys, gate.reshape(1, d), g.reshape(1, d), b.reshape(1, d))


def _moe_plan(info, counts, *, tm, n_tiles):
    e1 = info[:, 0].astype(I32)
    e2 = info[:, 1].astype(I32)
    r1 = info[:, 4].astype(I32)
    r2 = info[:, 5].astype(I32)
    cnt = counts[0, :N_EXPERTS].astype(I32)
    tiles = (cnt + tm - 1) // tm
    tile_end = jnp.cumsum(tiles)
    row_off = (tile_end - tiles) * tm
    slot1 = row_off[e1] + r1
    slot2 = row_off[e2] + r2
    tile_expert = jnp.minimum(
        jnp.sum(jnp.arange(n_tiles, dtype=I32)[:, None] >= tile_end[None, :], axis=1),
        N_EXPERTS - 1).astype(I32)
    n_used = tile_end[-1:].astype(I32)
    return slot1, slot2, tile_expert, n_used


def kernel(x, c, ada_w, ada_b, ln_g, ln_b, even_w_in, lam_q1, lam_k1, lam_q2, lam_k2, subln_g, pool_w,
           pool_scale, even_w_out, ffn_w_in, ffn_w_out, odd_w_in, lb_raw, gnorm_g, odd_w_out, router_w,
           exp_w_in, exp_w_out):
    _, seq, d = x.shape
    depth = ada_w.shape[0]
    alpha = (2 * depth) ** 0.25
    a_width = d // 2
    a_heads = a_width // A_VDIM
    x2 = x.reshape(seq, d)

    lb_all = jnp.cumsum(jax.nn.softmax(lb_raw.astype(F32), axis=0), axis=0)
    lb_all = lb_all - lb_all[0]
    mod = _ada_mod(c, ada_w, ada_b)

    def mod_parts(l):
        return [mod[l, i * d:(i + 1) * d] for i in range(6)]

    sh1, sc1, g1, sh2, sc2, g2 = mod_parts(0)
    h = _modulate(x2, sc1, sh1)
    qkvu = _proj([h], even_w_in[0], col_off=0, n_cols=3 * a_width + (d - a_width), out_dtype=BF16,
                 epilogue="qscale", q_cols=a_width, q_scale=A_QKDIM ** -0.5 * math.log2(math.e),
                 name="even_in_proj")
    lam_init = 0.8 - 0.6 * math.exp(-0.3 * 0)
    v_t = _attn_value_rows(qkvu[:, 2 * a_width:3 * a_width], a_heads)
    o_a = _diff_attention(qkvu, v_t, lam_q1[0], lam_k1[0], lam_q2[0], lam_k2[0], subln_g[0],
                          n_heads=a_heads, lam_init=lam_init)
    o_b = _multiscale_pool(qkvu, pool_w[0].astype(BF16), pool_scale[0], col_off=3 * a_width)
    y = _proj([o_a, o_b], even_w_out[0], col_off=0, n_cols=d, out_dtype=BF16, name="even_out_proj")
    x2, h = _postnorm(x2, y, g1, ln_g[0, 0], ln_b[0, 0], sc2, sh2, alpha=alpha)

    ff = ffn_w_out.shape[1]
    gact = _swiglu_in(h, ffn_w_in[0])
    y = _mm_acc(gact, ffn_w_out[0], tk=ff // 2, out_dtype=BF16)
    sh1, sc1, g1n, sh2n, sc2n, g2n = mod_parts(1)
    x2, h = _postnorm(x2, y, g2, ln_g[0, 1], ln_b[0, 1], sc1, sh1, alpha=alpha)

    w_odd = odd_w_in[0]
    qs = _proj([h], w_odd, col_off=0, n_cols=d, out_dtype=BF16, epilogue="silu", name="odd_in_q")
    fg = _proj([h], w_odd, col_off=d, n_cols=d, out_dtype=F32, epilogue="forget", lb=lb_all[1], name="odd_in_f")
    vi = _proj([h], w_odd, col_off=2 * d, n_cols=d, out_dtype=BF16, name="odd_in_i")
    gs = _proj([h], w_odd, col_off=3 * d, n_cols=d, out_dtype=BF16, epilogue="silu", name="odd_in_g")
    o_c = _hgrn2(qs, fg, vi, gs, gnorm_g[0])
    y = _proj([o_c], odd_w_out[0], col_off=0, n_cols=d, out_dtype=BF16, name="odd_out_proj")
    x2, h_words = _postnorm(x2, y, g1n, ln_g[1, 0], ln_b[1, 0], sc2n, sh2n, alpha=alpha, pack=True)

    tm_e = 512
    n_tiles = (2 * seq) // tm_e + N_EXPERTS
    info, counts = _router(x2, sc2n, sh2n, router_w[0])
    slot1, slot2, tile_expert, n_used = _moe_plan(info, counts, tm=tm_e, n_tiles=n_tiles)
    xs_words = _dispatch(h_words, slot1, slot2, n_tiles * tm_e)
    gexp = _expert_in(xs_words, exp_w_in[0], tile_expert, n_used, tm=tm_e)
    ys = _expert_out(gexp, exp_w_out[0], tile_expert, n_used, tm=tm_e)
    out = _combine_postnorm(x2, info, ys, slot1, slot2, g2n, ln_g[1, 1], ln_b[1, 1], alpha=alpha)
    return out.reshape(x.shape)
```
